```python
import math
import jax, jax.numpy as jnp
from jax import lax
import numpy as np

D_MODEL = 2048
BATCH = 1
SEQ = 8192
DEPTH = 1

CHUNK = 64
MEM_LEN = 256
GM_BLOCK = 128
GM_GROUPS = 8
GM_WIDTH = 1024
GM_GROUP_DIM = GM_WIDTH // GM_GROUPS
SB_HEADS = 8
SB_HEAD_DIM = 128
SB_WIDTH = SB_HEADS * SB_HEAD_DIM
SB_QBLOCK = 128
MIX_WIDTH = GM_WIDTH + SB_WIDTH
IN_WIDTH = 2 * GM_WIDTH + 3 * SB_WIDTH
MEM_HEADS = 4
MEM_HEAD_DIM = D_MODEL // MEM_HEADS
N_GROUPS = 8
EXPERTS_PER_GROUP = 8
N_EXPERTS = N_GROUPS * EXPERTS_PER_GROUP
TOP_K = 2
D_EXPERT = D_MODEL // 4
MOE_BLOCK = 128
DN_ALPHA = (2 * DEPTH) ** 0.25
DN_BETA = (8 * DEPTH) ** -0.25
LN_EPS = 1e-5

kernel_name = "hybrid_gmlp_stickbreak_memxattn_hmoe"


def layer_norm(x, g, b):
    xf = x.astype(jnp.float32)
    mu = jnp.mean(xf, axis=-1, keepdims=True)
    xc = xf - mu
    var = jnp.mean(xc * xc, axis=-1, keepdims=True)
    return (xc * lax.rsqrt(var + LN_EPS) * g.astype(jnp.float32) + b.astype(jnp.float32)).astype(x.dtype)


def chunked_spatial_gating(u, v, ln_g, ln_b, w_s, b_s):
    b, s, _ = u.shape
    n = s // GM_BLOCK
    vg = layer_norm(v.reshape(b, s, GM_GROUPS, GM_GROUP_DIM), ln_g, ln_b)
    vg = vg.reshape(b, n, GM_BLOCK, GM_GROUPS, GM_GROUP_DIM)
    chunk_id = jnp.arange(GM_BLOCK) // CHUNK
    mask = chunk_id[:, None] >= chunk_id[None, :]
    w = jnp.where(mask, w_s, 0)
    mixed = jnp.einsum('gts,bnsgc->bntgc', w, vg) + b_s.T[:, :, None]
    return u * mixed.reshape(b, s, GM_WIDTH)


def stick_breaking_attention(q, k, v):
    b, s, h, d = q.shape
    nq = s // SB_QBLOCK
    qf = (q.astype(jnp.float32) * (d ** -0.5)).transpose(0, 2, 1, 3)
    kf = k.astype(jnp.float32).transpose(0, 2, 1, 3)
    vf = v.astype(jnp.float32).transpose(0, 2, 1, 3)
    q_blocks = qf.reshape(b, h, nq, SB_QBLOCK, d).transpose(2, 0, 1, 3, 4)
    key_pos = jnp.arange(s)

    def block(args):
        qb, i = args
        q_pos = i * SB_QBLOCK + jnp.arange(SB_QBLOCK)
        mask = key_pos[None, :] < q_pos[:, None]
        z = jnp.einsum('bhqd,bhkd->bhqk', qb, kf)
        log_beta = jax.nn.log_sigmoid(z)
        log_stay = jnp.where(mask, log_beta - z, 0.0)
        after = lax.cumsum(log_stay, axis=3, reverse=True) - log_stay
        a = jnp.where(mask, jnp.exp(log_beta + after), 0.0)
        return jnp.einsum('bhqk,bhkd->bhqd', a, vf)

    o = lax.map(block, (q_blocks, jnp.arange(nq)))
    return o.transpose(1, 0, 3, 2, 4).reshape(b, s, h * d).astype(q.dtype)


def memory_cross_attention(h, mem, w_q, w_k, w_v, w_o):
    b, s, dm = h.shape
    m = mem.shape[1]
    q = (h @ w_q).reshape(b, s, MEM_HEADS, MEM_HEAD_DIM).astype(jnp.float32)
    k = (mem @ w_k).reshape(b, m, MEM_HEADS, MEM_HEAD_DIM).astype(jnp.float32)
    v = (mem @ w_v).reshape(b, m, MEM_HEADS, MEM_HEAD_DIM).astype(jnp.float32)
    scores = jnp.einsum('bshd,bmhd->bhsm', q, k) * (MEM_HEAD_DIM ** -0.5)
    p = jax.nn.softmax(scores, axis=-1)
    o = jnp.einsum('bhsm,bmhd->bshd', p, v).astype(h.dtype).reshape(b, s, dm)
    return o @ w_o


def hierarchical_moe(h, w_group, b_group, w_router, b_router, w1, w3, w2):
    b, s, dm = h.shape
    t = b * s
    xf = h.reshape(t, dm)
    g_logits = (xf @ w_group).astype(jnp.float32) + b_group.astype(jnp.float32)
    g_probs = jax.nn.softmax(g_logits, axis=-1)
    g_val, g_idx = lax.top_k(g_probs, 1)
    e_logits_all = jnp.einsum('td,gde->tge', xf, w_router).astype(jnp.float32) + b_router.astype(jnp.float32)
    e_logits = jnp.take_along_axis(e_logits_all, g_idx[:, :, None], axis=1)[:, 0]
    top_val, top_idx = lax.top_k(e_logits, TOP_K)
    gate = jax.nn.softmax(top_val, axis=-1) * g_val
    expert_id = g_idx * EXPERTS_PER_GROUP + top_idx

    n_assign = t * TOP_K
    flat_e = expert_id.reshape(-1)
    flat_tok = jnp.repeat(jnp.arange(t, dtype=jnp.int32), TOP_K)
    flat_gate = gate.reshape(-1)
    order = jnp.argsort(flat_e)
    se, stok, sgate = flat_e[order], flat_tok[order], flat_gate[order]
    counts = jnp.bincount(flat_e, length=N_EXPERTS)
    starts = jnp.cumsum(counts) - counts
    padded = (counts + MOE_BLOCK - 1) // MOE_BLOCK * MOE_BLOCK
    pad_ends = jnp.cumsum(padded)
    pad_starts = pad_ends - padded
    dest = pad_starts[se] + jnp.arange(n_assign) - starts[se]
    n_blocks = -(-n_assign // MOE_BLOCK) + N_EXPERTS
    n_rows = n_blocks * MOE_BLOCK
    row_tok = jnp.full((n_rows,), t, jnp.int32).at[dest].set(stok)
    row_gate = jnp.zeros((n_rows,), jnp.float32).at[dest].set(sgate)
    block_expert = jnp.minimum(
        jnp.searchsorted(pad_ends, jnp.arange(n_blocks) * MOE_BLOCK, side='right'), N_EXPERTS - 1)

    x_pad = jnp.concatenate([xf, jnp.zeros((1, dm), xf.dtype)], axis=0)
    x_rows = x_pad[row_tok].reshape(n_blocks, MOE_BLOCK, dm)

    def expert_block(args):
        xb, e = args
        hid = jax.nn.silu(xb @ w1[e]) * (xb @ w3[e])
        return hid @ w2[e]

    y_rows = lax.map(expert_block, (x_rows, block_expert)).reshape(n_rows, dm)
    y = jnp.zeros((t + 1, dm), jnp.float32).at[row_tok].add(
        y_rows.astype(jnp.float32) * row_gate[:, None])[:t]
    return y.astype(h.dtype).reshape(b, s, dm)


def setup_inputs(seed: int = 0) -> dict:
    key = jax.random.key(seed)
    ks = jax.random.split(key, 26)
    f32 = jnp.float32
    nrm = lambda k, shape, scale: jax.random.normal(k, shape, f32) * scale
    col_scale = jnp.concatenate([
        jnp.full((2 * GM_WIDTH,), DN_BETA, f32),
        jnp.ones((2 * SB_WIDTH,), f32),
        jnp.full((SB_WIDTH,), DN_BETA, f32)])
    return {
        "x": nrm(ks[0], (BATCH, SEQ, D_MODEL), 1.0),
        "mem": nrm(ks[1], (BATCH, MEM_LEN, D_MODEL), 1.0),
        "w_in": nrm(ks[2], (DEPTH, D_MODEL, IN_WIDTH), D_MODEL ** -0.5) * col_scale,
        "gm_ln_g": 1.0 + nrm(ks[3], (DEPTH, GM_GROUPS, GM_GROUP_DIM), 0.02),
        "gm_ln_b": nrm(ks[4], (DEPTH, GM_GROUPS, GM_GROUP_DIM), 0.02),
        "gm_w_s": nrm(ks[5], (DEPTH, GM_GROUPS, GM_BLOCK, GM_BLOCK), GM_BLOCK ** -0.5),
        "gm_b_s": 1.0 + nrm(ks[6], (DEPTH, GM_GROUPS, GM_BLOCK), 0.01),
        "w_mix_out": nrm(ks[7], (DEPTH, MIX_WIDTH, D_MODEL), MIX_WIDTH ** -0.5 * DN_BETA),
        "ln1_g": 1.0 + nrm(ks[8], (DEPTH, D_MODEL), 0.02),
        "ln1_b": nrm(ks[9], (DEPTH, D_MODEL), 0.02),
        "mem_w_q": nrm(ks[10], (DEPTH, D_MODEL, D_MODEL), D_MODEL ** -0.5),
        "mem_w_k": nrm(ks[11], (DEPTH, D_MODEL, D_MODEL), D_MODEL ** -0.5),
        "mem_w_v": nrm(ks[12], (DEPTH, D_MODEL, D_MODEL), D_MODEL ** -0.5 * DN_BETA),
        "mem_w_o": nrm(ks[13], (DEPTH, D_MODEL, D_MODEL), D_MODEL ** -0.5 * DN_BETA),
        "ln2_g": 1.0 + nrm(ks[14], (DEPTH, D_MODEL), 0.02),
        "ln2_b": nrm(ks[15], (DEPTH, D_MODEL), 0.02),
        "w_group": nrm(ks[16], (DEPTH, D_MODEL, N_GROUPS), D_MODEL ** -0.5),
        "b_group": nrm(ks[17], (DEPTH, N_GROUPS), 0.01),
        "w_router": nrm(ks[18], (DEPTH, N_GROUPS, D_MODEL, EXPERTS_PER_GROUP), D_MODEL ** -0.5),
        "b_router": nrm(ks[19], (DEPTH, N_GROUPS, EXPERTS_PER_GROUP), 0.01),
        "w1": nrm(ks[20], (DEPTH, N_EXPERTS, D_MODEL, D_EXPERT), D_MODEL ** -0.5 * DN_BETA),
        "w3": nrm(ks[21], (DEPTH, N_EXPERTS, D_MODEL, D_EXPERT), D_MODEL ** -0.5 * DN_BETA),
        "w2": nrm(ks[22], (DEPTH, N_EXPERTS, D_EXPERT, D_MODEL), D_EXPERT ** -0.5 * DN_BETA),
        "ln3_g": 1.0 + nrm(ks[23], (DEPTH, D_MODEL), 0.02),
        "ln3_b": nrm(ks[24], (DEPTH, D_MODEL), 0.02),
    }


def reference(x, mem, w_in, gm_ln_g, gm_ln_b, gm_w_s, gm_b_s, w_mix_out, ln1_g, ln1_b,
              mem_w_q, mem_w_k, mem_w_v, mem_w_o, ln2_g, ln2_b,
              w_group, b_group, w_router, b_router, w1, w3, w2, ln3_g, ln3_b):
    b, s, dm = x.shape
    h = x
    for l in range(DEPTH):
        proj = h @ w_in[l]
        u_a, v_a, q, k, v = jnp.split(
            proj, [GM_WIDTH, 2 * GM_WIDTH, 2 * GM_WIDTH + SB_WIDTH, 2 * GM_WIDTH + 2 * SB_WIDTH], axis=-1)
        mix_a = chunked_spatial_gating(jax.nn.gelu(u_a), jax.nn.gelu(v_a),
                                       gm_ln_g[l], gm_ln_b[l], gm_w_s[l], gm_b_s[l])
        mix_b = stick_breaking_attention(q.reshape(b, s, SB_HEADS, SB_HEAD_DIM),
                                         k.reshape(b, s, SB_HEADS, SB_HEAD_DIM),
                                         v.reshape(b, s, SB_HEADS, SB_HEAD_DIM))
        mixed = jnp.concatenate([mix_a, mix_b], axis=-1) @ w_mix_out[l]
        h = layer_norm(DN_ALPHA * h + mixed, ln1_g[l], ln1_b[l])
        h = layer_norm(DN_ALPHA * h + memory_cross_attention(h, mem, mem_w_q[l], mem_w_k[l],
                                                             mem_w_v[l], mem_w_o[l]),
                       ln2_g[l], ln2_b[l])
        h = layer_norm(DN_ALPHA * h + hierarchical_moe(h, w_group[l], b_group[l], w_router[l],
                                                       b_router[l], w1[l], w3[l], w2[l]),
                       ln3_g[l], ln3_b[l])
    return h
```

```python
import functools
import math

import jax
import jax.numpy as jnp
from jax import lax
from jax.experimental import pallas as pl
from jax.experimental.pallas import tpu as pltpu

D_MODEL = 2048
SEQ = 8192
CHUNK = 64
MEM_LEN = 256
GM_BLOCK = 128
GM_GROUPS = 8
GM_WIDTH = 1024
SB_HEADS = 8
SB_HEAD_DIM = 128
SB_WIDTH = 1024
MEM_HEADS = 4
MEM_HEAD_DIM = 512
N_GROUPS = 8
EXPERTS_PER_GROUP = 8
N_EXPERTS = 64
TOP_K = 2
D_EXPERT = 512
DN_ALPHA = 2.0 ** 0.25
LN_EPS = 1e-5

LANES = 128
MOE_ROWS = 128
N_ASSIGN = SEQ * TOP_K
N_ROW_BLOCKS = N_ASSIGN // MOE_ROWS + N_EXPERTS
N_ROWS = N_ROW_BLOCKS * MOE_ROWS
SB_UNDERFLOW = 110.0
VMEM_LIMIT = 56 * 1024 * 1024

BF16 = jnp.bfloat16
F32 = jnp.float32


def _cparams(sem):
    return pltpu.CompilerParams(dimension_semantics=sem, vmem_limit_bytes=VMEM_LIMIT)


def _layer_norm(y, g, b):
    mu = jnp.mean(y, axis=-1, keepdims=True)
    yc = y - mu
    var = jnp.mean(yc * yc, axis=-1, keepdims=True)
    return yc * lax.rsqrt(var + LN_EPS) * g + b


def _proj_nn_kernel(x_ref, w_ref, o_ref, xb_ref):
    @pl.when(pl.program_id(1) == 0)
    def _():
        xb_ref[...] = x_ref[...].astype(BF16)

    o_ref[...] = jnp.dot(xb_ref[...], w_ref[...], preferred_element_type=F32).astype(o_ref.dtype)


def _proj_nn(x, w, tm, tn):
    m, k = x.shape
    n = w.shape[1]
    return pl.pallas_call(
        _proj_nn_kernel,
        grid=(m // tm, n // tn),
        in_specs=[pl.BlockSpec((tm, k), lambda i, j: (i, 0)),
                  pl.BlockSpec((k, tn), lambda i, j: (0, j))],
        out_specs=pl.BlockSpec((tm, tn), lambda i, j: (i, j)),
        out_shape=jax.ShapeDtypeStruct((m, n), BF16),
        scratch_shapes=[pltpu.VMEM((tm, k), BF16)],
        compiler_params=_cparams(("arbitrary", "arbitrary")),
        name="proj_nn",
    )(x, w)


def _proj_nt_kernel(x_ref, wt_ref, o_ref, xb_ref, *, q_scale):
    j = pl.program_id(1)

    @pl.when(j == 0)
    def _():
        xb_ref[...] = x_ref[...].astype(BF16)

    acc = lax.dot_general(wt_ref[...], xb_ref[...], (((1,), (1,)), ((), ())),
                          preferred_element_type=F32)
    scale = jnp.where(j == 0, q_scale, 1.0).astype(F32)
    o_ref[...] = (acc * scale).astype(o_ref.dtype)


def _proj_nt(x, wt, tm, tr, q_scale):
    m, k = x.shape
    r = wt.shape[0]
    return pl.pallas_call(
        functools.partial(_proj_nt_kernel, q_scale=q_scale),
        grid=(m // tm, r // tr),
        in_specs=[pl.BlockSpec((tm, k), lambda i, j: (i, 0)),
                  pl.BlockSpec((tr, k), lambda i, j: (j, 0))],
        out_specs=pl.BlockSpec((tr, tm), lambda i, j: (j, i)),
        out_shape=jax.ShapeDtypeStruct((r, m), BF16),
        scratch_shapes=[pltpu.VMEM((tm, k), BF16)],
        compiler_params=_cparams(("arbitrary", "arbitrary")),
        name="proj_nt",
    )(x, wt)


def _gmlp_kernel(u_ref, v_ref, ws_ref, g_ref, b_ref, bs_ref, o_ref, *, tg):
    t_chunk = lax.broadcasted_iota(jnp.int32, (GM_BLOCK, GM_BLOCK), 0) // CHUNK
    s_chunk = lax.broadcasted_iota(jnp.int32, (GM_BLOCK, GM_BLOCK), 1) // CHUNK
    causal = t_chunk >= s_chunk
    for g in range(GM_GROUPS):
        wm = jnp.where(causal, ws_ref[g], 0.0).astype(BF16)
        ln_g = g_ref[g:g + 1, :]
        ln_b = b_ref[g:g + 1, :]
        bias = bs_ref[:, g:g + 1]
        cols = slice(g * LANES, (g + 1) * LANES)
        for n in range(tg // GM_BLOCK):
            rows = slice(n * GM_BLOCK, (n + 1) * GM_BLOCK)
            v = jax.nn.gelu(v_ref[rows, cols].astype(F32))
            vg = _layer_norm(v, ln_g, ln_b).astype(BF16)
            mixed = jnp.dot(wm, vg, preferred_element_type=F32) + bias
            u = jax.nn.gelu(u_ref[rows, cols].astype(F32))
            o_ref[rows, cols] = (u * mixed).astype(o_ref.dtype)


def _gmlp(uvk, gm_w_s, gm_ln_g, gm_ln_b, gm_b_s_t, tg):
    s = uvk.shape[0]
    return pl.pallas_call(
        functools.partial(_gmlp_kernel, tg=tg),
        grid=(s // tg,),
        in_specs=[pl.BlockSpec((tg, GM_WIDTH), lambda i: (i, 0)),
                  pl.BlockSpec((tg, GM_WIDTH), lambda i: (i, 1)),
                  pl.BlockSpec((GM_GROUPS, GM_BLOCK, GM_BLOCK), lambda i: (0, 0, 0)),
                  pl.BlockSpec((GM_GROUPS, LANES), lambda i: (0, 0)),
                  pl.BlockSpec((GM_GROUPS, LANES), lambda i: (0, 0)),
                  pl.BlockSpec((GM_BLOCK, GM_GROUPS), lambda i: (0, 0))],
        out_specs=pl.BlockSpec((tg, GM_WIDTH), lambda i: (i, 0)),
        out_shape=jax.ShapeDtypeStruct((s, GM_WIDTH), BF16),
        compiler_params=_cparams(("arbitrary",)),
        name="gmlp",
    )(uvk, uvk, gm_w_s, gm_ln_g, gm_ln_b, gm_b_s_t)


def _softplus(z):
    return jnp.maximum(z, 0.0) + jnp.log(1.0 + jnp.exp(-jnp.abs(z)))


def _sb_kernel(qt_ref, k_ref, vt_ref, o_ref, *, tq, bk):
    i = pl.program_id(1)
    q0 = i * tq
    qt = qt_ref[...]
    upper = (lax.broadcasted_iota(jnp.int32, (bk, bk), 1)
             >= lax.broadcasted_iota(jnp.int32, (bk, bk), 0)).astype(BF16)

    def suffix_sum(sp):
        hi = sp.astype(BF16)
        lo = (sp - hi.astype(F32)).astype(BF16)
        return (jnp.dot(upper, hi, preferred_element_type=F32)
                + jnp.dot(upper, lo, preferred_element_type=F32))

    def tile(ks, carry, acc, mask):
        kb = k_ref[pl.ds(ks, bk), :]
        z = jnp.dot(kb, qt, preferred_element_type=F32)
        sp = _softplus(z)
        if mask is not None:
            sp = jnp.where(mask, sp, 0.0)
        csum = suffix_sum(sp) + carry
        a = jnp.exp(z - csum)
        if mask is not None:
            a = jnp.where(mask, a, 0.0)
        vt = vt_ref[:, pl.ds(ks, bk)]
        acc = acc + jnp.dot(vt, a.astype(BF16), preferred_element_type=F32)
        return csum[0:1, :], acc

    carry = jnp.zeros((1, tq), F32)
    acc = jnp.zeros((SB_HEAD_DIM, tq), F32)
    key_in_tile = lax.broadcasted_iota(jnp.int32, (bk, tq), 0)
    q_in_tile = lax.broadcasted_iota(jnp.int32, (bk, tq), 1)
    for d in reversed(range(tq // bk)):
        mask = (key_in_tile + d * bk) < q_in_tile
        carry, acc = tile(pl.multiple_of(q0 + d * bk, bk), carry, acc, mask)

    def cond(state):
        j, carry, _ = state
        return jnp.logical_and(j >= 0, jnp.min(carry) < SB_UNDERFLOW)

    def body(state):
        j, carry, acc = state
        carry, acc = tile(pl.multiple_of(j * bk, bk), carry, acc, None)
        return j - 1, carry, acc

    _, _, acc = lax.while_loop(cond, body, (q0 // bk - 1, carry, acc))
    o_ref[...] = acc.T.astype(o_ref.dtype)


def _stick_breaking(qvt, uvk, tq, bk):
    s = uvk.shape[0]
    k_col0 = 2 * GM_WIDTH // SB_HEAD_DIM
    return pl.pallas_call(
        functools.partial(_sb_kernel, tq=tq, bk=bk),
        grid=(SB_HEADS, s // tq),
        in_specs=[pl.BlockSpec((SB_HEAD_DIM, tq), lambda h, i: (h, i)),
                  pl.BlockSpec((s, SB_HEAD_DIM), lambda h, i: (0, k_col0 + h)),
                  pl.BlockSpec((SB_HEAD_DIM, s), lambda h, i: (SB_HEADS + h, 0))],
        out_specs=pl.BlockSpec((tq, SB_HEAD_DIM), lambda h, i: (i, h)),
        out_shape=jax.ShapeDtypeStruct((s, SB_WIDTH), BF16),
        compiler_params=_cparams(("arbitrary", "arbitrary")),
        name="stick_breaking",
    )(qvt, uvk, qvt)


def _mixout_kernel(a_ref, b_ref, wa_ref, wb_ref, x_ref, g_ref, beta_ref, o_ref):
    mixed = (jnp.dot(a_ref[...], wa_ref[...], preferred_element_type=F32)
             + jnp.dot(b_ref[...], wb_ref[...], preferred_element_type=F32))
    y = DN_ALPHA * x_ref[...] + mixed
    o_ref[...] = _layer_norm(y, g_ref[...], beta_ref[...])


def _mixout(mix_a, mix_b, w, x, g, b, tm):
    s = x.shape[0]
    return pl.pallas_call(
        _mixout_kernel,
        grid=(s // tm,),
        in_specs=[pl.BlockSpec((tm, GM_WIDTH), lambda i: (i, 0)),
                  pl.BlockSpec((tm, SB_WIDTH), lambda i: (i, 0)),
                  pl.BlockSpec((GM_WIDTH, D_MODEL), lambda i: (0, 0)),
                  pl.BlockSpec((SB_WIDTH, D_MODEL), lambda i: (1, 0)),
                  pl.BlockSpec((tm, D_MODEL), lambda i: (i, 0)),
                  pl.BlockSpec((1, D_MODEL), lambda i: (0, 0)),
                  pl.BlockSpec((1, D_MODEL), lambda i: (0, 0))],
        out_specs=pl.BlockSpec((tm, D_MODEL), lambda i: (i, 0)),
        out_shape=jax.ShapeDtypeStruct((s, D_MODEL), F32),
        compiler_params=_cparams(("arbitrary",)),
        name="mixout_ln1",
    )(mix_a, mix_b, w, w, x, g, b)


def _memattn_kernel(h_ref, wq_ref, kv_ref, wo_ref, g_ref, beta_ref, o_ref):
    h = h_ref[...]
    q = jnp.dot(h.astype(BF16), wq_ref[...], preferred_element_type=F32).astype(BF16)
    outs = []
    for hd in range(MEM_HEADS):
        cols = slice(hd * MEM_HEAD_DIM, (hd + 1) * MEM_HEAD_DIM)
        kh = kv_ref[:, cols]
        vh = kv_ref[:, D_MODEL + hd * MEM_HEAD_DIM:D_MODEL + (hd + 1) * MEM_HEAD_DIM]
        s = lax.dot_general(q[:, cols], kh, (((1,), (1,)), ((), ())),
                            preferred_element_type=F32) * (MEM_HEAD_DIM ** -0.5)
        s = s - jnp.max(s, axis=-1, keepdims=True)
        e = jnp.exp(s)
        p = e / jnp.sum(e, axis=-1, keepdims=True)
        outs.append(jnp.dot(p.astype(BF16), vh, preferred_element_type=F32).astype(BF16))
    o = jnp.concatenate(outs, axis=-1)
    attn = jnp.dot(o, wo_ref[...], preferred_element_type=F32)
    o_ref[...] = _layer_norm(DN_ALPHA * h + attn, g_ref[...], beta_ref[...])


def _memattn(h1, wq, kv, wo, g, b, tm):
    s = h1.shape[0]
    return pl.pallas_call(
        _memattn_kernel,
        grid=(s // tm,),
        in_specs=[pl.BlockSpec((tm, D_MODEL), lambda i: (i, 0)),
                  pl.BlockSpec((D_MODEL, D_MODEL), lambda i: (0, 0)),
                  pl.BlockSpec((MEM_LEN, 2 * D_MODEL), lambda i: (0, 0)),
                  pl.BlockSpec((D_MODEL, D_MODEL), lambda i: (0, 0)),
                  pl.BlockSpec((1, D_MODEL), lambda i: (0, 0)),
                  pl.BlockSpec((1, D_MODEL), lambda i: (0, 0))],
        out_specs=pl.BlockSpec((tm, D_MODEL), lambda i: (i, 0)),
        out_shape=jax.ShapeDtypeStruct((s, D_MODEL), F32),
        compiler_params=_cparams(("arbitrary",)),
        name="memattn_ln2",
    )(h1, wq, kv, wo, g, b)


def _router_kernel(h_ref, w_ref, b_ref, o_ref):
    logits = jnp.dot(h_ref[...], w_ref[...], preferred_element_type=F32,
                     precision=lax.Precision.HIGHEST) + b_ref[...]
    lane = lax.broadcasted_iota(jnp.int32, logits.shape, 1)
    neg = jnp.float32(-jnp.inf)
    big = jnp.int32(1 << 20)

    def first_argmax(vals):
        m = jnp.max(vals, axis=-1, keepdims=True)
        idx = jnp.min(jnp.where(vals == m, lane, big), axis=-1, keepdims=True)
        return m, idx

    is_group = jnp.logical_and(lane >= N_EXPERTS, lane < N_EXPERTS + N_GROUPS)
    gl = jnp.where(is_group, logits, neg)
    gmax, glane = first_argmax(gl)
    g_val = 1.0 / jnp.sum(jnp.exp(gl - gmax), axis=-1, keepdims=True)
    gidx = glane - N_EXPERTS
    in_group = jnp.logical_and(lane >= gidx * EXPERTS_PER_GROUP,
                               lane < (gidx + 1) * EXPERTS_PER_GROUP)
    el = jnp.where(in_group, logits, neg)
    m1, i1 = first_argmax(el)
    m2, i2 = first_argmax(jnp.where(lane == i1, neg, el))
    e2 = jnp.exp(m2 - m1)
    gate1 = g_val / (1.0 + e2)
    gate2 = g_val * e2 / (1.0 + e2)
    out = jnp.where(lane == 0, i1.astype(F32),
                    jnp.where(lane == 1, i2.astype(F32),
                              jnp.where(lane == 2, gate1,
                                        jnp.where(lane == 3, gate2, 0.0))))
    o_ref[...] = out


def _router(h2, w_r, b_r, tm):
    s = h2.shape[0]
    return pl.pallas_call(
        _router_kernel,
        grid=(s // tm,),
        in_specs=[pl.BlockSpec((tm, D_MODEL), lambda i: (i, 0)),
                  pl.BlockSpec((D_MODEL, LANES), lambda i: (0, 0)),
                  pl.BlockSpec((1, LANES), lambda i: (0, 0))],
        out_specs=pl.BlockSpec((tm, LANES), lambda i: (i, 0)),
        out_shape=jax.ShapeDtypeStruct((s, LANES), F32),
        compiler_params=_cparams(("arbitrary",)),
        name="router",
    )(h2, w_r, b_r)


def _expert_kernel(be_ref, nvalid_ref, tok_ref, h_hbm, w1_ref, w3_ref, w2_ref, gate_ref,
                   o_ref, xbuf, sem, w1b, w3b, w2b):
    i = pl.program_id(0)
    nvalid = nvalid_ref[0]
    slot = i % 2

    def start_gather(blk, slot_):
        def issue(r, c):
            tok = tok_ref[blk * MOE_ROWS + r]
            pltpu.make_async_copy(h_hbm.at[pl.ds(tok, 1)], xbuf.at[slot_, pl.ds(r, 1)],
                                  sem.at[slot_]).start()
            return c
        lax.fori_loop(0, MOE_ROWS, issue, 0, unroll=8)

    @pl.when(jnp.logical_and(i == 0, nvalid > 0))
    def _():
        start_gather(0, 0)

    @pl.when(i + 1 < nvalid)
    def _():
        start_gather(i + 1, 1 - slot)

    prev = be_ref[jnp.maximum(i - 1, 0)]

    @pl.when(jnp.logical_or(i == 0, be_ref[i] != prev))
    def _():
        w1b[...] = w1_ref[0].astype(BF16)
        w3b[...] = w3_ref[0].astype(BF16)
        w2b[...] = w2_ref[0].astype(BF16)

    @pl.when(i < nvalid)
    def _():
        pltpu.make_async_copy(h_hbm.at[pl.ds(0, MOE_ROWS)], xbuf.at[slot], sem.at[slot]).wait()
        xb = xbuf[slot].astype(BF16)
        a = jnp.dot(xb, w1b[...], preferred_element_type=F32)
        b = jnp.dot(xb, w3b[...], preferred_element_type=F32)
        hid = (a * jax.nn.sigmoid(a) * b).astype(BF16)
        y = jnp.dot(hid, w2b[...], preferred_element_type=F32)
        o_ref[...] = y * gate_ref[...]

    @pl.when(i >= nvalid)
    def _():
        o_ref[...] = jnp.zeros_like(o_ref)


def _experts(block_expert, nvalid, row_tok, h2, w1, w3, w2, row_gate):
    grid_spec = pltpu.PrefetchScalarGridSpec(
        num_scalar_prefetch=3,
        grid=(N_ROW_BLOCKS,),
        in_specs=[pl.BlockSpec(memory_space=pl.ANY),
                  pl.BlockSpec((1, D_MODEL, D_EXPERT), lambda i, be, nv, tok: (be[i], 0, 0)),
                  pl.BlockSpec((1, D_MODEL, D_EXPERT), lambda i, be, nv, tok: (be[i], 0, 0)),
                  pl.BlockSpec((1, D_EXPERT, D_MODEL), lambda i, be, nv, tok: (be[i], 0, 0)),
                  pl.BlockSpec((MOE_ROWS, 1), lambda i, be, nv, tok: (i, 0))],
        out_specs=pl.BlockSpec((MOE_ROWS, D_MODEL), lambda i, be, nv, tok: (i, 0)),
        scratch_shapes=[pltpu.VMEM((2, MOE_ROWS, D_MODEL), F32),
                        pltpu.SemaphoreType.DMA((2,)),
                        pltpu.VMEM((D_MODEL, D_EXPERT), BF16),
                        pltpu.VMEM((D_MODEL, D_EXPERT), BF16),
                        pltpu.VMEM((D_EXPERT, D_MODEL), BF16)],
    )
    return pl.pallas_call(
        _expert_kernel,
        grid_spec=grid_spec,
        out_shape=jax.ShapeDtypeStruct((N_ROWS, D_MODEL), F32),
        compiler_params=_cparams(("arbitrary",)),
        name="experts",
    )(block_expert, nvalid, row_tok, h2, w1, w3, w2, row_gate)


def _combine_kernel(pos_ref, y_hbm, h_ref, g_ref, beta_ref, o_ref, ybuf, sem, *, tc):
    i = pl.program_id(0)
    n = pl.num_programs(0)
    slot = i % 2

    def start_gather(blk, slot_):
        def issue(r, c):
            for kk in range(TOP_K):
                p = pos_ref[(blk * tc + r) * TOP_K + kk]
                pltpu.make_async_copy(y_hbm.at[pl.ds(p, 1)], ybuf.at[slot_, kk, pl.ds(r, 1)],
                                      sem.at[slot_]).start()
            return c
        lax.fori_loop(0, tc, issue, 0, unroll=8)

    @pl.when(i == 0)
    def _():
        start_gather(0, 0)

    @pl.when(i + 1 < n)
    def _():
        start_gather(i + 1, 1 - slot)

    for kk in range(TOP_K):
        pltpu.make_async_copy(y_hbm.at[pl.ds(0, tc)], ybuf.at[slot, kk], sem.at[slot]).wait()
    y = DN_ALPHA * h_ref[...] + (ybuf[slot, 0] + ybuf[slot, 1])
    o_ref[...] = _layer_norm(y, g_ref[...], beta_ref[...])


def _combine(pos, y_rows, h2, g, b, tc):
    s = h2.shape[0]
    grid_spec = pltpu.PrefetchScalarGridSpec(
        num_scalar_prefetch=1,
        grid=(s // tc,),
        in_specs=[pl.BlockSpec(memory_space=pl.ANY),
                  pl.BlockSpec((tc, D_MODEL), lambda i, pos: (i, 0)),
                  pl.BlockSpec((1, D_MODEL), lambda i, pos: (0, 0)),
                  pl.BlockSpec((1, D_MODEL), lambda i, pos: (0, 0))],
        out_specs=pl.BlockSpec((tc, D_MODEL), lambda i, pos: (i, 0)),
        scratch_shapes=[pltpu.VMEM((2, TOP_K, tc, D_MODEL), F32),
                        pltpu.SemaphoreType.DMA((2,))],
    )
    return pl.pallas_call(
        functools.partial(_combine_kernel, tc=tc),
        grid_spec=grid_spec,
        out_shape=jax.ShapeDtypeStruct((s, D_MODEL), F32),
        compiler_params=_cparams(("arbitrary",)),
        name="combine_ln3",
    )(pos, y_rows, h2, g, b)


def _routing_tables(expert_id, gate):
    flat_e = expert_id.reshape(-1)
    flat_tok = jnp.repeat(jnp.arange(SEQ, dtype=jnp.int32), TOP_K)
    flat_gate = gate.reshape(-1)
    order = jnp.argsort(flat_e, stable=True)
    se = flat_e[order]
    counts = jnp.bincount(flat_e, length=N_EXPERTS).astype(jnp.int32)
    starts = jnp.cumsum(counts) - counts
    padded = (counts + MOE_ROWS - 1) // MOE_ROWS * MOE_ROWS
    pad_ends = jnp.cumsum(padded)
    pad_starts = pad_ends - padded
    dest = (pad_starts[se] + jnp.arange(N_ASSIGN, dtype=jnp.int32) - starts[se]).astype(jnp.int32)
    row_tok = jnp.zeros((N_ROWS,), jnp.int32).at[dest].set(flat_tok[order])
    row_gate = jnp.zeros((N_ROWS,), F32).at[dest].set(flat_gate[order])
    pos = jnp.zeros((N_ASSIGN,), jnp.int32).at[order].set(dest)
    block_expert = jnp.minimum(
        jnp.searchsorted(pad_ends, jnp.arange(N_ROW_BLOCKS, dtype=jnp.int32) * MOE_ROWS,
                         side='right'), N_EXPERTS - 1).astype(jnp.int32)
    nvalid = (pad_ends[-1] // MOE_ROWS).astype(jnp.int32).reshape(1)
    return block_expert, nvalid, row_tok, row_gate.reshape(N_ROWS, 1), pos


def kernel(x, mem, w_in, gm_ln_g, gm_ln_b, gm_w_s, gm_b_s, w_mix_out, ln1_g, ln1_b,
           mem_w_q, mem_w_k, mem_w_v, mem_w_o, ln2_g, ln2_b,
           w_group, b_group, w_router, b_router, w1, w3, w2, ln3_g, ln3_b):
    assert x.shape == (1, SEQ, D_MODEL) and w_in.shape[0] == 1
    xs = x[0]
    l = 0
    w_in_l = w_in[l]
    q_lo, k_lo, v_lo = 2 * GM_WIDTH, 2 * GM_WIDTH + SB_WIDTH, 2 * GM_WIDTH + 2 * SB_WIDTH
    w_uvk = jnp.concatenate([w_in_l[:, :q_lo], w_in_l[:, k_lo:v_lo]], axis=1).astype(BF16)
    w_qv_t = jnp.concatenate([w_in_l[:, q_lo:k_lo], w_in_l[:, v_lo:]], axis=1).T.astype(BF16)
    w_mix = w_mix_out[l].astype(BF16)
    w_q = mem_w_q[l].astype(BF16)
    w_kv = jnp.concatenate([mem_w_k[l], mem_w_v[l]], axis=1).astype(BF16)
    w_o = mem_w_o[l].astype(BF16)
    w_r = jnp.concatenate(
        [w_router[l].transpose(1, 0, 2).reshape(D_MODEL, N_EXPERTS), w_group[l],
         jnp.zeros((D_MODEL, LANES - N_EXPERTS - N_GROUPS), F32)], axis=1)
    b_r = jnp.concatenate([b_router[l].reshape(-1), b_group[l],
                           jnp.zeros((LANES - N_EXPERTS - N_GROUPS,), F32)]).reshape(1, LANES)
    row = lambda v: v.reshape(1, D_MODEL)

    uvk = _proj_nn(xs, w_uvk, tm=1024, tn=1024)
    qvt = _proj_nt(xs, w_qv_t, tm=1024, tr=1024, q_scale=SB_HEAD_DIM ** -0.5)
    mix_a = _gmlp(uvk, gm_w_s[l], gm_ln_g[l], gm_ln_b[l], gm_b_s[l].T, tg=512)
    mix_b = _stick_breaking(qvt, uvk, tq=256, bk=128)
    h1 = _mixout(mix_a, mix_b, w_mix, xs, row(ln1_g[l]), row(ln1_b[l]), tm=512)
    kv = _proj_nn(mem[0], w_kv, tm=MEM_LEN, tn=1024)
    h2 = _memattn(h1, w_q, kv, w_o, row(ln2_g[l]), row(ln2_b[l]), tm=256)
    rout = _router(h2, w_r, b_r, tm=512)
    expert_id = rout[:, 0:TOP_K].astype(jnp.int32)
    gate = rout[:, TOP_K:2 * TOP_K]
    block_expert, nvalid, row_tok, row_gate, pos = _routing_tables(expert_id, gate)
    y_rows = _experts(block_expert, nvalid, row_tok, h2, w1[l], w3[l], w2[l], row_gate)
    out = _combine(pos, y_rows, h2, row(ln3_g[l]), row(ln3_b[l]), tc=256)
    return out.reshape(1, SEQ, D_MODEL)
```

```python
import functools

import jax
import jax.numpy as jnp
from jax import lax
from jax.experimental import pallas as pl
from jax.experimental.pallas import tpu as pltpu

D_MODEL = 2048
SEQ = 8192
CHUNK = 64
MEM_LEN = 256
GM_BLOCK = 128
GM_GROUPS = 8
GM_WIDTH = 1024
SB_HEADS = 8
SB_HEAD_DIM = 128
SB_WIDTH = 1024
MEM_HEADS = 4
MEM_HEAD_DIM = 512
N_GROUPS = 8
EXPERTS_PER_GROUP = 8
N_EXPERTS = 64
TOP_K = 2
D_EXPERT = 512
DN_ALPHA = 2.0 ** 0.25
LN_EPS = 1e-5

LANES = 128
MOE_ROWS = 128
MOE_ROWS_LOG2 = 7
N_ASSIGN = SEQ * TOP_K
N_ROW_BLOCKS = N_ASSIGN // MOE_ROWS + N_EXPERTS
N_ROWS = N_ROW_BLOCKS * MOE_ROWS
DISPATCH_CHUNK = 256
SB_UNDERFLOW = 110.0
VMEM_LIMIT = 56 * 1024 * 1024

BF16 = jnp.bfloat16
F32 = jnp.float32
I32 = jnp.int32


def _cparams(sem):
    return pltpu.CompilerParams(dimension_semantics=sem, vmem_limit_bytes=VMEM_LIMIT)


def _resident(shape, index_map):
    return pl.BlockSpec(shape, index_map, pipeline_mode=pl.Buffered(1))


def _layer_norm(y, g, b):
    mu = jnp.mean(y, axis=-1, keepdims=True)
    yc = y - mu
    var = jnp.mean(yc * yc, axis=-1, keepdims=True)
    return yc * lax.rsqrt(var + LN_EPS) * g + b


def _split_bf16(v):
    hi = v.astype(BF16)
    lo = (v - hi.astype(F32)).astype(BF16)
    return hi, lo


def _proj_nn_kernel(x_ref, w_ref, o_ref, xb_ref):
    @pl.when(pl.program_id(1) == 0)
    def _():
        xb_ref[...] = x_ref[...].astype(BF16)

    o_ref[...] = jnp.dot(xb_ref[...], w_ref[...], preferred_element_type=F32).astype(o_ref.dtype)


def _proj_nn(x, w, tm, tn):
    m, k = x.shape
    n = w.shape[1]
    return pl.pallas_call(
        _proj_nn_kernel,
        grid=(m // tm, n // tn),
        in_specs=[pl.BlockSpec((tm, k), lambda i, j: (i, 0)),
                  pl.BlockSpec((k, tn), lambda i, j: (0, j))],
        out_specs=pl.BlockSpec((tm, tn), lambda i, j: (i, j)),
        out_shape=jax.ShapeDtypeStruct((m, n), BF16),
        scratch_shapes=[pltpu.VMEM((tm, k), BF16)],
        compiler_params=_cparams(("arbitrary", "arbitrary")),
        name="proj_nn",
    )(x, w)


def _proj_nt_kernel(x_ref, wt_ref, o_ref, xb_ref, *, q_scale):
    j = pl.program_id(1)

    @pl.when(j == 0)
    def _():
        xb_ref[...] = x_ref[...].astype(BF16)

    acc = lax.dot_general(wt_ref[...], xb_ref[...], (((1,), (1,)), ((), ())),
                          preferred_element_type=F32)
    scale = jnp.where(j == 0, q_scale, 1.0).astype(F32)
    o_ref[...] = (acc * scale).astype(o_ref.dtype)


def _proj_nt(x, wt, tm, tr, q_scale):
    m, k = x.shape
    r = wt.shape[0]
    return pl.pallas_call(
        functools.partial(_proj_nt_kernel, q_scale=q_scale),
        grid=(m // tm, r // tr),
        in_specs=[pl.BlockSpec((tm, k), lambda i, j: (i, 0)),
                  pl.BlockSpec((tr, k), lambda i, j: (j, 0))],
        out_specs=pl.BlockSpec((tr, tm), lambda i, j: (j, i)),
        out_shape=jax.ShapeDtypeStruct((r, m), BF16),
        scratch_shapes=[pltpu.VMEM((tm, k), BF16)],
        compiler_params=_cparams(("arbitrary", "arbitrary")),
        name="proj_nt",
    )(x, wt)


def _gmlp_kernel(u_ref, v_ref, ws_ref, g_ref, b_ref, bs_ref, o_ref, *, tg):
    t_chunk = lax.broadcasted_iota(I32, (GM_BLOCK, GM_BLOCK), 0) // CHUNK
    s_chunk = lax.broadcasted_iota(I32, (GM_BLOCK, GM_BLOCK), 1) // CHUNK
    causal = t_chunk >= s_chunk
    for g in range(GM_GROUPS):
        wm = jnp.where(causal, ws_ref[g], 0.0).astype(BF16)
        ln_g = g_ref[g:g + 1, :]
        ln_b = b_ref[g:g + 1, :]
        bias = bs_ref[:, g:g + 1]
        cols = slice(g * LANES, (g + 1) * LANES)
        for n in range(tg // GM_BLOCK):
            rows = slice(n * GM_BLOCK, (n + 1) * GM_BLOCK)
            v = jax.nn.gelu(v_ref[rows, cols].astype(F32))
            vg = _layer_norm(v, ln_g, ln_b).astype(BF16)
            mixed = jnp.dot(wm, vg, preferred_element_type=F32) + bias
            u = jax.nn.gelu(u_ref[rows, cols].astype(F32))
            o_ref[rows, cols] = (u * mixed).astype(o_ref.dtype)


def _gmlp(uvk, gm_w_s, gm_ln_g, gm_ln_b, gm_b_s_t, tg):
    s = uvk.shape[0]
    return pl.pallas_call(
        functools.partial(_gmlp_kernel, tg=tg),
        grid=(s // tg,),
        in_specs=[pl.BlockSpec((tg, GM_WIDTH), lambda i: (i, 0)),
                  pl.BlockSpec((tg, GM_WIDTH), lambda i: (i, 1)),
                  pl.BlockSpec((GM_GROUPS, GM_BLOCK, GM_BLOCK), lambda i: (0, 0, 0)),
                  pl.BlockSpec((GM_GROUPS, LANES), lambda i: (0, 0)),
                  pl.BlockSpec((GM_GROUPS, LANES), lambda i: (0, 0)),
                  pl.BlockSpec((GM_BLOCK, GM_GROUPS), lambda i: (0, 0))],
        out_specs=pl.BlockSpec((tg, GM_WIDTH), lambda i: (i, 0)),
        out_shape=jax.ShapeDtypeStruct((s, GM_WIDTH), BF16),
        compiler_params=_cparams(("arbitrary",)),
        name="gmlp",
    )(uvk, uvk, gm_w_s, gm_ln_g, gm_ln_b, gm_b_s_t)


def _softplus(z):
    return jnp.maximum(z, 0.0) + jnp.log(1.0 + jnp.exp(-jnp.abs(z)))


def _sb_kernel(qt_ref, k_ref, vt_ref, o_ref, acc_ref, *, tq, bk):
    q0 = pl.program_id(0) * tq
    upper = (lax.broadcasted_iota(I32, (bk, bk), 1)
             >= lax.broadcasted_iota(I32, (bk, bk), 0)).astype(BF16)
    upper2 = jnp.concatenate([upper, upper], axis=1)

    def tile(h, ks, carry, mask):
        hrows = slice(h * SB_HEAD_DIM, (h + 1) * SB_HEAD_DIM)
        kb = k_ref[pl.ds(ks, bk), hrows]
        z = jnp.dot(kb, qt_ref[hrows, :], preferred_element_type=F32)
        sp = _softplus(z)
        if mask is not None:
            sp = jnp.where(mask, sp, 0.0)
        hi, lo = _split_bf16(sp)
        csum = jnp.dot(upper2, jnp.concatenate([hi, lo], axis=0),
                       preferred_element_type=F32) + carry
        a = jnp.exp(z - csum)
        if mask is not None:
            a = jnp.where(mask, a, 0.0)
        vt = vt_ref[hrows, pl.ds(ks, bk)]
        acc_ref[h] += jnp.dot(vt, a.astype(BF16), preferred_element_type=F32)
        return csum[0:1, :]

    acc_ref[...] = jnp.zeros_like(acc_ref)
    carries = [jnp.zeros((1, tq), F32) for _ in range(SB_HEADS)]
    key_in_tile = lax.broadcasted_iota(I32, (bk, tq), 0)
    q_in_tile = lax.broadcasted_iota(I32, (bk, tq), 1)
    for d in reversed(range(tq // bk)):
        mask = (key_in_tile + d * bk) < q_in_tile
        ks = pl.multiple_of(q0 + d * bk, bk)
        carries = [tile(h, ks, carries[h], mask) for h in range(SB_HEADS)]

    def cond(state):
        j = state[0]
        lowest = functools.reduce(jnp.minimum, state[1:])
        return jnp.logical_and(j >= 0, jnp.min(lowest) < SB_UNDERFLOW)

    def body(state):
        j = state[0]
        ks = pl.multiple_of(j * bk, bk)
        return (j - 1,) + tuple(tile(h, ks, state[1 + h], None) for h in range(SB_HEADS))

    lax.while_loop(cond, body, (q0 // bk - 1,) + tuple(carries))
    for h in range(SB_HEADS):
        o_ref[:, h * SB_HEAD_DIM:(h + 1) * SB_HEAD_DIM] = acc_ref[h].T.astype(o_ref.dtype)


def _stick_breaking(qvt, uvk, tq, bk):
    s = uvk.shape[0]
    return pl.pallas_call(
        functools.partial(_sb_kernel, tq=tq, bk=bk),
        grid=(s // tq,),
        in_specs=[pl.BlockSpec((SB_WIDTH, tq), lambda i: (0, i)),
                  _resident((s, SB_WIDTH), lambda i: (0, 2 * GM_WIDTH // SB_WIDTH)),
                  _resident((SB_WIDTH, s), lambda i: (1, 0))],
        out_specs=pl.BlockSpec((tq, SB_WIDTH), lambda i: (i, 0)),
        out_shape=jax.ShapeDtypeStruct((s, SB_WIDTH), BF16),
        scratch_shapes=[pltpu.VMEM((SB_HEADS, SB_HEAD_DIM, tq), F32)],
        compiler_params=_cparams(("arbitrary",)),
        name="stick_breaking",
    )(qvt, uvk, qvt)


def _mixout_kernel(a_ref, b_ref, wa_ref, wb_ref, x_ref, g_ref, beta_ref, o_ref):
    mixed = (jnp.dot(a_ref[...], wa_ref[...], preferred_element_type=F32)
             + jnp.dot(b_ref[...], wb_ref[...], preferred_element_type=F32))
    y = DN_ALPHA * x_ref[...] + mixed
    o_ref[...] = _layer_norm(y, g_ref[...], beta_ref[...])


def _mixout(mix_a, mix_b, w, x, g, b, tm):
    s = x.shape[0]
    return pl.pallas_call(
        _mixout_kernel,
        grid=(s // tm,),
        in_specs=[pl.BlockSpec((tm, GM_WIDTH), lambda i: (i, 0)),
                  pl.BlockSpec((tm, SB_WIDTH), lambda i: (i, 0)),
                  pl.BlockSpec((GM_WIDTH, D_MODEL), lambda i: (0, 0)),
                  pl.BlockSpec((SB_WIDTH, D_MODEL), lambda i: (1, 0)),
                  pl.BlockSpec((tm, D_MODEL), lambda i: (i, 0)),
                  pl.BlockSpec((1, D_MODEL), lambda i: (0, 0)),
                  pl.BlockSpec((1, D_MODEL), lambda i: (0, 0))],
        out_specs=pl.BlockSpec((tm, D_MODEL), lambda i: (i, 0)),
        out_shape=jax.ShapeDtypeStruct((s, D_MODEL), F32),
        compiler_params=_cparams(("arbitrary",)),
        name="mixout_ln1",
    )(mix_a, mix_b, w, w, x, g, b)


def _route(logits):
    lane = lax.broadcasted_iota(I32, logits.shape, 1)
    neg = jnp.float32(-jnp.inf)
    big = jnp.int32(1 << 20)

    def first_argmax(vals):
        m = jnp.max(vals, axis=-1, keepdims=True)
        idx = jnp.min(jnp.where(vals == m, lane, big), axis=-1, keepdims=True)
        return m, idx

    is_group = jnp.logical_and(lane >= N_EXPERTS, lane < N_EXPERTS + N_GROUPS)
    gl = jnp.where(is_group, logits, neg)
    gmax, glane = first_argmax(gl)
    g_val = 1.0 / jnp.sum(jnp.exp(gl - gmax), axis=-1, keepdims=True)
    gidx = glane - N_EXPERTS
    in_group = jnp.logical_and(lane >= gidx * EXPERTS_PER_GROUP,
                               lane < (gidx + 1) * EXPERTS_PER_GROUP)
    el = jnp.where(in_group, logits, neg)
    m1, i1 = first_argmax(el)
    m2, i2 = first_argmax(jnp.where(lane == i1, neg, el))
    e2 = jnp.exp(m2 - m1)
    gate1 = g_val / (1.0 + e2)
    gate2 = g_val * e2 / (1.0 + e2)
    return jnp.where(lane == 0, i1.astype(F32),
                     jnp.where(lane == 1, i2.astype(F32),
                               jnp.where(lane == 2, gate1,
                                         jnp.where(lane == 3, gate2, 0.0))))


def _memattn_kernel(h_ref, wq_ref, kv_ref, wo_ref, g_ref, beta_ref, wrh_ref, wrl_ref, br_ref,
                    o_ref, r_ref):
    h = h_ref[...]
    q = jnp.dot(h.astype(BF16), wq_ref[...], preferred_element_type=F32).astype(BF16)
    outs = []
    for hd in range(MEM_HEADS):
        cols = slice(hd * MEM_HEAD_DIM, (hd + 1) * MEM_HEAD_DIM)
        kh = kv_ref[:, cols]
        vh = kv_ref[:, D_MODEL + hd * MEM_HEAD_DIM:D_MODEL + (hd + 1) * MEM_HEAD_DIM]
        s = lax.dot_general(q[:, cols], kh, (((1,), (1,)), ((), ())),
                            preferred_element_type=F32) * (MEM_HEAD_DIM ** -0.5)
        s = s - jnp.max(s, axis=-1, keepdims=True)
        e = jnp.exp(s)
        p = e / jnp.sum(e, axis=-1, keepdims=True)
        outs.append(jnp.dot(p.astype(BF16), vh, preferred_element_type=F32).astype(BF16))
    o = jnp.concatenate(outs, axis=-1)
    attn = jnp.dot(o, wo_ref[...], preferred_element_type=F32)
    h2 = _layer_norm(DN_ALPHA * h + attn, g_ref[...], beta_ref[...])
    o_ref[...] = h2
    h_hi, h_lo = _split_bf16(h2)
    logits = (jnp.dot(h_hi, wrh_ref[...], preferred_element_type=F32)
              + jnp.dot(h_lo, wrh_ref[...], preferred_element_type=F32)
              + jnp.dot(h_hi, wrl_ref[...], preferred_element_type=F32)) + br_ref[...]
    r_ref[...] = _route(logits)


def _memattn(h1, wq, kv, wo, g, b, wr_hi, wr_lo, b_r, tm):
    s = h1.shape[0]
    return pl.pallas_call(
        _memattn_kernel,
        grid=(s // tm,),
        in_specs=[pl.BlockSpec((tm, D_MODEL), lambda i: (i, 0)),
                  _resident((D_MODEL, D_MODEL), lambda i: (0, 0)),
                  _resident((MEM_LEN, 2 * D_MODEL), lambda i: (0, 0)),
                  _resident((D_MODEL, D_MODEL), lambda i: (0, 0)),
                  pl.BlockSpec((1, D_MODEL), lambda i: (0, 0)),
                  pl.BlockSpec((1, D_MODEL), lambda i: (0, 0)),
                  _resident((D_MODEL, LANES), lambda i: (0, 0)),
                  _resident((D_MODEL, LANES), lambda i: (0, 0)),
                  pl.BlockSpec((1, LANES), lambda i: (0, 0))],
        out_specs=[pl.BlockSpec((tm, D_MODEL), lambda i: (i, 0)),
                   pl.BlockSpec((tm, LANES), lambda i: (i, 0))],
        out_shape=[jax.ShapeDtypeStruct((s, D_MODEL), F32),
                   jax.ShapeDtypeStruct((s, LANES), F32)],
        compiler_params=_cparams(("arbitrary",)),
        name="memattn_ln2_router",
    )(h1, wq, kv, wo, g, b, wr_hi, wr_lo, b_r)


def _dispatch_kernel(e_ref, tok_ref, pos_ref, meta_ref):
    ca = DISPATCH_CHUNK
    n_chunks = N_ASSIGN // ca
    e_iota = lax.broadcasted_iota(I32, (N_EXPERTS, ca), 0)

    def onehot(c):
        e_c = e_ref[:, pl.ds(pl.multiple_of(c * ca, ca), ca)]
        return (e_iota == e_c).astype(F32)

    counts = lax.fori_loop(
        0, n_chunks, lambda c, acc: acc + jnp.sum(onehot(c), axis=1, keepdims=True),
        jnp.zeros((N_EXPERTS, 1), F32))
    nblk = jnp.right_shift(counts.astype(I32) + (MOE_ROWS - 1), MOE_ROWS_LOG2)
    strict_lower = (lax.broadcasted_iota(I32, (N_EXPERTS, N_EXPERTS), 1)
                    < lax.broadcasted_iota(I32, (N_EXPERTS, N_EXPERTS), 0)).astype(BF16)
    nblk_lanes = jnp.broadcast_to(nblk.astype(F32), (N_EXPERTS, LANES)).astype(BF16)
    blk_start = jnp.dot(strict_lower, nblk_lanes, preferred_element_type=F32)
    pad_start = blk_start[:, 0:1] * MOE_ROWS
    lane = lax.broadcasted_iota(I32, (N_EXPERTS, LANES), 1)
    meta_ref[...] = jnp.where(lane == 0, blk_start.astype(I32),
                              jnp.where(lane == 1, jnp.broadcast_to(nblk, (N_EXPERTS, LANES)), 0))

    earlier = (lax.broadcasted_iota(I32, (ca, ca), 0)
               < lax.broadcasted_iota(I32, (ca, ca), 1)).astype(BF16)
    blk_iota = lax.broadcasted_iota(I32, (N_ROW_BLOCKS, ca), 0)
    row_iota = lax.broadcasted_iota(I32, (MOE_ROWS, ca), 0)
    a_iota = lax.broadcasted_iota(I32, (1, ca), 1)

    def body(c, state):
        seen, acc_hi, acc_lo = state
        oh = onehot(c)
        before = jnp.dot(oh.astype(BF16), earlier, preferred_element_type=F32) + seen
        dest = jnp.sum(oh * (before + pad_start), axis=0, keepdims=True).astype(I32)
        pos_ref[:, pl.ds(pl.multiple_of(c * ca, ca), ca)] = dest
        in_blk = (blk_iota == jnp.right_shift(dest, MOE_ROWS_LOG2)).astype(F32)
        in_row = (row_iota == jnp.bitwise_and(dest, MOE_ROWS - 1)).astype(BF16)
        tok = jnp.right_shift(c * ca + a_iota, 1)
        tok_hi = jnp.right_shift(tok, 7).astype(F32)
        tok_lo = jnp.bitwise_and(tok, 127).astype(F32)
        nt = (((1,), (1,)), ((), ()))
        acc_hi = acc_hi + lax.dot_general((in_blk * tok_hi).astype(BF16), in_row, nt,
                                          preferred_element_type=F32)
        acc_lo = acc_lo + lax.dot_general((in_blk * tok_lo).astype(BF16), in_row, nt,
                                          preferred_element_type=F32)
        return seen + jnp.sum(oh, axis=1, keepdims=True), acc_hi, acc_lo

    zeros = jnp.zeros((N_ROW_BLOCKS, MOE_ROWS), F32)
    _, acc_hi, acc_lo = lax.fori_loop(0, n_chunks, body,
                                      (jnp.zeros((N_EXPERTS, 1), F32), zeros, zeros))
    tok_ref[...] = (acc_hi * 128.0 + acc_lo).astype(I32)


def _dispatch(flat_e):
    return pl.pallas_call(
        _dispatch_kernel,
        out_shape=[jax.ShapeDtypeStruct((N_ROW_BLOCKS, MOE_ROWS), I32),
                   jax.ShapeDtypeStruct((1, N_ASSIGN), I32),
                   jax.ShapeDtypeStruct((N_EXPERTS, LANES), I32)],
        compiler_params=pltpu.CompilerParams(vmem_limit_bytes=VMEM_LIMIT),
        name="moe_dispatch",
    )(flat_e)


def _expert_kernel(bs_ref, nb_ref, tot_ref, tok_ref, h_hbm, w1_ref, w3_ref, w2_ref, y_hbm,
                   xbuf, ybuf, gsem, ysem, w1b, w3b, w2b):
    e = pl.program_id(0)
    total = tot_ref[0]
    first = bs_ref[e]
    nblk = nb_ref[e]

    def start_gather(blk, slot):
        def issue(r, c):
            tok = tok_ref[blk * MOE_ROWS + r]
            pltpu.make_async_copy(h_hbm.at[pl.ds(tok, 1)], xbuf.at[slot, pl.ds(r, 1)],
                                  gsem.at[slot]).start()
            return c
        lax.fori_loop(0, MOE_ROWS, issue, 0, unroll=8)

    def y_copy(blk, slot):
        return pltpu.make_async_copy(ybuf.at[slot], y_hbm.at[pl.ds(blk * MOE_ROWS, MOE_ROWS)],
                                     ysem.at[slot])

    @pl.when(jnp.logical_and(e == 0, total > 0))
    def _():
        start_gather(0, 0)

    @pl.when(nblk > 0)
    def _():
        w1b[...] = w1_ref[0].astype(BF16)
        w3b[...] = w3_ref[0].astype(BF16)
        w2b[...] = w2_ref[0].astype(BF16)

    def block(b, c):
        blk = first + b
        slot = blk % 2

        @pl.when(blk + 1 < total)
        def _():
            start_gather(blk + 1, 1 - slot)

        pltpu.make_async_copy(h_hbm.at[pl.ds(0, MOE_ROWS)], xbuf.at[slot], gsem.at[slot]).wait()

        @pl.when(blk >= 2)
        def _():
            y_copy(blk - 2, slot).wait()

        xb = xbuf[slot].astype(BF16)
        a = jnp.dot(xb, w1b[...], preferred_element_type=F32)
        g = jnp.dot(xb, w3b[...], preferred_element_type=F32)
        hid = (a * jax.nn.sigmoid(a) * g).astype(BF16)
        ybuf[slot] = jnp.dot(hid, w2b[...], preferred_element_type=F32)
        y_copy(blk, slot).start()
        return c

    lax.fori_loop(0, nblk, block, 0)

    @pl.when(e == pl.num_programs(0) - 1)
    def _():
        for back in (1, 2):
            @pl.when(total >= back)
            def _():
                y_copy(total - back, (total - back) % 2).wait()

        ybuf[0] = jnp.zeros((MOE_ROWS, D_MODEL), F32)

        def zero_block(blk, c):
            y_copy(blk, 0).start()
            y_copy(blk, 0).wait()
            return c

        lax.fori_loop(total, N_ROW_BLOCKS, zero_block, 0)


def _experts(blk_start, nblk, total, row_tok, h2, w1, w3, w2):
    wmap = lambda e, *_: (e, 0, 0)
    grid_spec = pltpu.PrefetchScalarGridSpec(
        num_scalar_prefetch=4,
        grid=(N_EXPERTS,),
        in_specs=[pl.BlockSpec(memory_space=pl.ANY),
                  pl.BlockSpec((1, D_MODEL, D_EXPERT), wmap),
                  pl.BlockSpec((1, D_MODEL, D_EXPERT), wmap),
                  pl.BlockSpec((1, D_EXPERT, D_MODEL), wmap)],
        out_specs=pl.BlockSpec(memory_space=pl.ANY),
        scratch_shapes=[pltpu.VMEM((2, MOE_ROWS, D_MODEL), F32),
                        pltpu.VMEM((2, MOE_ROWS, D_MODEL), F32),
                        pltpu.SemaphoreType.DMA((2,)),
                        pltpu.SemaphoreType.DMA((2,)),
                        pltpu.VMEM((D_MODEL, D_EXPERT), BF16),
                        pltpu.VMEM((D_MODEL, D_EXPERT), BF16),
                        pltpu.VMEM((D_EXPERT, D_MODEL), BF16)],
    )
    return pl.pallas_call(
        _expert_kernel,
        grid_spec=grid_spec,
        out_shape=jax.ShapeDtypeStruct((N_ROWS, D_MODEL), F32),
        compiler_params=_cparams(("arbitrary",)),
        name="experts",
    )(blk_start, nblk, total, row_tok, h2, w1, w3, w2)


def _combine_kernel(pos_ref, y_hbm, h_ref, r_ref, g_ref, beta_ref, o_ref, ybuf, sem, *, tc):
    i = pl.program_id(0)
    n = pl.num_programs(0)
    slot = i % 2

    def start_gather(blk, slot_):
        def issue(r, c):
            for kk in range(TOP_K):
                p = pos_ref[(blk * tc + r) * TOP_K + kk]
                pltpu.make_async_copy(y_hbm.at[pl.ds(p, 1)], ybuf.at[slot_, kk, pl.ds(r, 1)],
                                      sem.at[slot_]).start()
            return c
        lax.fori_loop(0, tc, issue, 0, unroll=8)

    @pl.when(i == 0)
    def _():
        start_gather(0, 0)

    @pl.when(i + 1 < n)
    def _():
        start_gather(i + 1, 1 - slot)

    for kk in range(TOP_K):
        pltpu.make_async_copy(y_hbm.at[pl.ds(0, tc)], ybuf.at[slot, kk], sem.at[slot]).wait()
    gates = r_ref[...]
    moe = (ybuf[slot, 0] * gates[:, TOP_K:TOP_K + 1]
           + ybuf[slot, 1] * gates[:, TOP_K + 1:TOP_K + 2])
    o_ref[...] = _layer_norm(DN_ALPHA * h_ref[...] + moe, g_ref[...], beta_ref[...])


def _combine(pos, y_rows, h2, rout, g, b, tc):
    s = h2.shape[0]
    grid_spec = pltpu.PrefetchScalarGridSpec(
        num_scalar_prefetch=1,
        grid=(s // tc,),
        in_specs=[pl.BlockSpec(memory_space=pl.ANY),
                  pl.BlockSpec((tc, D_MODEL), lambda i, pos: (i, 0)),
                  pl.BlockSpec((tc, LANES), lambda i, pos: (i, 0)),
                  pl.BlockSpec((1, D_MODEL), lambda i, pos: (0, 0)),
                  pl.BlockSpec((1, D_MODEL), lambda i, pos: (0, 0))],
        out_specs=pl.BlockSpec((tc, D_MODEL), lambda i, pos: (i, 0)),
        scratch_shapes=[pltpu.VMEM((2, TOP_K, tc, D_MODEL), F32),
                        pltpu.SemaphoreType.DMA((2,))],
    )
    return pl.pallas_call(
        functools.partial(_combine_kernel, tc=tc),
        grid_spec=grid_spec,
        out_shape=jax.ShapeDtypeStruct((s, D_MODEL), F32),
        compiler_params=_cparams(("arbitrary",)),
        name="combine_ln3",
    )(pos, y_rows, h2, rout, g, b)


def kernel(x, mem, w_in, gm_ln_g, gm_ln_b, gm_w_s, gm_b_s, w_mix_out, ln1_g, ln1_b,
           mem_w_q, mem_w_k, mem_w_v, mem_w_o, ln2_g, ln2_b,
           w_group, b_group, w_router, b_router, w1, w3, w2, ln3_g, ln3_b):
    assert x.shape == (1, SEQ, D_MODEL) and w_in.shape[0] == 1
    xs = x[0]
    l = 0
    w_in_l = w_in[l]
    q_lo, k_lo, v_lo = 2 * GM_WIDTH, 2 * GM_WIDTH + SB_WIDTH, 2 * GM_WIDTH + 2 * SB_WIDTH
    w_uvk = jnp.concatenate([w_in_l[:, :q_lo], w_in_l[:, k_lo:v_lo]], axis=1).astype(BF16)
    w_qv_t = jnp.concatenate([w_in_l[:, q_lo:k_lo], w_in_l[:, v_lo:]], axis=1).T.astype(BF16)
    w_mix = w_mix_out[l].astype(BF16)
    w_q = mem_w_q[l].astype(BF16)
    w_kv = jnp.concatenate([mem_w_k[l], mem_w_v[l]], axis=1).astype(BF16)
    w_o = mem_w_o[l].astype(BF16)
    w_r = jnp.concatenate(
        [w_router[l].transpose(1, 0, 2).reshape(D_MODEL, N_EXPERTS), w_group[l],
         jnp.zeros((D_MODEL, LANES - N_EXPERTS - N_GROUPS), F32)], axis=1)
    wr_hi, wr_lo = _split_bf16(w_r)
    b_r = jnp.concatenate([b_router[l].reshape(-1), b_group[l],
                           jnp.zeros((LANES - N_EXPERTS - N_GROUPS,), F32)]).reshape(1, LANES)
    row = lambda v: v.reshape(1, D_MODEL)

    uvk = _proj_nn(xs, w_uvk, tm=1024, tn=1024)
    qvt = _proj_nt(xs, w_qv_t, tm=1024, tr=1024, q_scale=SB_HEAD_DIM ** -0.5)
    mix_a = _gmlp(uvk, gm_w_s[l], gm_ln_g[l], gm_ln_b[l], gm_b_s[l].T, tg=512)
    mix_b = _stick_breaking(qvt, uvk, tq=256, bk=128)
    h1 = _mixout(mix_a, mix_b, w_mix, xs, row(ln1_g[l]), row(ln1_b[l]), tm=512)
    kv = _proj_nn(mem[0], w_kv, tm=MEM_LEN, tn=1024)
    h2, rout = _memattn(h1, w_q, kv, w_o, row(ln2_g[l]), row(ln2_b[l]), wr_hi, wr_lo, b_r, tm=256)
    flat_e = rout[:, 0:TOP_K].astype(I32).reshape(1, N_ASSIGN)
    row_tok, pos, meta = _dispatch(flat_e)
    blk_start, nblk = meta[:, 0], meta[:, 1]
    total = (blk_start[N_EXPERTS - 1] + nblk[N_EXPERTS - 1]).reshape(1)
    y_rows = _experts(blk_start, nblk, total, row_tok.reshape(N_ROWS), h2, w1[l], w3[l], w2[l])
    out = _combine(pos.reshape(N_ASSIGN), y_rows, h2, rout, row(ln3_g[l]), row(ln3_b[l]), tc=256)
    return out.reshape(1, SEQ, D_MODEL)
```

```python
import functools
import math

import jax
import jax.numpy as jnp
from jax import lax
from jax.experimental import pallas as pl
from jax.experimental.pallas import tpu as pltpu

D_MODEL = 2048
SEQ = 8192
CHUNK = 64
MEM_LEN = 256
GM_BLOCK = 128
GM_GROUPS = 8
GM_WIDTH = 1024
SB_HEADS = 8
SB_HEAD_DIM = 128
SB_WIDTH = 1024
MEM_HEADS = 4
MEM_HEAD_DIM = 512
N_GROUPS = 8
EXPERTS_PER_GROUP = 8
N_EXPERTS = 64
TOP_K = 2
D_EXPERT = 512
DN_ALPHA = 2.0 ** 0.25
LN_EPS = 1e-5

LANES = 128
MOE_ROWS = 128
MOE_ROWS_LOG2 = 7
N_ASSIGN = SEQ * TOP_K
N_ROW_BLOCKS = N_ASSIGN // MOE_ROWS + N_EXPERTS
N_ROWS = N_ROW_BLOCKS * MOE_ROWS
DISPATCH_CHUNK = 256
SB_UNDERFLOW = 160.0
VMEM_LIMIT = 56 * 1024 * 1024

BF16 = jnp.bfloat16
F32 = jnp.float32
I32 = jnp.int32


def _cparams(sem):
    return pltpu.CompilerParams(dimension_semantics=sem, vmem_limit_bytes=VMEM_LIMIT)


def _resident(shape, index_map):
    return pl.BlockSpec(shape, index_map, pipeline_mode=pl.Buffered(1))


def _layer_norm(y, g, b):
    mu = jnp.mean(y, axis=-1, keepdims=True)
    yc = y - mu
    var = jnp.mean(yc * yc, axis=-1, keepdims=True)
    return yc * lax.rsqrt(var + LN_EPS) * g + b


def _split_bf16(v):
    hi = v.astype(BF16)
    lo = (v - hi.astype(F32)).astype(BF16)
    return hi, lo


def _proj_nn_kernel(x_ref, w_ref, o_ref, xb_ref):
    @pl.when(pl.program_id(1) == 0)
    def _():
        xb_ref[...] = x_ref[...].astype(BF16)

    o_ref[...] = jnp.dot(xb_ref[...], w_ref[...], preferred_element_type=F32).astype(o_ref.dtype)


def _proj_nn(x, w, tm, tn):
    m, k = x.shape
    n = w.shape[1]
    return pl.pallas_call(
        _proj_nn_kernel,
        grid=(m // tm, n // tn),
        in_specs=[pl.BlockSpec((tm, k), lambda i, j: (i, 0)),
                  pl.BlockSpec((k, tn), lambda i, j: (0, j))],
        out_specs=pl.BlockSpec((tm, tn), lambda i, j: (i, j)),
        out_shape=jax.ShapeDtypeStruct((m, n), BF16),
        scratch_shapes=[pltpu.VMEM((tm, k), BF16)],
        compiler_params=_cparams(("arbitrary", "arbitrary")),
        name="proj_nn",
    )(x, w)


def _proj_nt_kernel(x_ref, wt_ref, o_ref, xb_ref, *, q_scale):
    j = pl.program_id(1)

    @pl.when(j == 0)
    def _():
        xb_ref[...] = x_ref[...].astype(BF16)

    acc = lax.dot_general(wt_ref[...], xb_ref[...], (((1,), (1,)), ((), ())),
                          preferred_element_type=F32)
    scale = jnp.where(j == 0, q_scale, 1.0).astype(F32)
    o_ref[...] = (acc * scale).astype(o_ref.dtype)


def _proj_nt(x, wt, tm, tr, q_scale):
    m, k = x.shape
    r = wt.shape[0]
    return pl.pallas_call(
        functools.partial(_proj_nt_kernel, q_scale=q_scale),
        grid=(m // tm, r // tr),
        in_specs=[pl.BlockSpec((tm, k), lambda i, j: (i, 0)),
                  pl.BlockSpec((tr, k), lambda i, j: (j, 0))],
        out_specs=pl.BlockSpec((tr, tm), lambda i, j: (j, i)),
        out_shape=jax.ShapeDtypeStruct((r, m), BF16),
        scratch_shapes=[pltpu.VMEM((tm, k), BF16)],
        compiler_params=_cparams(("arbitrary", "arbitrary")),
        name="proj_nt",
    )(x, wt)


def _gmlp_kernel(u_ref, v_ref, ws_ref, g_ref, b_ref, bs_ref, o_ref, *, tg):
    t_chunk = lax.broadcasted_iota(I32, (GM_BLOCK, GM_BLOCK), 0) // CHUNK
    s_chunk = lax.broadcasted_iota(I32, (GM_BLOCK, GM_BLOCK), 1) // CHUNK
    causal = t_chunk >= s_chunk
    for g in range(GM_GROUPS):
        wm = jnp.where(causal, ws_ref[g], 0.0).astype(BF16)
        ln_g = g_ref[g:g + 1, :]
        ln_b = b_ref[g:g + 1, :]
        bias = bs_ref[:, g:g + 1]
        cols = slice(g * LANES, (g + 1) * LANES)
        for n in range(tg // GM_BLOCK):
            rows = slice(n * GM_BLOCK, (n + 1) * GM_BLOCK)
            v = jax.nn.gelu(v_ref[rows, cols].astype(F32))
            vg = _layer_norm(v, ln_g, ln_b).astype(BF16)
            mixed = jnp.dot(wm, vg, preferred_element_type=F32) + bias
            u = jax.nn.gelu(u_ref[rows, cols].astype(F32))
            o_ref[rows, cols] = (u * mixed).astype(o_ref.dtype)


def _gmlp(uvk, gm_w_s, gm_ln_g, gm_ln_b, gm_b_s_t, tg):
    s = uvk.shape[0]
    return pl.pallas_call(
        functools.partial(_gmlp_kernel, tg=tg),
        grid=(s // tg,),
        in_specs=[pl.BlockSpec((tg, GM_WIDTH), lambda i: (i, 0)),
                  pl.BlockSpec((tg, GM_WIDTH), lambda i: (i, 1)),
                  pl.BlockSpec((GM_GROUPS, GM_BLOCK, GM_BLOCK), lambda i: (0, 0, 0)),
                  pl.BlockSpec((GM_GROUPS, LANES), lambda i: (0, 0)),
                  pl.BlockSpec((GM_GROUPS, LANES), lambda i: (0, 0)),
                  pl.BlockSpec((GM_BLOCK, GM_GROUPS), lambda i: (0, 0))],
        out_specs=pl.BlockSpec((tg, GM_WIDTH), lambda i: (i, 0)),
        out_shape=jax.ShapeDtypeStruct((s, GM_WIDTH), BF16),
        compiler_params=_cparams(("arbitrary",)),
        name="gmlp",
    )(uvk, uvk, gm_w_s, gm_ln_g, gm_ln_b, gm_b_s_t)


def _softplus2(u):
    return jnp.maximum(u, 0.0) + jnp.log2(1.0 + jnp.exp2(-jnp.abs(u)))


def _sb_kernel(qt_ref, k_ref, vt_ref, o_ref, *acc_refs, tq, bk):
    q0 = pl.program_id(0) * tq
    upper = (lax.broadcasted_iota(I32, (bk, bk), 1)
             >= lax.broadcasted_iota(I32, (bk, bk), 0)).astype(BF16)
    upper2 = jnp.concatenate([upper, upper], axis=1)

    heads = range(SB_HEADS)
    hrows = [slice(h * SB_HEAD_DIM, (h + 1) * SB_HEAD_DIM) for h in heads]

    def tiles(ks, carries, mask):
        zs = [jnp.dot(k_ref[pl.ds(ks, bk), hrows[h]], qt_ref[hrows[h], :],
                      preferred_element_type=F32) for h in heads]
        sps = [_softplus2(z) for z in zs]
        if mask is not None:
            sps = [jnp.where(mask, sp, 0.0) for sp in sps]
        hls = [jnp.concatenate(_split_bf16(sp), axis=0) for sp in sps]
        csums = [jnp.dot(upper2, hls[h], preferred_element_type=F32) + carries[h] for h in heads]
        probs = [jnp.exp2(zs[h] - csums[h]) for h in heads]
        if mask is not None:
            probs = [jnp.where(mask, a, 0.0) for a in probs]
        for h in heads:
            acc_refs[h][...] += jnp.dot(vt_ref[hrows[h], pl.ds(ks, bk)], probs[h].astype(BF16),
                                        preferred_element_type=F32)
        return [csum[0:1, :] for csum in csums]

    for acc_ref in acc_refs:
        acc_ref[...] = jnp.zeros_like(acc_ref)
    carries = [jnp.zeros((1, tq), F32) for _ in range(SB_HEADS)]
    key_in_tile = lax.broadcasted_iota(I32, (bk, tq), 0)
    q_in_tile = lax.broadcasted_iota(I32, (bk, tq), 1)
    for d in reversed(range(tq // bk)):
        mask = (key_in_tile + d * bk) < q_in_tile
        ks = pl.multiple_of(q0 + d * bk, bk)
        carries = tiles(ks, carries, mask)

    def cond(state):
        j = state[0]
        lowest = functools.reduce(jnp.minimum, state[1:])
        return jnp.logical_and(j >= 0, jnp.min(lowest) < SB_UNDERFLOW)

    def body(state):
        j = state[0]
        ks = pl.multiple_of(j * bk, bk)
        return (j - 1,) + tuple(tiles(ks, state[1:], None))

    lax.while_loop(cond, body, (q0 // bk - 1,) + tuple(carries))
    for h in range(SB_HEADS):
        o_ref[:, h * SB_HEAD_DIM:(h + 1) * SB_HEAD_DIM] = acc_refs[h][...].T.astype(o_ref.dtype)


def _stick_breaking(qvt, uvk, tq, bk):
    s = uvk.shape[0]
    return pl.pallas_call(
        functools.partial(_sb_kernel, tq=tq, bk=bk),
        grid=(s // tq,),
        in_specs=[pl.BlockSpec((SB_WIDTH, tq), lambda i: (0, i)),
                  _resident((s, SB_WIDTH), lambda i: (0, 2 * GM_WIDTH // SB_WIDTH)),
                  _resident((SB_WIDTH, s), lambda i: (1, 0))],
        out_specs=pl.BlockSpec((tq, SB_WIDTH), lambda i: (i, 0)),
        out_shape=jax.ShapeDtypeStruct((s, SB_WIDTH), BF16),
        scratch_shapes=[pltpu.VMEM((SB_HEAD_DIM, tq), F32) for _ in range(SB_HEADS)],
        compiler_params=_cparams(("arbitrary",)),
        name="stick_breaking",
    )(qvt, uvk, qvt)


def _mixout_kernel(a_ref, b_ref, wa_ref, wb_ref, x_ref, g_ref, beta_ref, o_ref):
    mixed = (jnp.dot(a_ref[...], wa_ref[...], preferred_element_type=F32)
             + jnp.dot(b_ref[...], wb_ref[...], preferred_element_type=F32))
    y = DN_ALPHA * x_ref[...] + mixed
    o_ref[...] = _layer_norm(y, g_ref[...], beta_ref[...])


def _mixout(mix_a, mix_b, w, x, g, b, tm):
    s = x.shape[0]
    return pl.pallas_call(
        _mixout_kernel,
        grid=(s // tm,),
        in_specs=[pl.BlockSpec((tm, GM_WIDTH), lambda i: (i, 0)),
                  pl.BlockSpec((tm, SB_WIDTH), lambda i: (i, 0)),
                  pl.BlockSpec((GM_WIDTH, D_MODEL), lambda i: (0, 0)),
                  pl.BlockSpec((SB_WIDTH, D_MODEL), lambda i: (1, 0)),
                  pl.BlockSpec((tm, D_MODEL), lambda i: (i, 0)),
                  pl.BlockSpec((1, D_MODEL), lambda i: (0, 0)),
                  pl.BlockSpec((1, D_MODEL), lambda i: (0, 0))],
        out_specs=pl.BlockSpec((tm, D_MODEL), lambda i: (i, 0)),
        out_shape=jax.ShapeDtypeStruct((s, D_MODEL), F32),
        compiler_params=_cparams(("arbitrary",)),
        name="mixout_ln1",
    )(mix_a, mix_b, w, w, x, g, b)


def _route(logits):
    lane = lax.broadcasted_iota(I32, logits.shape, 1)
    neg = jnp.float32(-jnp.inf)
    big = jnp.int32(1 << 20)

    def first_argmax(vals):
        m = jnp.max(vals, axis=-1, keepdims=True)
        idx = jnp.min(jnp.where(vals == m, lane, big), axis=-1, keepdims=True)
        return m, idx

    is_group = jnp.logical_and(lane >= N_EXPERTS, lane < N_EXPERTS + N_GROUPS)
    gl = jnp.where(is_group, logits, neg)
    gmax, glane = first_argmax(gl)
    g_val = 1.0 / jnp.sum(jnp.exp(gl - gmax), axis=-1, keepdims=True)
    gidx = glane - N_EXPERTS
    in_group = jnp.logical_and(lane >= gidx * EXPERTS_PER_GROUP,
                               lane < (gidx + 1) * EXPERTS_PER_GROUP)
    el = jnp.where(in_group, logits, neg)
    m1, i1 = first_argmax(el)
    m2, i2 = first_argmax(jnp.where(lane == i1, neg, el))
    e2 = jnp.exp(m2 - m1)
    gate1 = g_val / (1.0 + e2)
    gate2 = g_val * e2 / (1.0 + e2)
    return jnp.where(lane == 0, i1.astype(F32),
                     jnp.where(lane == 1, i2.astype(F32),
                               jnp.where(lane == 2, gate1,
                                         jnp.where(lane == 3, gate2, 0.0))))


def _memattn_kernel(h_ref, wq_ref, kv_ref, wo_ref, g_ref, beta_ref, wrh_ref, wrl_ref, br_ref,
                    o_ref, r_ref):
    h = h_ref[...]
    q = jnp.dot(h.astype(BF16), wq_ref[...], preferred_element_type=F32).astype(BF16)
    outs = []
    for hd in range(MEM_HEADS):
        cols = slice(hd * MEM_HEAD_DIM, (hd + 1) * MEM_HEAD_DIM)
        kh = kv_ref[:, cols]
        vh = kv_ref[:, D_MODEL + hd * MEM_HEAD_DIM:D_MODEL + (hd + 1) * MEM_HEAD_DIM]
        s = lax.dot_general(q[:, cols], kh, (((1,), (1,)), ((), ())),
                            preferred_element_type=F32) * (MEM_HEAD_DIM ** -0.5)
        s = s - jnp.max(s, axis=-1, keepdims=True)
        e = jnp.exp(s)
        p = e / jnp.sum(e, axis=-1, keepdims=True)
        outs.append(jnp.dot(p.astype(BF16), vh, preferred_element_type=F32).astype(BF16))
    o = jnp.concatenate(outs, axis=-1)
    attn = jnp.dot(o, wo_ref[...], preferred_element_type=F32)
    h2 = _layer_norm(DN_ALPHA * h + attn, g_ref[...], beta_ref[...])
    o_ref[...] = h2
    h_hi, h_lo = _split_bf16(h2)
    logits = (jnp.dot(h_hi, wrh_ref[...], preferred_element_type=F32)
              + jnp.dot(h_lo, wrh_ref[...], preferred_element_type=F32)
              + jnp.dot(h_hi, wrl_ref[...], preferred_element_type=F32)) + br_ref[...]
    r_ref[...] = _route(logits)


def _memattn(h1, wq, kv, wo, g, b, wr_hi, wr_lo, b_r, tm):
    s = h1.shape[0]
    return pl.pallas_call(
        _memattn_kernel,
        grid=(s // tm,),
        in_specs=[pl.BlockSpec((tm, D_MODEL), lambda i: (i, 0)),
                  _resident((D_MODEL, D_MODEL), lambda i: (0, 0)),
                  _resident((MEM_LEN, 2 * D_MODEL), lambda i: (0, 0)),
                  _resident((D_MODEL, D_MODEL), lambda i: (0, 0)),
                  pl.BlockSpec((1, D_MODEL), lambda i: (0, 0)),
                  pl.BlockSpec((1, D_MODEL), lambda i: (0, 0)),
                  _resident((D_MODEL, LANES), lambda i: (0, 0)),
                  _resident((D_MODEL, LANES), lambda i: (0, 0)),
                  pl.BlockSpec((1, LANES), lambda i: (0, 0))],
        out_specs=[pl.BlockSpec((tm, D_MODEL), lambda i: (i, 0)),
                   pl.BlockSpec((tm, LANES), lambda i: (i, 0))],
        out_shape=[jax.ShapeDtypeStruct((s, D_MODEL), F32),
                   jax.ShapeDtypeStruct((s, LANES), F32)],
        compiler_params=_cparams(("arbitrary",)),
        name="memattn_ln2_router",
    )(h1, wq, kv, wo, g, b, wr_hi, wr_lo, b_r)


def _dispatch_kernel(e_ref, tok_ref, pos_ref, meta_ref):
    ca = DISPATCH_CHUNK
    n_chunks = N_ASSIGN // ca
    e_iota = lax.broadcasted_iota(I32, (N_EXPERTS, ca), 0)

    def onehot(c):
        e_c = e_ref[:, pl.ds(pl.multiple_of(c * ca, ca), ca)]
        return (e_iota == e_c).astype(F32)

    counts = lax.fori_loop(
        0, n_chunks, lambda c, acc: acc + jnp.sum(onehot(c), axis=1, keepdims=True),
        jnp.zeros((N_EXPERTS, 1), F32))
    nblk = jnp.right_shift(counts.astype(I32) + (MOE_ROWS - 1), MOE_ROWS_LOG2)
    strict_lower = (lax.broadcasted_iota(I32, (N_EXPERTS, N_EXPERTS), 1)
                    < lax.broadcasted_iota(I32, (N_EXPERTS, N_EXPERTS), 0)).astype(BF16)
    nblk_lanes = jnp.broadcast_to(nblk.astype(F32), (N_EXPERTS, LANES)).astype(BF16)
    blk_start = jnp.dot(strict_lower, nblk_lanes, preferred_element_type=F32)
    pad_start = blk_start[:, 0:1] * MOE_ROWS
    lane = lax.broadcasted_iota(I32, (N_EXPERTS, LANES), 1)
    meta_ref[...] = jnp.where(lane == 0, blk_start.astype(I32),
                              jnp.where(lane == 1, jnp.broadcast_to(nblk, (N_EXPERTS, LANES)), 0))

    earlier = (lax.broadcasted_iota(I32, (ca, ca), 0)
               < lax.broadcasted_iota(I32, (ca, ca), 1)).astype(BF16)
    blk_iota = lax.broadcasted_iota(I32, (N_ROW_BLOCKS, ca), 0)
    row_iota = lax.broadcasted_iota(I32, (MOE_ROWS, ca), 0)
    a_iota = lax.broadcasted_iota(I32, (1, ca), 1)

    def body(c, state):
        seen, acc_hi, acc_lo = state
        oh = onehot(c)
        before = jnp.dot(oh.astype(BF16), earlier, preferred_element_type=F32) + seen
        dest = jnp.sum(oh * (before + pad_start), axis=0, keepdims=True).astype(I32)
        pos_ref[:, pl.ds(pl.multiple_of(c * ca, ca), ca)] = dest
        in_blk = (blk_iota == jnp.right_shift(dest, MOE_ROWS_LOG2)).astype(F32)
        in_row = (row_iota == jnp.bitwise_and(dest, MOE_ROWS - 1)).astype(BF16)
        tok = jnp.right_shift(c * ca + a_iota, 1)
        tok_hi = jnp.right_shift(tok, 7).astype(F32)
        tok_lo = jnp.bitwise_and(tok, 127).astype(F32)
        nt = (((1,), (1,)), ((), ()))
        acc_hi = acc_hi + lax.dot_general((in_blk * tok_hi).astype(BF16), in_row, nt,
                                          preferred_element_type=F32)
        acc_lo = acc_lo + lax.dot_general((in_blk * tok_lo).astype(BF16), in_row, nt,
                                          preferred_element_type=F32)
        return seen + jnp.sum(oh, axis=1, keepdims=True), acc_hi, acc_lo

    zeros = jnp.zeros((N_ROW_BLOCKS, MOE_ROWS), F32)
    _, acc_hi, acc_lo = lax.fori_loop(0, n_chunks, body,
                                      (jnp.zeros((N_EXPERTS, 1), F32), zeros, zeros))
    tok_ref[...] = (acc_hi * 128.0 + acc_lo).astype(I32)


def _dispatch(flat_e):
    return pl.pallas_call(
        _dispatch_kernel,
        out_shape=[jax.ShapeDtypeStruct((N_ROW_BLOCKS, MOE_ROWS), I32),
                   jax.ShapeDtypeStruct((1, N_ASSIGN), I32),
                   jax.ShapeDtypeStruct((N_EXPERTS, LANES), I32)],
        compiler_params=pltpu.CompilerParams(vmem_limit_bytes=VMEM_LIMIT),
        name="moe_dispatch",
    )(flat_e)


def _expert_kernel(bs_ref, nb_ref, tot_ref, tok_ref, h_hbm, w1_ref, w3_ref, w2_ref, y_hbm,
                   xbuf, ybuf, gsem, ysem, w1b, w3b, w2b):
    e = pl.program_id(0)
    total = tot_ref[0]
    first = bs_ref[e]
    nblk = nb_ref[e]

    def issue_rows(blk, slot, rows):
        for r in rows:
            tok = tok_ref[blk * MOE_ROWS + r]
            pltpu.make_async_copy(h_hbm.at[pl.ds(tok, 1)], xbuf.at[slot, pl.ds(r, 1)],
                                  gsem.at[slot]).start(priority=1)

    def wait_gather(slot):
        pltpu.make_async_copy(h_hbm.at[pl.ds(0, MOE_ROWS)], xbuf.at[slot], gsem.at[slot]).wait()

    def y_copy(blk, slot):
        return pltpu.make_async_copy(ybuf.at[slot], y_hbm.at[pl.ds(blk * MOE_ROWS, MOE_ROWS)],
                                     ysem.at[slot])

    @pl.when(e == 0)
    def _():
        issue_rows(0, 0, range(MOE_ROWS))

    @pl.when(nblk > 0)
    def _():
        w1b[...] = w1_ref[0].astype(BF16)
        w3b[...] = w3_ref[0].astype(BF16)
        w2b[...] = w2_ref[0].astype(BF16)

    n_piece = 4
    rows_per_piece = MOE_ROWS // n_piece
    cols_per_piece = D_MODEL // n_piece

    def block(b, c):
        blk = first + b
        slot = blk % 2
        nxt = jnp.minimum(blk + 1, total - 1)
        wait_gather(slot)

        @pl.when(blk >= 2)
        def _():
            y_copy(blk - 2, slot).wait()

        xb = xbuf[slot].astype(BF16)
        a = jnp.dot(xb, w1b[...], preferred_element_type=F32)
        g = jnp.dot(xb, w3b[...], preferred_element_type=F32)
        hid = (a * jax.nn.sigmoid(a) * g).astype(BF16)
        for p in range(n_piece):
            issue_rows(nxt, 1 - slot, range(p * rows_per_piece, (p + 1) * rows_per_piece))
            cols = slice(p * cols_per_piece, (p + 1) * cols_per_piece)
            ybuf[slot, :, cols] = jnp.dot(hid, w2b[:, cols], preferred_element_type=F32)
        y_copy(blk, slot).start()
        return c

    lax.fori_loop(0, nblk, block, 0)

    @pl.when(e == pl.num_programs(0) - 1)
    def _():
        wait_gather(total % 2)
        for back in (1, 2):
            @pl.when(total >= back)
            def _():
                y_copy(total - back, (total - back) % 2).wait()

        ybuf[0] = jnp.zeros((MOE_ROWS, D_MODEL), F32)

        def zero_block(blk, c):
            y_copy(blk, 0).start()
            y_copy(blk, 0).wait()
            return c

        lax.fori_loop(total, N_ROW_BLOCKS, zero_block, 0)


def _experts(blk_start, nblk, total, row_tok, h2, w1, w3, w2):
    wmap = lambda e, *_: (e, 0, 0)
    grid_spec = pltpu.PrefetchScalarGridSpec(
        num_scalar_prefetch=4,
        grid=(N_EXPERTS,),
        in_specs=[pl.BlockSpec(memory_space=pl.ANY),
                  pl.BlockSpec((1, D_MODEL, D_EXPERT), wmap),
                  pl.BlockSpec((1, D_MODEL, D_EXPERT), wmap),
                  pl.BlockSpec((1, D_EXPERT, D_MODEL), wmap)],
        out_specs=pl.BlockSpec(memory_space=pl.ANY),
        scratch_shapes=[pltpu.VMEM((2, MOE_ROWS, D_MODEL), F32),
                        pltpu.VMEM((2, MOE_ROWS, D_MODEL), F32),
                        pltpu.SemaphoreType.DMA((2,)),
                        pltpu.SemaphoreType.DMA((2,)),
                        pltpu.VMEM((D_MODEL, D_EXPERT), BF16),
                        pltpu.VMEM((D_MODEL, D_EXPERT), BF16),
                        pltpu.VMEM((D_EXPERT, D_MODEL), BF16)],
    )
    return pl.pallas_call(
        _expert_kernel,
        grid_spec=grid_spec,
        out_shape=jax.ShapeDtypeStruct((N_ROWS, D_MODEL), F32),
        compiler_params=_cparams(("arbitrary",)),
        name="experts",
    )(blk_start, nblk, total, row_tok, h2, w1, w3, w2)


def _combine_kernel(pos_ref, y_hbm, h_ref, r_ref, g_ref, beta_ref, o_ref, ybuf, sem, *, tc):
    i = pl.program_id(0)
    n = pl.num_programs(0)
    slot = i % 2

    def start_gather(blk, slot_):
        def issue(r, c):
            for kk in range(TOP_K):
                p = pos_ref[(blk * tc + r) * TOP_K + kk]
                pltpu.make_async_copy(y_hbm.at[pl.ds(p, 1)], ybuf.at[slot_, kk, pl.ds(r, 1)],
                                      sem.at[slot_]).start()
            return c
        lax.fori_loop(0, tc, issue, 0, unroll=8)

    @pl.when(i == 0)
    def _():
        start_gather(0, 0)

    @pl.when(i + 1 < n)
    def _():
        start_gather(i + 1, 1 - slot)

    for kk in range(TOP_K):
        pltpu.make_async_copy(y_hbm.at[pl.ds(0, tc)], ybuf.at[slot, kk], sem.at[slot]).wait()
    gates = r_ref[...]
    moe = (ybuf[slot, 0] * gates[:, TOP_K:TOP_K + 1]
           + ybuf[slot, 1] * gates[:, TOP_K + 1:TOP_K + 2])
    o_ref[...] = _layer_norm(DN_ALPHA * h_ref[...] + moe, g_ref[...], beta_ref[...])


def _combine(pos, y_rows, h2, rout, g, b, tc):
    s = h2.shape[0]
    grid_spec = pltpu.PrefetchScalarGridSpec(
        num_scalar_prefetch=1,
        grid=(s // tc,),
        in_specs=[pl.BlockSpec(memory_space=pl.ANY),
                  pl.BlockSpec((tc, D_MODEL), lambda i, pos: (i, 0)),
                  pl.BlockSpec((tc, LANES), lambda i, pos: (i, 0)),
                  pl.BlockSpec((1, D_MODEL), lambda i, pos: (0, 0)),
                  pl.BlockSpec((1, D_MODEL), lambda i, pos: (0, 0))],
        out_specs=pl.BlockSpec((tc, D_MODEL), lambda i, pos: (i, 0)),
        scratch_shapes=[pltpu.VMEM((2, TOP_K, tc, D_MODEL), F32),
                        pltpu.SemaphoreType.DMA((2,))],
    )
    return pl.pallas_call(
        functools.partial(_combine_kernel, tc=tc),
        grid_spec=grid_spec,
        out_shape=jax.ShapeDtypeStruct((s, D_MODEL), F32),
        compiler_params=_cparams(("arbitrary",)),
        name="combine_ln3",
    )(pos, y_rows, h2, rout, g, b)


def kernel(x, mem, w_in, gm_ln_g, gm_ln_b, gm_w_s, gm_b_s, w_mix_out, ln1_g, ln1_b,
           mem_w_q, mem_w_k, mem_w_v, mem_w_o, ln2_g, ln2_b,
           w_group, b_group, w_router, b_router, w1, w3, w2, ln3_g, ln3_b):
    assert x.shape == (1, SEQ, D_MODEL) and w_in.shape[0] == 1
    xs = x[0]
    l = 0
    w_in_l = w_in[l]
    q_lo, k_lo, v_lo = 2 * GM_WIDTH, 2 * GM_WIDTH + SB_WIDTH, 2 * GM_WIDTH + 2 * SB_WIDTH
    w_uvk = jnp.concatenate([w_in_l[:, :q_lo], w_in_l[:, k_lo:v_lo]], axis=1).astype(BF16)
    w_qv_t = jnp.concatenate([w_in_l[:, q_lo:k_lo], w_in_l[:, v_lo:]], axis=1).T.astype(BF16)
    w_mix = w_mix_out[l].astype(BF16)
    w_q = mem_w_q[l].astype(BF16)
    w_kv = jnp.concatenate([mem_w_k[l], mem_w_v[l]], axis=1).astype(BF16)
    w_o = mem_w_o[l].astype(BF16)
    w_r = jnp.concatenate(
        [w_router[l].transpose(1, 0, 2).reshape(D_MODEL, N_EXPERTS), w_group[l],
         jnp.zeros((D_MODEL, LANES - N_EXPERTS - N_GROUPS), F32)], axis=1)
    wr_hi, wr_lo = _split_bf16(w_r)
    b_r = jnp.concatenate([b_router[l].reshape(-1), b_group[l],
                           jnp.zeros((LANES - N_EXPERTS - N_GROUPS,), F32)]).reshape(1, LANES)
    row = lambda v: v.reshape(1, D_MODEL)

    uvk = _proj_nn(xs, w_uvk, tm=1024, tn=1024)
    qvt = _proj_nt(xs, w_qv_t, tm=1024, tr=1024, q_scale=SB_HEAD_DIM ** -0.5 * math.log2(math.e))
    mix_a = _gmlp(uvk, gm_w_s[l], gm_ln_g[l], gm_ln_b[l], gm_b_s[l].T, tg=512)
    mix_b = _stick_breaking(qvt, uvk, tq=256, bk=128)
    h1 = _mixout(mix_a, mix_b, w_mix, xs, row(ln1_g[l]), row(ln1_b[l]), tm=512)
    kv = _proj_nn(mem[0], w_kv, tm=MEM_LEN, tn=1024)
    h2, rout = _memattn(h1, w_q, kv, w_o, row(ln2_g[l]), row(ln2_b[l]), wr_hi, wr_lo, b_r, tm=256)
    flat_e = rout[:, 0:TOP_K].astype(I32).reshape(1, N_ASSIGN)
    row_tok, pos, meta = _dispatch(flat_e)
    blk_start, nblk = meta[:, 0], meta[:, 1]
    total = (blk_start[N_EXPERTS - 1] + nblk[N_EXPERTS - 1]).reshape(1)
    y_rows = _experts(blk_start, nblk, total, row_tok.reshape(N_ROWS), h2, w1[l], w3[l], w2[l])
    out = _combine(pos.reshape(N_ASSIGN), y_rows, h2, rout, row(ln3_g[l]), row(ln3_b[l]), tc=256)
    return out.reshape(1, SEQ, D_MODEL)
```

```python
import functools
import math

import jax
import jax.numpy as jnp
from jax import lax
from jax.experimental import pallas as pl
from jax.experimental.pallas import tpu as pltpu

D_MODEL = 2048
SEQ = 8192
CHUNK = 64
MEM_LEN = 256
GM_BLOCK = 128
GM_GROUPS = 8
GM_WIDTH = 1024
SB_HEADS = 8
SB_HEAD_DIM = 128
SB_WIDTH = 1024
MEM_HEADS = 4
MEM_HEAD_DIM = 512
N_GROUPS = 8
EXPERTS_PER_GROUP = 8
N_EXPERTS = 64
TOP_K = 2
D_EXPERT = 512
DN_ALPHA = 2.0 ** 0.25
LN_EPS = 1e-5

LANES = 128
MOE_ROWS = 128
MOE_ROWS_LOG2 = 7
N_ASSIGN = SEQ * TOP_K
N_ROW_BLOCKS = N_ASSIGN // MOE_ROWS + N_EXPERTS
N_ROWS = N_ROW_BLOCKS * MOE_ROWS
DISPATCH_CHUNK = 256
GATHER_SLOTS = 3
SB_UNDERFLOW = 160.0
VMEM_LIMIT = 56 * 1024 * 1024

BF16 = jnp.bfloat16
F32 = jnp.float32
I32 = jnp.int32


def _cparams(sem):
    return pltpu.CompilerParams(dimension_semantics=sem, vmem_limit_bytes=VMEM_LIMIT)


def _resident(shape, index_map):
    return pl.BlockSpec(shape, index_map, pipeline_mode=pl.Buffered(1))


def _layer_norm(y, g, b):
    mu = jnp.mean(y, axis=-1, keepdims=True)
    yc = y - mu
    var = jnp.mean(yc * yc, axis=-1, keepdims=True)
    return yc * lax.rsqrt(var + LN_EPS) * g + b


def _split_bf16(v):
    hi = v.astype(BF16)
    lo = (v - hi.astype(F32)).astype(BF16)
    return hi, lo


def _proj_nn_kernel(x_ref, w_ref, o_ref, xb_ref):
    @pl.when(pl.program_id(1) == 0)
    def _():
        xb_ref[...] = x_ref[...].astype(BF16)

    o_ref[...] = jnp.dot(xb_ref[...], w_ref[...], preferred_element_type=F32).astype(o_ref.dtype)


def _proj_nn(x, w, tm, tn):
    m, k = x.shape
    n = w.shape[1]
    return pl.pallas_call(
        _proj_nn_kernel,
        grid=(m // tm, n // tn),
        in_specs=[pl.BlockSpec((tm, k), lambda i, j: (i, 0)),
                  pl.BlockSpec((k, tn), lambda i, j: (0, j))],
        out_specs=pl.BlockSpec((tm, tn), lambda i, j: (i, j)),
        out_shape=jax.ShapeDtypeStruct((m, n), BF16),
        scratch_shapes=[pltpu.VMEM((tm, k), BF16)],
        compiler_params=_cparams(("arbitrary", "arbitrary")),
        name="proj_nn",
    )(x, w)


def _proj_nt_kernel(x_ref, wt_ref, o_ref, xb_ref, *, q_scale):
    j = pl.program_id(1)

    @pl.when(j == 0)
    def _():
        xb_ref[...] = x_ref[...].astype(BF16)

    acc = lax.dot_general(wt_ref[...], xb_ref[...], (((1,), (1,)), ((), ())),
                          preferred_element_type=F32)
    scale = jnp.where(j == 0, q_scale, 1.0).astype(F32)
    o_ref[...] = (acc * scale).astype(o_ref.dtype)


def _proj_nt(x, wt, tm, tr, q_scale):
    m, k = x.shape
    r = wt.shape[0]
    return pl.pallas_call(
        functools.partial(_proj_nt_kernel, q_scale=q_scale),
        grid=(m // tm, r // tr),
        in_specs=[pl.BlockSpec((tm, k), lambda i, j: (i, 0)),
                  pl.BlockSpec((tr, k), lambda i, j: (j, 0))],
        out_specs=pl.BlockSpec((tr, tm), lambda i, j: (j, i)),
        out_shape=jax.ShapeDtypeStruct((r, m), BF16),
        scratch_shapes=[pltpu.VMEM((tm, k), BF16)],
        compiler_params=_cparams(("arbitrary", "arbitrary")),
        name="proj_nt",
    )(x, wt)


def _gmlp_kernel(u_ref, v_ref, ws_ref, g_ref, b_ref, bs_ref, o_ref, *, tg):
    t_chunk = lax.broadcasted_iota(I32, (GM_BLOCK, GM_BLOCK), 0) // CHUNK
    s_chunk = lax.broadcasted_iota(I32, (GM_BLOCK, GM_BLOCK), 1) // CHUNK
    causal = t_chunk >= s_chunk
    for g in range(GM_GROUPS):
        wm = jnp.where(causal, ws_ref[g], 0.0).astype(BF16)
        ln_g = g_ref[g:g + 1, :]
        ln_b = b_ref[g:g + 1, :]
        bias = bs_ref[:, g:g + 1]
        cols = slice(g * LANES, (g + 1) * LANES)
        for n in range(tg // GM_BLOCK):
            rows = slice(n * GM_BLOCK, (n + 1) * GM_BLOCK)
            v = jax.nn.gelu(v_ref[rows, cols].astype(F32))
            vg = _layer_norm(v, ln_g, ln_b).astype(BF16)
            mixed = jnp.dot(wm, vg, preferred_element_type=F32) + bias
            u = jax.nn.gelu(u_ref[rows, cols].astype(F32))
            o_ref[rows, cols] = (u * mixed).astype(o_ref.dtype)


def _gmlp(uvk, gm_w_s, gm_ln_g, gm_ln_b, gm_b_s_t, tg):
    s = uvk.shape[0]
    return pl.pallas_call(
        functools.partial(_gmlp_kernel, tg=tg),
        grid=(s // tg,),
        in_specs=[pl.BlockSpec((tg, GM_WIDTH), lambda i: (i, 0)),
                  pl.BlockSpec((tg, GM_WIDTH), lambda i: (i, 1)),
                  pl.BlockSpec((GM_GROUPS, GM_BLOCK, GM_BLOCK), lambda i: (0, 0, 0)),
                  pl.BlockSpec((GM_GROUPS, LANES), lambda i: (0, 0)),
                  pl.BlockSpec((GM_GROUPS, LANES), lambda i: (0, 0)),
                  pl.BlockSpec((GM_BLOCK, GM_GROUPS), lambda i: (0, 0))],
        out_specs=pl.BlockSpec((tg, GM_WIDTH), lambda i: (i, 0)),
        out_shape=jax.ShapeDtypeStruct((s, GM_WIDTH), BF16),
        compiler_params=_cparams(("arbitrary",)),
        name="gmlp",
    )(uvk, uvk, gm_w_s, gm_ln_g, gm_ln_b, gm_b_s_t)


def _softplus2(u):
    return jnp.maximum(u, 0.0) + jnp.log2(1.0 + jnp.exp2(-jnp.abs(u)))


def _sb_kernel(qt_ref, k_ref, vt_ref, o_ref, *acc_refs, tq, bk):
    q0 = pl.program_id(0) * tq
    upper = (lax.broadcasted_iota(I32, (bk, bk), 1)
             >= lax.broadcasted_iota(I32, (bk, bk), 0)).astype(BF16)
    upper2 = jnp.concatenate([upper, upper], axis=1)

    heads = range(SB_HEADS)
    hrows = [slice(h * SB_HEAD_DIM, (h + 1) * SB_HEAD_DIM) for h in heads]

    def tiles(ks, carries, mask):
        zs = [jnp.dot(k_ref[pl.ds(ks, bk), hrows[h]], qt_ref[hrows[h], :],
                      preferred_element_type=F32) for h in heads]
        sps = [_softplus2(z) for z in zs]
        if mask is not None:
            sps = [jnp.where(mask, sp, 0.0) for sp in sps]
        hls = [jnp.concatenate(_split_bf16(sp), axis=0) for sp in sps]
        csums = [jnp.dot(upper2, hls[h], preferred_element_type=F32) + carries[h] for h in heads]
        probs = [jnp.exp2(zs[h] - csums[h]) for h in heads]
        if mask is not None:
            probs = [jnp.where(mask, a, 0.0) for a in probs]
        for h in heads:
            acc_refs[h][...] += jnp.dot(vt_ref[hrows[h], pl.ds(ks, bk)], probs[h].astype(BF16),
                                        preferred_element_type=F32)
        return [csum[0:1, :] for csum in csums]

    for acc_ref in acc_refs:
        acc_ref[...] = jnp.zeros_like(acc_ref)
    carries = [jnp.zeros((1, tq), F32) for _ in range(SB_HEADS)]
    key_in_tile = lax.broadcasted_iota(I32, (bk, tq), 0)
    q_in_tile = lax.broadcasted_iota(I32, (bk, tq), 1)
    for d in reversed(range(tq // bk)):
        mask = (key_in_tile + d * bk) < q_in_tile
        ks = pl.multiple_of(q0 + d * bk, bk)
        carries = tiles(ks, carries, mask)

    def cond(state):
        j = state[0]
        lowest = functools.reduce(jnp.minimum, state[1:])
        return jnp.logical_and(j >= 0, jnp.min(lowest) < SB_UNDERFLOW)

    def body(state):
        j = state[0]
        ks = pl.multiple_of(j * bk, bk)
        return (j - 1,) + tuple(tiles(ks, state[1:], None))

    lax.while_loop(cond, body, (q0 // bk - 1,) + tuple(carries))
    for h in range(SB_HEADS):
        o_ref[:, h * SB_HEAD_DIM:(h + 1) * SB_HEAD_DIM] = acc_refs[h][...].T.astype(o_ref.dtype)


def _stick_breaking(qvt, uvk, tq, bk):
    s = uvk.shape[0]
    return pl.pallas_call(
        functools.partial(_sb_kernel, tq=tq, bk=bk),
        grid=(s // tq,),
        in_specs=[pl.BlockSpec((SB_WIDTH, tq), lambda i: (0, i)),
                  _resident((s, SB_WIDTH), lambda i: (0, 2 * GM_WIDTH // SB_WIDTH)),
                  _resident((SB_WIDTH, s), lambda i: (1, 0))],
        out_specs=pl.BlockSpec((tq, SB_WIDTH), lambda i: (i, 0)),
        out_shape=jax.ShapeDtypeStruct((s, SB_WIDTH), BF16),
        scratch_shapes=[pltpu.VMEM((SB_HEAD_DIM, tq), F32) for _ in range(SB_HEADS)],
        compiler_params=_cparams(("arbitrary",)),
        name="stick_breaking",
    )(qvt, uvk, qvt)


def _mixout_kernel(a_ref, b_ref, wa_ref, wb_ref, x_ref, g_ref, beta_ref, o_ref):
    mixed = (jnp.dot(a_ref[...], wa_ref[...], preferred_element_type=F32)
             + jnp.dot(b_ref[...], wb_ref[...], preferred_element_type=F32))
    y = DN_ALPHA * x_ref[...] + mixed
    o_ref[...] = _layer_norm(y, g_ref[...], beta_ref[...])


def _mixout(mix_a, mix_b, w, x, g, b, tm):
    s = x.shape[0]
    return pl.pallas_call(
        _mixout_kernel,
        grid=(s // tm,),
        in_specs=[pl.BlockSpec((tm, GM_WIDTH), lambda i: (i, 0)),
                  pl.BlockSpec((tm, SB_WIDTH), lambda i: (i, 0)),
                  pl.BlockSpec((GM_WIDTH, D_MODEL), lambda i: (0, 0)),
                  pl.BlockSpec((SB_WIDTH, D_MODEL), lambda i: (1, 0)),
                  pl.BlockSpec((tm, D_MODEL), lambda i: (i, 0)),
                  pl.BlockSpec((1, D_MODEL), lambda i: (0, 0)),
                  pl.BlockSpec((1, D_MODEL), lambda i: (0, 0))],
        out_specs=pl.BlockSpec((tm, D_MODEL), lambda i: (i, 0)),
        out_shape=jax.ShapeDtypeStruct((s, D_MODEL), F32),
        compiler_params=_cparams(("arbitrary",)),
        name="mixout_ln1",
    )(mix_a, mix_b, w, w, x, g, b)


def _route(logits):
    lane = lax.broadcasted_iota(I32, logits.shape, 1)
    neg = jnp.float32(-jnp.inf)
    big = jnp.int32(1 << 20)

    def first_argmax(vals):
        m = jnp.max(vals, axis=-1, keepdims=True)
        idx = jnp.min(jnp.where(vals == m, lane, big), axis=-1, keepdims=True)
        return m, idx

    is_group = jnp.logical_and(lane >= N_EXPERTS, lane < N_EXPERTS + N_GROUPS)
    gl = jnp.where(is_group, logits, neg)
    gmax, glane = first_argmax(gl)
    g_val = 1.0 / jnp.sum(jnp.exp(gl - gmax), axis=-1, keepdims=True)
    gidx = glane - N_EXPERTS
    in_group = jnp.logical_and(lane >= gidx * EXPERTS_PER_GROUP,
                               lane < (gidx + 1) * EXPERTS_PER_GROUP)
    el = jnp.where(in_group, logits, neg)
    m1, i1 = first_argmax(el)
    m2, i2 = first_argmax(jnp.where(lane == i1, neg, el))
    e2 = jnp.exp(m2 - m1)
    gate1 = g_val / (1.0 + e2)
    gate2 = g_val * e2 / (1.0 + e2)
    return jnp.where(lane == 0, i1.astype(F32),
                     jnp.where(lane == 1, i2.astype(F32),
                               jnp.where(lane == 2, gate1,
                                         jnp.where(lane == 3, gate2, 0.0))))


def _memattn_kernel(h_ref, wq_ref, kv_ref, wo_ref, g_ref, beta_ref, wrh_ref, wrl_ref, br_ref,
                    o_ref, r_ref):
    h = h_ref[...]
    q = jnp.dot(h.astype(BF16), wq_ref[...], preferred_element_type=F32).astype(BF16)
    outs = []
    for hd in range(MEM_HEADS):
        cols = slice(hd * MEM_HEAD_DIM, (hd + 1) * MEM_HEAD_DIM)
        kh = kv_ref[:, cols]
        vh = kv_ref[:, D_MODEL + hd * MEM_HEAD_DIM:D_MODEL + (hd + 1) * MEM_HEAD_DIM]
        s = lax.dot_general(q[:, cols], kh, (((1,), (1,)), ((), ())),
                            preferred_element_type=F32) * (MEM_HEAD_DIM ** -0.5)
        s = s - jnp.max(s, axis=-1, keepdims=True)
        e = jnp.exp(s)
        p = e / jnp.sum(e, axis=-1, keepdims=True)
        outs.append(jnp.dot(p.astype(BF16), vh, preferred_element_type=F32).astype(BF16))
    o = jnp.concatenate(outs, axis=-1)
    attn = jnp.dot(o, wo_ref[...], preferred_element_type=F32)
    h2 = _layer_norm(DN_ALPHA * h + attn, g_ref[...], beta_ref[...])
    o_ref[...] = h2
    h_hi, h_lo = _split_bf16(h2)
    logits = (jnp.dot(h_hi, wrh_ref[...], preferred_element_type=F32)
              + jnp.dot(h_lo, wrh_ref[...], preferred_element_type=F32)
              + jnp.dot(h_hi, wrl_ref[...], preferred_element_type=F32)) + br_ref[...]
    r_ref[...] = _route(logits)


def _memattn(h1, wq, kv, wo, g, b, wr_hi, wr_lo, b_r, tm):
    s = h1.shape[0]
    return pl.pallas_call(
        _memattn_kernel,
        grid=(s // tm,),
        in_specs=[pl.BlockSpec((tm, D_MODEL), lambda i: (i, 0)),
                  _resident((D_MODEL, D_MODEL), lambda i: (0, 0)),
                  _resident((MEM_LEN, 2 * D_MODEL), lambda i: (0, 0)),
                  _resident((D_MODEL, D_MODEL), lambda i: (0, 0)),
                  pl.BlockSpec((1, D_MODEL), lambda i: (0, 0)),
                  pl.BlockSpec((1, D_MODEL), lambda i: (0, 0)),
                  _resident((D_MODEL, LANES), lambda i: (0, 0)),
                  _resident((D_MODEL, LANES), lambda i: (0, 0)),
                  pl.BlockSpec((1, LANES), lambda i: (0, 0))],
        out_specs=[pl.BlockSpec((tm, D_MODEL), lambda i: (i, 0)),
                   pl.BlockSpec((tm, LANES), lambda i: (i, 0))],
        out_shape=[jax.ShapeDtypeStruct((s, D_MODEL), F32),
                   jax.ShapeDtypeStruct((s, LANES), F32)],
        compiler_params=_cparams(("arbitrary",)),
        name="memattn_ln2_router",
    )(h1, wq, kv, wo, g, b, wr_hi, wr_lo, b_r)


def _dispatch_kernel(e_ref, tok_ref, pos_ref, meta_ref):
    ca = DISPATCH_CHUNK
    n_chunks = N_ASSIGN // ca
    e_iota = lax.broadcasted_iota(I32, (N_EXPERTS, ca), 0)

    def onehot(c):
        e_c = e_ref[:, pl.ds(pl.multiple_of(c * ca, ca), ca)]
        return (e_iota == e_c).astype(F32)

    counts = lax.fori_loop(
        0, n_chunks, lambda c, acc: acc + jnp.sum(onehot(c), axis=1, keepdims=True),
        jnp.zeros((N_EXPERTS, 1), F32))
    nblk = jnp.right_shift(counts.astype(I32) + (MOE_ROWS - 1), MOE_ROWS_LOG2)
    strict_lower = (lax.broadcasted_iota(I32, (N_EXPERTS, N_EXPERTS), 1)
                    < lax.broadcasted_iota(I32, (N_EXPERTS, N_EXPERTS), 0)).astype(BF16)
    nblk_lanes = jnp.broadcast_to(nblk.astype(F32), (N_EXPERTS, LANES)).astype(BF16)
    blk_start = jnp.dot(strict_lower, nblk_lanes, preferred_element_type=F32)
    pad_start = blk_start[:, 0:1] * MOE_ROWS
    lane = lax.broadcasted_iota(I32, (N_EXPERTS, LANES), 1)
    meta_ref[...] = jnp.where(lane == 0, blk_start.astype(I32),
                              jnp.where(lane == 1, jnp.broadcast_to(nblk, (N_EXPERTS, LANES)), 0))

    earlier = (lax.broadcasted_iota(I32, (ca, ca), 0)
               < lax.broadcasted_iota(I32, (ca, ca), 1)).astype(BF16)
    blk_iota = lax.broadcasted_iota(I32, (N_ROW_BLOCKS, ca), 0)
    row_iota = lax.broadcasted_iota(I32, (MOE_ROWS, ca), 0)
    a_iota = lax.broadcasted_iota(I32, (1, ca), 1)

    def body(c, state):
        seen, acc_hi, acc_lo = state
        oh = onehot(c)
        before = jnp.dot(oh.astype(BF16), earlier, preferred_element_type=F32) + seen
        dest = jnp.sum(oh * (before + pad_start), axis=0, keepdims=True).astype(I32)
        pos_ref[:, pl.ds(pl.multiple_of(c * ca, ca), ca)] = dest
        in_blk = (blk_iota == jnp.right_shift(dest, MOE_ROWS_LOG2)).astype(F32)
        in_row = (row_iota == jnp.bitwise_and(dest, MOE_ROWS - 1)).astype(BF16)
        tok = jnp.right_shift(c * ca + a_iota, 1)
        tok_hi = jnp.right_shift(tok, 7).astype(F32)
        tok_lo = jnp.bitwise_and(tok, 127).astype(F32)
        nt = (((1,), (1,)), ((), ()))
        acc_hi = acc_hi + lax.dot_general((in_blk * tok_hi).astype(BF16), in_row, nt,
                                          preferred_element_type=F32)
        acc_lo = acc_lo + lax.dot_general((in_blk * tok_lo).astype(BF16), in_row, nt,
                                          preferred_element_type=F32)
        return seen + jnp.sum(oh, axis=1, keepdims=True), acc_hi, acc_lo

    zeros = jnp.zeros((N_ROW_BLOCKS, MOE_ROWS), F32)
    _, acc_hi, acc_lo = lax.fori_loop(0, n_chunks, body,
                                      (jnp.zeros((N_EXPERTS, 1), F32), zeros, zeros))
    tok_ref[...] = (acc_hi * 128.0 + acc_lo).astype(I32)


def _dispatch(flat_e):
    return pl.pallas_call(
        _dispatch_kernel,
        out_shape=[jax.ShapeDtypeStruct((N_ROW_BLOCKS, MOE_ROWS), I32),
                   jax.ShapeDtypeStruct((1, N_ASSIGN), I32),
                   jax.ShapeDtypeStruct((N_EXPERTS, LANES), I32)],
        compiler_params=pltpu.CompilerParams(vmem_limit_bytes=VMEM_LIMIT),
        name="moe_dispatch",
    )(flat_e)


def _expert_kernel(bs_ref, nb_ref, tot_ref, tok_ref, h_hbm, w1_hbm, w3_hbm, w2_hbm, y_hbm,
                   xbuf, ybuf, gsem, ysem, wsem, w1f, w3f, w2f, w1b, w3b, w2b):
    e = pl.program_id(0)
    n_exp = pl.num_programs(0)
    total = tot_ref[0]
    first = bs_ref[e]
    nblk = nb_ref[e]

    def issue_rows(blk, rows):
        base = jnp.minimum(blk, N_ROW_BLOCKS - 1) * MOE_ROWS
        slot = blk % GATHER_SLOTS
        for r in rows:
            tok = tok_ref[base + r]
            pltpu.make_async_copy(h_hbm.at[pl.ds(tok, 1)], xbuf.at[slot, pl.ds(r, 1)],
                                  gsem.at[slot]).start()

    def wait_gather(blk):
        slot = blk % GATHER_SLOTS
        pltpu.make_async_copy(h_hbm.at[pl.ds(0, MOE_ROWS)], xbuf.at[slot], gsem.at[slot]).wait()

    def y_copy(blk, slot):
        return pltpu.make_async_copy(ybuf.at[slot], y_hbm.at[pl.ds(blk * MOE_ROWS, MOE_ROWS)],
                                     ysem.at[slot])

    def weight_copies(ex):
        slot = ex % 2
        return [pltpu.make_async_copy(src.at[ex], dst.at[slot], wsem.at[slot])
                for src, dst in ((w1_hbm, w1f), (w3_hbm, w3f), (w2_hbm, w2f))]

    @pl.when(e == 0)
    def _():
        for cp in weight_copies(0):
            cp.start(priority=1)

        def prime(r, c):
            for blk in range(GATHER_SLOTS - 1):
                tok = tok_ref[blk * MOE_ROWS + r]
                pltpu.make_async_copy(h_hbm.at[pl.ds(tok, 1)], xbuf.at[blk, pl.ds(r, 1)],
                                      gsem.at[blk]).start()
            return c
        lax.fori_loop(0, MOE_ROWS, prime, 0, unroll=8)

    @pl.when(e + 1 < n_exp)
    def _():
        for cp in weight_copies(e + 1):
            cp.start(priority=1)

    for cp in weight_copies(e):
        cp.wait()

    @pl.when(nblk > 0)
    def _():
        wslot = e % 2
        w1b[...] = w1f[wslot].astype(BF16)
        w3b[...] = w3f[wslot].astype(BF16)
        w2b[...] = w2f[wslot].astype(BF16)

    n_piece = 4
    rows_per_piece = MOE_ROWS // n_piece
    cols_per_piece = D_MODEL // n_piece

    def block(b, c):
        blk = first + b
        slot = blk % 2
        wait_gather(blk)

        @pl.when(blk >= 2)
        def _():
            y_copy(blk - 2, slot).wait()

        xb = xbuf[blk % GATHER_SLOTS].astype(BF16)
        a = jnp.dot(xb, w1b[...], preferred_element_type=F32)
        g = jnp.dot(xb, w3b[...], preferred_element_type=F32)
        hid = (a * jax.nn.sigmoid(a) * g).astype(BF16)
        for p in range(n_piece):
            issue_rows(blk + GATHER_SLOTS - 1,
                       range(p * rows_per_piece, (p + 1) * rows_per_piece))
            cols = slice(p * cols_per_piece, (p + 1) * cols_per_piece)
            ybuf[slot, :, cols] = jnp.dot(hid, w2b[:, cols], preferred_element_type=F32)
        y_copy(blk, slot).start()
        return c

    lax.fori_loop(0, nblk, block, 0)

    @pl.when(e == n_exp - 1)
    def _():
        for ahead in range(GATHER_SLOTS - 1):
            wait_gather(total + ahead)
        for back in (1, 2):
            @pl.when(total >= back)
            def _():
                y_copy(total - back, (total - back) % 2).wait()

        ybuf[0] = jnp.zeros((MOE_ROWS, D_MODEL), F32)

        def zero_block(blk, c):
            y_copy(blk, 0).start()
            y_copy(blk, 0).wait()
            return c

        lax.fori_loop(total, N_ROW_BLOCKS, zero_block, 0)


def _experts(blk_start, nblk, total, row_tok, h2, w1, w3, w2):
    grid_spec = pltpu.PrefetchScalarGridSpec(
        num_scalar_prefetch=4,
        grid=(N_EXPERTS,),
        in_specs=[pl.BlockSpec(memory_space=pl.ANY)] * 4,
        out_specs=pl.BlockSpec(memory_space=pl.ANY),
        scratch_shapes=[pltpu.VMEM((GATHER_SLOTS, MOE_ROWS, D_MODEL), F32),
                        pltpu.VMEM((2, MOE_ROWS, D_MODEL), F32),
                        pltpu.SemaphoreType.DMA((GATHER_SLOTS,)),
                        pltpu.SemaphoreType.DMA((2,)),
                        pltpu.SemaphoreType.DMA((2,)),
                        pltpu.VMEM((2, D_MODEL, D_EXPERT), F32),
                        pltpu.VMEM((2, D_MODEL, D_EXPERT), F32),
                        pltpu.VMEM((2, D_EXPERT, D_MODEL), F32),
                        pltpu.VMEM((D_MODEL, D_EXPERT), BF16),
                        pltpu.VMEM((D_MODEL, D_EXPERT), BF16),
                        pltpu.VMEM((D_EXPERT, D_MODEL), BF16)],
    )
    return pl.pallas_call(
        _expert_kernel,
        grid_spec=grid_spec,
        out_shape=jax.ShapeDtypeStruct((N_ROWS, D_MODEL), F32),
        compiler_params=_cparams(("arbitrary",)),
        name="experts",
    )(blk_start, nblk, total, row_tok, h2, w1, w3, w2)


def _combine_kernel(pos_ref, y_hbm, h_ref, r_ref, g_ref, beta_ref, o_ref, ybuf, sem, *, tc):
    i = pl.program_id(0)
    n = pl.num_programs(0)
    slot = i % 2

    def start_gather(blk, slot_):
        def issue(r, c):
            for kk in range(TOP_K):
                p = pos_ref[(blk * tc + r) * TOP_K + kk]
                pltpu.make_async_copy(y_hbm.at[pl.ds(p, 1)], ybuf.at[slot_, kk, pl.ds(r, 1)],
                                      sem.at[slot_]).start()
            return c
        lax.fori_loop(0, tc, issue, 0, unroll=8)

    @pl.when(i == 0)
    def _():
        start_gather(0, 0)

    @pl.when(i + 1 < n)
    def _():
        start_gather(i + 1, 1 - slot)

    for kk in range(TOP_K):
        pltpu.make_async_copy(y_hbm.at[pl.ds(0, tc)], ybuf.at[slot, kk], sem.at[slot]).wait()
    gates = r_ref[...]
    moe = (ybuf[slot, 0] * gates[:, TOP_K:TOP_K + 1]
           + ybuf[slot, 1] * gates[:, TOP_K + 1:TOP_K + 2])
    o_ref[...] = _layer_norm(DN_ALPHA * h_ref[...] + moe, g_ref[...], beta_ref[...])


def _combine(pos, y_rows, h2, rout, g, b, tc):
    s = h2.shape[0]
    grid_spec = pltpu.PrefetchScalarGridSpec(
        num_scalar_prefetch=1,
        grid=(s // tc,),
        in_specs=[pl.BlockSpec(memory_space=pl.ANY),
                  pl.BlockSpec((tc, D_MODEL), lambda i, pos: (i, 0)),
                  pl.BlockSpec((tc, LANES), lambda i, pos: (i, 0)),
                  pl.BlockSpec((1, D_MODEL), lambda i, pos: (0, 0)),
                  pl.BlockSpec((1, D_MODEL), lambda i, pos: (0, 0))],
        out_specs=pl.BlockSpec((tc, D_MODEL), lambda i, pos: (i, 0)),
        scratch_shapes=[pltpu.VMEM((2, TOP_K, tc, D_MODEL), F32),
                        pltpu.SemaphoreType.DMA((2,))],
    )
    return pl.pallas_call(
        functools.partial(_combine_kernel, tc=tc),
        grid_spec=grid_spec,
        out_shape=jax.ShapeDtypeStruct((s, D_MODEL), F32),
        compiler_params=_cparams(("arbitrary",)),
        name="combine_ln3",
    )(pos, y_rows, h2, rout, g, b)


def kernel(x, mem, w_in, gm_ln_g, gm_ln_b, gm_w_s, gm_b_s, w_mix_out, ln1_g, ln1_b,
           mem_w_q, mem_w_k, mem_w_v, mem_w_o, ln2_g, ln2_b,
           w_group, b_group, w_router, b_router, w1, w3, w2, ln3_g, ln3_b):
    assert x.shape == (1, SEQ, D_MODEL) and w_in.shape[0] == 1
    xs = x[0]
    l = 0
    w_in_l = w_in[l]
    q_lo, k_lo, v_lo = 2 * GM_WIDTH, 2 * GM_WIDTH + SB_WIDTH, 2 * GM_WIDTH + 2 * SB_WIDTH
    w_uvk = jnp.concatenate([w_in_l[:, :q_lo], w_in_l[:, k_lo:v_lo]], axis=1).astype(BF16)
    w_qv_t = jnp.concatenate([w_in_l[:, q_lo:k_lo], w_in_l[:, v_lo:]], axis=1).T.astype(BF16)
    w_mix = w_mix_out[l].astype(BF16)
    w_q = mem_w_q[l].astype(BF16)
    w_kv = jnp.concatenate([mem_w_k[l], mem_w_v[l]], axis=1).astype(BF16)
    w_o = mem_w_o[l].astype(BF16)
    w_r = jnp.concatenate(
        [w_router[l].transpose(1, 0, 2).reshape(D_MODEL, N_EXPERTS), w_group[l],
         jnp.zeros((D_MODEL, LANES - N_EXPERTS - N_GROUPS), F32)], axis=1)
    wr_hi, wr_lo = _split_bf16(w_r)
    b_r = jnp.concatenate([b_router[l].reshape(-1), b_group[l],
                           jnp.zeros((LANES - N_EXPERTS - N_GROUPS,), F32)]).reshape(1, LANES)
    row = lambda v: v.reshape(1, D_MODEL)

    uvk = _proj_nn(xs, w_uvk, tm=1024, tn=1024)
    qvt = _proj_nt(xs, w_qv_t, tm=1024, tr=1024, q_scale=SB_HEAD_DIM ** -0.5 * math.log2(math.e))
    mix_a = _gmlp(uvk, gm_w_s[l], gm_ln_g[l], gm_ln_b[l], gm_b_s[l].T, tg=512)
    mix_b = _stick_breaking(qvt, uvk, tq=256, bk=128)
    h1 = _mixout(mix_a, mix_b, w_mix, xs, row(ln1_g[l]), row(ln1_b[l]), tm=512)
    kv = _proj_nn(mem[0], w_kv, tm=MEM_LEN, tn=1024)
    h2, rout = _memattn(h1, w_q, kv, w_o, row(ln2_g[l]), row(ln2_b[l]), wr_hi, wr_lo, b_r, tm=256)
    flat_e = rout[:, 0:TOP_K].astype(I32).reshape(1, N_ASSIGN)
    row_tok, pos, meta = _dispatch(flat_e)
    blk_start, nblk = meta[:, 0], meta[:, 1]
    total = (blk_start[N_EXPERTS - 1] + nblk[N_EXPERTS - 1]).reshape(1)
    y_rows = _experts(blk_start, nblk, total, row_tok.reshape(N_ROWS), h2, w1[l], w3[l], w2[l])
    out = _combine(pos.reshape(N_ASSIGN), y_rows, h2, rout, row(ln3_g[l]), row(ln3_b[l]), tc=256)
    return out.reshape(1, SEQ, D_MODEL)
```

```python
import functools
import math

import jax
import jax.numpy as jnp
from jax import lax
from jax.experimental import pallas as pl
from jax.experimental.pallas import tpu as pltpu

D_MODEL = 2048
SEQ = 8192
CHUNK = 64
MEM_LEN = 256
GM_BLOCK = 128
GM_GROUPS = 8
GM_WIDTH = 1024
SB_HEADS = 8
SB_HEAD_DIM = 128
SB_WIDTH = 1024
MEM_HEADS = 4
MEM_HEAD_DIM = 512
N_GROUPS = 8
EXPERTS_PER_GROUP = 8
N_EXPERTS = 64
TOP_K = 2
D_EXPERT = 512
DN_ALPHA = 2.0 ** 0.25
LN_EPS = 1e-5

LANES = 128
MOE_ROWS = 128
MOE_ROWS_LOG2 = 7
N_ASSIGN = SEQ * TOP_K
N_ROW_BLOCKS = N_ASSIGN // MOE_ROWS + N_EXPERTS
N_ROWS = N_ROW_BLOCKS * MOE_ROWS
DISPATCH_CHUNK = 256
GATHER_SLOTS = 3
SB_UNDERFLOW = 160.0
VMEM_LIMIT = 56 * 1024 * 1024

BF16 = jnp.bfloat16
F32 = jnp.float32
I32 = jnp.int32


def _cparams(sem):
    return pltpu.CompilerParams(dimension_semantics=sem, vmem_limit_bytes=VMEM_LIMIT)


def _resident(shape, index_map):
    return pl.BlockSpec(shape, index_map, pipeline_mode=pl.Buffered(1))


def _layer_norm(y, g, b):
    mu = jnp.mean(y, axis=-1, keepdims=True)
    yc = y - mu
    var = jnp.mean(yc * yc, axis=-1, keepdims=True)
    return yc * lax.rsqrt(var + LN_EPS) * g + b


def _split_bf16(v):
    hi = v.astype(BF16)
    lo = (v - hi.astype(F32)).astype(BF16)
    return hi, lo


def _proj_kernel(x_ref, w_ref, o_ref, xb_ref, *, transposed, first_scale):
    j = pl.program_id(1)

    @pl.when(j == 0)
    def _():
        xb_ref[...] = x_ref[...].astype(BF16)

    wb = w_ref[...].astype(BF16)
    if transposed:
        acc = lax.dot_general(wb, xb_ref[...], (((0,), (1,)), ((), ())), preferred_element_type=F32)
    else:
        acc = jnp.dot(xb_ref[...], wb, preferred_element_type=F32)
    if first_scale is not None:
        acc = acc * jnp.where(j == 0, first_scale, 1.0).astype(F32)
    o_ref[...] = acc.astype(o_ref.dtype)


def _proj(x, w, col_blocks, tm, tn, transposed=False, first_scale=None):
    m, k = x.shape
    nblk = len(col_blocks)
    wmap = lambda i, j: (0, sum(jnp.where(j == t, c, 0) for t, c in enumerate(col_blocks)))
    out_spec = (pl.BlockSpec((tn, tm), lambda i, j: (j, i)) if transposed
                else pl.BlockSpec((tm, tn), lambda i, j: (i, j)))
    out_shape = (nblk * tn, m) if transposed else (m, nblk * tn)
    return pl.pallas_call(
        functools.partial(_proj_kernel, transposed=transposed, first_scale=first_scale),
        grid=(m // tm, nblk),
        in_specs=[pl.BlockSpec((tm, k), lambda i, j: (i, 0)),
                  pl.BlockSpec((k, tn), wmap)],
        out_specs=out_spec,
        out_shape=jax.ShapeDtypeStruct(out_shape, BF16),
        scratch_shapes=[pltpu.VMEM((tm, k), BF16)],
        compiler_params=_cparams(("arbitrary", "arbitrary")),
        name="proj_t" if transposed else "proj",
    )(x, w)


def _gmlp_kernel(u_ref, v_ref, ws_ref, g_ref, b_ref, bs_ref, o_ref, *, tg):
    t_chunk = lax.broadcasted_iota(I32, (GM_BLOCK, GM_BLOCK), 0) // CHUNK
    s_chunk = lax.broadcasted_iota(I32, (GM_BLOCK, GM_BLOCK), 1) // CHUNK
    causal = t_chunk >= s_chunk
    for g in range(GM_GROUPS):
        wm = jnp.where(causal, ws_ref[g], 0.0).astype(BF16)
        ln_g = g_ref[g:g + 1, :]
        ln_b = b_ref[g:g + 1, :]
        bias = bs_ref[:, g:g + 1]
        cols = slice(g * LANES, (g + 1) * LANES)
        for n in range(tg // GM_BLOCK):
            rows = slice(n * GM_BLOCK, (n + 1) * GM_BLOCK)
            v = jax.nn.gelu(v_ref[rows, cols].astype(F32))
            vg = _layer_norm(v, ln_g, ln_b).astype(BF16)
            mixed = jnp.dot(wm, vg, preferred_element_type=F32) + bias
            u = jax.nn.gelu(u_ref[rows, cols].astype(F32))
            o_ref[rows, cols] = (u * mixed).astype(o_ref.dtype)


def _gmlp(uvk, gm_w_s, gm_ln_g, gm_ln_b, gm_b_s_t, tg):
    s = uvk.shape[0]
    return pl.pallas_call(
        functools.partial(_gmlp_kernel, tg=tg),
        grid=(s // tg,),
        in_specs=[pl.BlockSpec((tg, GM_WIDTH), lambda i: (i, 0)),
                  pl.BlockSpec((tg, GM_WIDTH), lambda i: (i, 1)),
                  pl.BlockSpec((GM_GROUPS, GM_BLOCK, GM_BLOCK), lambda i: (0, 0, 0)),
                  pl.BlockSpec((GM_GROUPS, LANES), lambda i: (0, 0)),
                  pl.BlockSpec((GM_GROUPS, LANES), lambda i: (0, 0)),
                  pl.BlockSpec((GM_BLOCK, GM_GROUPS), lambda i: (0, 0))],
        out_specs=pl.BlockSpec((tg, GM_WIDTH), lambda i: (i, 0)),
        out_shape=jax.ShapeDtypeStruct((s, GM_WIDTH), BF16),
        compiler_params=_cparams(("arbitrary",)),
        name="gmlp",
    )(uvk, uvk, gm_w_s, gm_ln_g, gm_ln_b, gm_b_s_t)


def _softplus2(u):
    return jnp.maximum(u, 0.0) + jnp.log2(1.0 + jnp.exp2(-jnp.abs(u)))


def _sb_kernel(qt_ref, k_ref, vt_ref, o_ref, *acc_refs, tq, bk):
    q0 = pl.program_id(0) * tq
    upper = (lax.broadcasted_iota(I32, (bk, bk), 1)
             >= lax.broadcasted_iota(I32, (bk, bk), 0)).astype(BF16)
    upper2 = jnp.concatenate([upper, upper], axis=1)

    heads = range(SB_HEADS)
    hrows = [slice(h * SB_HEAD_DIM, (h + 1) * SB_HEAD_DIM) for h in heads]

    def tiles(ks, carries, mask):
        zs = [jnp.dot(k_ref[pl.ds(ks, bk), hrows[h]], qt_ref[hrows[h], :],
                      preferred_element_type=F32) for h in heads]
        sps = [_softplus2(z) for z in zs]
        if mask is not None:
            sps = [jnp.where(mask, sp, 0.0) for sp in sps]
        hls = [jnp.concatenate(_split_bf16(sp), axis=0) for sp in sps]
        csums = [jnp.dot(upper2, hls[h], preferred_element_type=F32) + carries[h] for h in heads]
        probs = [jnp.exp2(zs[h] - csums[h]) for h in heads]
        if mask is not None:
            probs = [jnp.where(mask, a, 0.0) for a in probs]
        for h in heads:
            acc_refs[h][...] += jnp.dot(vt_ref[hrows[h], pl.ds(ks, bk)], probs[h].astype(BF16),
                                        preferred_element_type=F32)
        return [csum[0:1, :] for csum in csums]

    for acc_ref in acc_refs:
        acc_ref[...] = jnp.zeros_like(acc_ref)
    carries = [jnp.zeros((1, tq), F32) for _ in range(SB_HEADS)]
    key_in_tile = lax.broadcasted_iota(I32, (bk, tq), 0)
    q_in_tile = lax.broadcasted_iota(I32, (bk, tq), 1)
    for d in reversed(range(tq // bk)):
        mask = (key_in_tile + d * bk) < q_in_tile
        ks = pl.multiple_of(q0 + d * bk, bk)
        carries = tiles(ks, carries, mask)

    def cond(state):
        j = state[0]
        lowest = functools.reduce(jnp.minimum, state[1:])
        return jnp.logical_and(j >= 0, jnp.min(lowest) < SB_UNDERFLOW)

    def body(state):
        j = state[0]
        ks = pl.multiple_of(j * bk, bk)
        return (j - 1,) + tuple(tiles(ks, state[1:], None))

    lax.while_loop(cond, body, (q0 // bk - 1,) + tuple(carries))
    for h in range(SB_HEADS):
        o_ref[:, h * SB_HEAD_DIM:(h + 1) * SB_HEAD_DIM] = acc_refs[h][...].T.astype(o_ref.dtype)


def _stick_breaking(qvt, uvk, tq, bk):
    s = uvk.shape[0]
    return pl.pallas_call(
        functools.partial(_sb_kernel, tq=tq, bk=bk),
        grid=(s // tq,),
        in_specs=[pl.BlockSpec((SB_WIDTH, tq), lambda i: (0, i)),
                  _resident((s, SB_WIDTH), lambda i: (0, 2 * GM_WIDTH // SB_WIDTH)),
                  _resident((SB_WIDTH, s), lambda i: (1, 0))],
        out_specs=pl.BlockSpec((tq, SB_WIDTH), lambda i: (i, 0)),
        out_shape=jax.ShapeDtypeStruct((s, SB_WIDTH), BF16),
        scratch_shapes=[pltpu.VMEM((SB_HEAD_DIM, tq), F32) for _ in range(SB_HEADS)],
        compiler_params=_cparams(("arbitrary",)),
        name="stick_breaking",
    )(qvt, uvk, qvt)


def _mixout_kernel(a_ref, b_ref, wa_ref, wb_ref, x_ref, g_ref, beta_ref, o_ref):
    mixed = (jnp.dot(a_ref[...], wa_ref[...], preferred_element_type=F32)
             + jnp.dot(b_ref[...], wb_ref[...], preferred_element_type=F32))
    y = DN_ALPHA * x_ref[...] + mixed
    o_ref[...] = _layer_norm(y, g_ref[...], beta_ref[...])


def _mixout(mix_a, mix_b, w, x, g, b, tm):
    s = x.shape[0]
    return pl.pallas_call(
        _mixout_kernel,
        grid=(s // tm,),
        in_specs=[pl.BlockSpec((tm, GM_WIDTH), lambda i: (i, 0)),
                  pl.BlockSpec((tm, SB_WIDTH), lambda i: (i, 0)),
                  pl.BlockSpec((GM_WIDTH, D_MODEL), lambda i: (0, 0)),
                  pl.BlockSpec((SB_WIDTH, D_MODEL), lambda i: (1, 0)),
                  pl.BlockSpec((tm, D_MODEL), lambda i: (i, 0)),
                  pl.BlockSpec((1, D_MODEL), lambda i: (0, 0)),
                  pl.BlockSpec((1, D_MODEL), lambda i: (0, 0))],
        out_specs=pl.BlockSpec((tm, D_MODEL), lambda i: (i, 0)),
        out_shape=jax.ShapeDtypeStruct((s, D_MODEL), F32),
        compiler_params=_cparams(("arbitrary",)),
        name="mixout_ln1",
    )(mix_a, mix_b, w, w, x, g, b)


def _route(logits):
    lane = lax.broadcasted_iota(I32, logits.shape, 1)
    neg = jnp.float32(-jnp.inf)
    big = jnp.int32(1 << 20)

    def first_argmax(vals):
        m = jnp.max(vals, axis=-1, keepdims=True)
        idx = jnp.min(jnp.where(vals == m, lane, big), axis=-1, keepdims=True)
        return m, idx

    is_group = jnp.logical_and(lane >= N_EXPERTS, lane < N_EXPERTS + N_GROUPS)
    gl = jnp.where(is_group, logits, neg)
    gmax, glane = first_argmax(gl)
    g_val = 1.0 / jnp.sum(jnp.exp(gl - gmax), axis=-1, keepdims=True)
    gidx = glane - N_EXPERTS
    in_group = jnp.logical_and(lane >= gidx * EXPERTS_PER_GROUP,
                               lane < (gidx + 1) * EXPERTS_PER_GROUP)
    el = jnp.where(in_group, logits, neg)
    m1, i1 = first_argmax(el)
    m2, i2 = first_argmax(jnp.where(lane == i1, neg, el))
    e2 = jnp.exp(m2 - m1)
    gate1 = g_val / (1.0 + e2)
    gate2 = g_val * e2 / (1.0 + e2)
    return jnp.where(lane == 0, i1.astype(F32),
                     jnp.where(lane == 1, i2.astype(F32),
                               jnp.where(lane == 2, gate1,
                                         jnp.where(lane == 3, gate2, 0.0))))


def _memattn_kernel(h_ref, wq_ref, km_ref, vm_ref, wo_ref, g_ref, beta_ref, wrh_ref, wrl_ref, br_ref,
                    o_ref, r_ref):
    h = h_ref[...]
    q = jnp.dot(h.astype(BF16), wq_ref[...], preferred_element_type=F32).astype(BF16)
    outs = []
    for hd in range(MEM_HEADS):
        cols = slice(hd * MEM_HEAD_DIM, (hd + 1) * MEM_HEAD_DIM)
        kh = km_ref[:, cols]
        vh = vm_ref[:, cols]
        s = lax.dot_general(q[:, cols], kh, (((1,), (1,)), ((), ())),
                            preferred_element_type=F32) * (MEM_HEAD_DIM ** -0.5)
        s = s - jnp.max(s, axis=-1, keepdims=True)
        e = jnp.exp(s)
        p = e / jnp.sum(e, axis=-1, keepdims=True)
        outs.append(jnp.dot(p.astype(BF16), vh, preferred_element_type=F32).astype(BF16))
    o = jnp.concatenate(outs, axis=-1)
    attn = jnp.dot(o, wo_ref[...], preferred_element_type=F32)
    h2 = _layer_norm(DN_ALPHA * h + attn, g_ref[...], beta_ref[...])
    o_ref[...] = h2
    h_hi, h_lo = _split_bf16(h2)
    logits = (jnp.dot(h_hi, wrh_ref[...], preferred_element_type=F32)
              + jnp.dot(h_lo, wrh_ref[...], preferred_element_type=F32)
              + jnp.dot(h_hi, wrl_ref[...], preferred_element_type=F32)) + br_ref[...]
    r_ref[...] = _route(logits)


def _memattn(h1, wq, k_mem, v_mem, wo, g, b, wr_hi, wr_lo, b_r, tm):
    s = h1.shape[0]
    return pl.pallas_call(
        _memattn_kernel,
        grid=(s // tm,),
        in_specs=[pl.BlockSpec((tm, D_MODEL), lambda i: (i, 0)),
                  _resident((D_MODEL, D_MODEL), lambda i: (0, 0)),
                  _resident((MEM_LEN, D_MODEL), lambda i: (0, 0)),
                  _resident((MEM_LEN, D_MODEL), lambda i: (0, 0)),
                  _resident((D_MODEL, D_MODEL), lambda i: (0, 0)),
                  pl.BlockSpec((1, D_MODEL), lambda i: (0, 0)),
                  pl.BlockSpec((1, D_MODEL), lambda i: (0, 0)),
                  _resident((D_MODEL, LANES), lambda i: (0, 0)),
                  _resident((D_MODEL, LANES), lambda i: (0, 0)),
                  pl.BlockSpec((1, LANES), lambda i: (0, 0))],
        out_specs=[pl.BlockSpec((tm, D_MODEL), lambda i: (i, 0)),
                   pl.BlockSpec((tm, LANES), lambda i: (i, 0))],
        out_shape=[jax.ShapeDtypeStruct((s, D_MODEL), F32),
                   jax.ShapeDtypeStruct((s, LANES), F32)],
        compiler_params=_cparams(("arbitrary",)),
        name="memattn_ln2_router",
    )(h1, wq, k_mem, v_mem, wo, g, b, wr_hi, wr_lo, b_r)


def _dispatch_kernel(e_ref, tok_ref, pos_ref, meta_ref):
    ca = DISPATCH_CHUNK
    n_chunks = N_ASSIGN // ca
    e_iota = lax.broadcasted_iota(I32, (N_EXPERTS, ca), 0)

    def onehot(c):
        e_c = e_ref[:, pl.ds(pl.multiple_of(c * ca, ca), ca)]
        return (e_iota == e_c).astype(F32)

    counts = lax.fori_loop(
        0, n_chunks, lambda c, acc: acc + jnp.sum(onehot(c), axis=1, keepdims=True),
        jnp.zeros((N_EXPERTS, 1), F32))
    nblk = jnp.right_shift(counts.astype(I32) + (MOE_ROWS - 1), MOE_ROWS_LOG2)
    strict_lower = (lax.broadcasted_iota(I32, (N_EXPERTS, N_EXPERTS), 1)
                    < lax.broadcasted_iota(I32, (N_EXPERTS, N_EXPERTS), 0)).astype(BF16)
    nblk_lanes = jnp.broadcast_to(nblk.astype(F32), (N_EXPERTS, LANES)).astype(BF16)
    blk_start = jnp.dot(strict_lower, nblk_lanes, preferred_element_type=F32)
    pad_start = blk_start[:, 0:1] * MOE_ROWS
    lane = lax.broadcasted_iota(I32, (N_EXPERTS, LANES), 1)
    meta_ref[...] = jnp.where(lane == 0, blk_start.astype(I32),
                              jnp.where(lane == 1, jnp.broadcast_to(nblk, (N_EXPERTS, LANES)), 0))

    earlier = (lax.broadcasted_iota(I32, (ca, ca), 0)
               < lax.broadcasted_iota(I32, (ca, ca), 1)).astype(BF16)
    blk_iota = lax.broadcasted_iota(I32, (N_ROW_BLOCKS, ca), 0)
    row_iota = lax.broadcasted_iota(I32, (MOE_ROWS, ca), 0)
    a_iota = lax.broadcasted_iota(I32, (1, ca), 1)

    def body(c, state):
        seen, acc_hi, acc_lo = state
        oh = onehot(c)
        before = jnp.dot(oh.astype(BF16), earlier, preferred_element_type=F32) + seen
        dest = jnp.sum(oh * (before + pad_start), axis=0, keepdims=True).astype(I32)
        pos_ref[:, pl.ds(pl.multiple_of(c * ca, ca), ca)] = dest
        in_blk = (blk_iota == jnp.right_shift(dest, MOE_ROWS_LOG2)).astype(F32)
        in_row = (row_iota == jnp.bitwise_and(dest, MOE_ROWS - 1)).astype(BF16)
        tok = jnp.right_shift(c * ca + a_iota, 1)
        tok_hi = jnp.right_shift(tok, 7).astype(F32)
        tok_lo = jnp.bitwise_and(tok, 127).astype(F32)
        nt = (((1,), (1,)), ((), ()))
        acc_hi = acc_hi + lax.dot_general((in_blk * tok_hi).astype(BF16), in_row, nt,
                                          preferred_element_type=F32)
        acc_lo = acc_lo + lax.dot_general((in_blk * tok_lo).astype(BF16), in_row, nt,
                                          preferred_element_type=F32)
        return seen + jnp.sum(oh, axis=1, keepdims=True), acc_hi, acc_lo

    zeros = jnp.zeros((N_ROW_BLOCKS, MOE_ROWS), F32)
    _, acc_hi, acc_lo = lax.fori_loop(0, n_chunks, body,
                                      (jnp.zeros((N_EXPERTS, 1), F32), zeros, zeros))
    tok_ref[...] = (acc_hi * 128.0 + acc_lo).astype(I32)


def _dispatch(flat_e):
    return pl.pallas_call(
        _dispatch_kernel,
        out_shape=[jax.ShapeDtypeStruct((N_ROW_BLOCKS, MOE_ROWS), I32),
                   jax.ShapeDtypeStruct((1, N_ASSIGN), I32),
                   jax.ShapeDtypeStruct((N_EXPERTS, LANES), I32)],
        compiler_params=pltpu.CompilerParams(vmem_limit_bytes=VMEM_LIMIT),
        name="moe_dispatch",
    )(flat_e)


def _expert_kernel(bs_ref, nb_ref, tot_ref, tok_ref, h_hbm, w1_hbm, w3_hbm, w2_hbm, y_hbm,
                   xbuf, ybuf, gsem, ysem, wsem, w1f, w3f, w2f, w1b, w3b, w2b):
    e = pl.program_id(0)
    n_exp = pl.num_programs(0)
    total = tot_ref[0]
    first = bs_ref[e]
    nblk = nb_ref[e]

    def issue_rows(blk, rows):
        base = jnp.minimum(blk, N_ROW_BLOCKS - 1) * MOE_ROWS
        slot = blk % GATHER_SLOTS
        for r in rows:
            tok = tok_ref[base + r]
            pltpu.make_async_copy(h_hbm.at[pl.ds(tok, 1)], xbuf.at[slot, pl.ds(r, 1)],
                                  gsem.at[slot]).start()

    def wait_gather(blk):
        slot = blk % GATHER_SLOTS
        pltpu.make_async_copy(h_hbm.at[pl.ds(0, MOE_ROWS)], xbuf.at[slot], gsem.at[slot]).wait()

    def y_copy(blk, slot):
        return pltpu.make_async_copy(ybuf.at[slot], y_hbm.at[pl.ds(blk * MOE_ROWS, MOE_ROWS)],
                                     ysem.at[slot])

    def weight_copies(ex):
        slot = ex % 2
        return [pltpu.make_async_copy(src.at[ex], dst.at[slot], wsem.at[slot])
                for src, dst in ((w1_hbm, w1f), (w3_hbm, w3f), (w2_hbm, w2f))]

    @pl.when(e == 0)
    def _():
        for cp in weight_copies(0):
            cp.start(priority=1)

        def prime(r, c):
            for blk in range(GATHER_SLOTS - 1):
                tok = tok_ref[blk * MOE_ROWS + r]
                pltpu.make_async_copy(h_hbm.at[pl.ds(tok, 1)], xbuf.at[blk, pl.ds(r, 1)],
                                      gsem.at[blk]).start()
            return c
        lax.fori_loop(0, MOE_ROWS, prime, 0, unroll=8)

    @pl.when(e + 1 < n_exp)
    def _():
        for cp in weight_copies(e + 1):
            cp.start(priority=1)

    for cp in weight_copies(e):
        cp.wait()

    @pl.when(nblk > 0)
    def _():
        wslot = e % 2
        w1b[...] = w1f[wslot].astype(BF16)
        w3b[...] = w3f[wslot].astype(BF16)
        w2b[...] = w2f[wslot].astype(BF16)

    n_piece = 4
    rows_per_piece = MOE_ROWS // n_piece
    cols_per_piece = D_MODEL // n_piece

    def block(b, c):
        blk = first + b
        slot = blk % 2
        wait_gather(blk)

        @pl.when(blk >= 2)
        def _():
            y_copy(blk - 2, slot).wait()

        xb = xbuf[blk % GATHER_SLOTS].astype(BF16)
        a = jnp.dot(xb, w1b[...], preferred_element_type=F32)
        g = jnp.dot(xb, w3b[...], preferred_element_type=F32)
        hid = (a * jax.nn.sigmoid(a) * g).astype(BF16)
        for p in range(n_piece):
            issue_rows(blk + GATHER_SLOTS - 1,
                       range(p * rows_per_piece, (p + 1) * rows_per_piece))
            cols = slice(p * cols_per_piece, (p + 1) * cols_per_piece)
            ybuf[slot, :, cols] = jnp.dot(hid, w2b[:, cols], preferred_element_type=F32)
        y_copy(blk, slot).start()
        return c

    lax.fori_loop(0, nblk, block, 0)

    @pl.when(e == n_exp - 1)
    def _():
        for ahead in range(GATHER_SLOTS - 1):
            wait_gather(total + ahead)
        for back in (1, 2):
            @pl.when(total >= back)
            def _():
                y_copy(total - back, (total - back) % 2).wait()

        ybuf[0] = jnp.zeros((MOE_ROWS, D_MODEL), F32)

        def zero_block(blk, c):
            y_copy(blk, 0).start()
            y_copy(blk, 0).wait()
            return c

        lax.fori_loop(total, N_ROW_BLOCKS, zero_block, 0)


def _experts(blk_start, nblk, total, row_tok, h2, w1, w3, w2):
    grid_spec = pltpu.PrefetchScalarGridSpec(
        num_scalar_prefetch=4,
        grid=(N_EXPERTS,),
        in_specs=[pl.BlockSpec(memory_space=pl.ANY)] * 4,
        out_specs=pl.BlockSpec(memory_space=pl.ANY),
        scratch_shapes=[pltpu.VMEM((GATHER_SLOTS, MOE_ROWS, D_MODEL), F32),
                        pltpu.VMEM((2, MOE_ROWS, D_MODEL), F32),
                        pltpu.SemaphoreType.DMA((GATHER_SLOTS,)),
                        pltpu.SemaphoreType.DMA((2,)),
                        pltpu.SemaphoreType.DMA((2,)),
                        pltpu.VMEM((2, D_MODEL, D_EXPERT), F32),
                        pltpu.VMEM((2, D_MODEL, D_EXPERT), F32),
                        pltpu.VMEM((2, D_EXPERT, D_MODEL), F32),
                        pltpu.VMEM((D_MODEL, D_EXPERT), BF16),
                        pltpu.VMEM((D_MODEL, D_EXPERT), BF16),
                        pltpu.VMEM((D_EXPERT, D_MODEL), BF16)],
    )
    return pl.pallas_call(
        _expert_kernel,
        grid_spec=grid_spec,
        out_shape=jax.ShapeDtypeStruct((N_ROWS, D_MODEL), F32),
        compiler_params=_cparams(("arbitrary",)),
        name="experts",
    )(blk_start, nblk, total, row_tok, h2, w1, w3, w2)


def _combine_kernel(pos_ref, y_hbm, h_ref, r_ref, g_ref, beta_ref, o_ref, ybuf, sem, *, tc):
    i = pl.program_id(0)
    n = pl.num_programs(0)
    slot = i % 2

    def start_gather(blk, slot_):
        def issue(r, c):
            for kk in range(TOP_K):
                p = pos_ref[(blk * tc + r) * TOP_K + kk]
                pltpu.make_async_copy(y_hbm.at[pl.ds(p, 1)], ybuf.at[slot_, kk, pl.ds(r, 1)],
                                      sem.at[slot_]).start()
            return c
        lax.fori_loop(0, tc, issue, 0, unroll=8)

    @pl.when(i == 0)
    def _():
        start_gather(0, 0)

    @pl.when(i + 1 < n)
    def _():
        start_gather(i + 1, 1 - slot)

    for kk in range(TOP_K):
        pltpu.make_async_copy(y_hbm.at[pl.ds(0, tc)], ybuf.at[slot, kk], sem.at[slot]).wait()
    gates = r_ref[...]
    moe = (ybuf[slot, 0] * gates[:, TOP_K:TOP_K + 1]
           + ybuf[slot, 1] * gates[:, TOP_K + 1:TOP_K + 2])
    o_ref[...] = _layer_norm(DN_ALPHA * h_ref[...] + moe, g_ref[...], beta_ref[...])


def _combine(pos, y_rows, h2, rout, g, b, tc):
    s = h2.shape[0]
    grid_spec = pltpu.PrefetchScalarGridSpec(
        num_scalar_prefetch=1,
        grid=(s // tc,),
        in_specs=[pl.BlockSpec(memory_space=pl.ANY),
                  pl.BlockSpec((tc, D_MODEL), lambda i, pos: (i, 0)),
                  pl.BlockSpec((tc, LANES), lambda i, pos: (i, 0)),
                  pl.BlockSpec((1, D_MODEL), lambda i, pos: (0, 0)),
                  pl.BlockSpec((1, D_MODEL), lambda i, pos: (0, 0))],
        out_specs=pl.BlockSpec((tc, D_MODEL), lambda i, pos: (i, 0)),
        scratch_shapes=[pltpu.VMEM((2, TOP_K, tc, D_MODEL), F32),
                        pltpu.SemaphoreType.DMA((2,))],
    )
    return pl.pallas_call(
        functools.partial(_combine_kernel, tc=tc),
        grid_spec=grid_spec,
        out_shape=jax.ShapeDtypeStruct((s, D_MODEL), F32),
        compiler_params=_cparams(("arbitrary",)),
        name="combine_ln3",
    )(pos, y_rows, h2, rout, g, b)


def kernel(x, mem, w_in, gm_ln_g, gm_ln_b, gm_w_s, gm_b_s, w_mix_out, ln1_g, ln1_b,
           mem_w_q, mem_w_k, mem_w_v, mem_w_o, ln2_g, ln2_b,
           w_group, b_group, w_router, b_router, w1, w3, w2, ln3_g, ln3_b):
    assert x.shape == (1, SEQ, D_MODEL) and w_in.shape[0] == 1
    xs = x[0]
    l = 0
    w_mix = w_mix_out[l].astype(BF16)
    w_q = mem_w_q[l].astype(BF16)
    w_o = mem_w_o[l].astype(BF16)
    w_r = jnp.concatenate(
        [w_router[l].transpose(1, 0, 2).reshape(D_MODEL, N_EXPERTS), w_group[l],
         jnp.zeros((D_MODEL, LANES - N_EXPERTS - N_GROUPS), F32)], axis=1)
    wr_hi, wr_lo = _split_bf16(w_r)
    b_r = jnp.concatenate([b_router[l].reshape(-1), b_group[l],
                           jnp.zeros((LANES - N_EXPERTS - N_GROUPS,), F32)]).reshape(1, LANES)
    row = lambda v: v.reshape(1, D_MODEL)

    uvk = _proj(xs, w_in[l], (0, 1, 3), tm=1024, tn=1024)
    qvt = _proj(xs, w_in[l], (2, 4), tm=1024, tn=1024, transposed=True,
                first_scale=SB_HEAD_DIM ** -0.5 * math.log2(math.e))
    mix_a = _gmlp(uvk, gm_w_s[l], gm_ln_g[l], gm_ln_b[l], gm_b_s[l].T, tg=512)
    mix_b = _stick_breaking(qvt, uvk, tq=256, bk=128)
    h1 = _mixout(mix_a, mix_b, w_mix, xs, row(ln1_g[l]), row(ln1_b[l]), tm=512)
    k_mem = _proj(mem[0], mem_w_k[l], (0, 1), tm=MEM_LEN, tn=1024)
    v_mem = _proj(mem[0], mem_w_v[l], (0, 1), tm=MEM_LEN, tn=1024)
    h2, rout = _memattn(h1, w_q, k_mem, v_mem, w_o, row(ln2_g[l]), row(ln2_b[l]), wr_hi, wr_lo, b_r, tm=256)
    flat_e = rout[:, 0:TOP_K].astype(I32).reshape(1, N_ASSIGN)
    row_tok, pos, meta = _dispatch(flat_e)
    blk_start, nblk = meta[:, 0], meta[:, 1]
    total = (blk_start[N_EXPERTS - 1] + nblk[N_EXPERTS - 1]).reshape(1)
    y_rows = _experts(blk_start, nblk, total, row_tok.reshape(N_ROWS), h2, w1[l], w3[l], w2[l])
    out = _combine(pos.reshape(N_ASSIGN), y_rows, h2, rout, row(ln3_g[l]), row(ln3_b[l]), tc=256)
    return out.reshape(1, SEQ, D_MODEL)
```

```python
import functools
import math

import jax
import jax.numpy as jnp
from jax import lax
from jax.experimental import pallas as pl
from jax.experimental.pallas import tpu as pltpu

D_MODEL = 2048
SEQ = 8192
CHUNK = 64
MEM_LEN = 256
GM_BLOCK = 128
GM_GROUPS = 8
GM_WIDTH = 1024
SB_HEADS = 8
SB_HEAD_DIM = 128
SB_WIDTH = 1024
MEM_HEADS = 4
MEM_HEAD_DIM = 512
N_GROUPS = 8
EXPERTS_PER_GROUP = 8
N_EXPERTS = 64
TOP_K = 2
D_EXPERT = 512
DN_ALPHA = 2.0 ** 0.25
LN_EPS = 1e-5

LANES = 128
ROW_TILES = D_MODEL // LANES
MOE_ROWS = 128
MOE_ROWS_LOG2 = 7
N_ASSIGN = SEQ * TOP_K
N_ROW_BLOCKS = N_ASSIGN // MOE_ROWS + N_EXPERTS
N_ROWS = N_ROW_BLOCKS * MOE_ROWS
DISPATCH_CHUNK = 256
GATHER_SLOTS = 3
SB_UNDERFLOW = 160.0
VMEM_LIMIT = 56 * 1024 * 1024

BF16 = jnp.bfloat16
F32 = jnp.float32
I32 = jnp.int32


def _cparams(sem):
    return pltpu.CompilerParams(dimension_semantics=sem, vmem_limit_bytes=VMEM_LIMIT)


def _resident(shape, index_map):
    return pl.BlockSpec(shape, index_map, pipeline_mode=pl.Buffered(1))


def _layer_norm(y, g, b):
    mu = jnp.mean(y, axis=-1, keepdims=True)
    yc = y - mu
    var = jnp.mean(yc * yc, axis=-1, keepdims=True)
    return yc * lax.rsqrt(var + LN_EPS) * g + b


def _split_bf16(v):
    hi = v.astype(BF16)
    lo = (v - hi.astype(F32)).astype(BF16)
    return hi, lo


def _proj_kernel(x_ref, w_ref, o_ref, xb_ref, *, transposed, first_scale):
    j = pl.program_id(1)

    @pl.when(j == 0)
    def _():
        xb_ref[...] = x_ref[...].astype(BF16)

    wb = w_ref[...].astype(BF16)
    if transposed:
        acc = lax.dot_general(wb, xb_ref[...], (((0,), (1,)), ((), ())), preferred_element_type=F32)
    else:
        acc = jnp.dot(xb_ref[...], wb, preferred_element_type=F32)
    if first_scale is not None:
        acc = acc * jnp.where(j == 0, first_scale, 1.0).astype(F32)
    o_ref[...] = acc.astype(o_ref.dtype)


def _proj(x, w, col_blocks, tm, tn, transposed=False, first_scale=None):
    m, k = x.shape
    nblk = len(col_blocks)
    wmap = lambda i, j: (0, sum(jnp.where(j == t, c, 0) for t, c in enumerate(col_blocks)))
    out_spec = (pl.BlockSpec((tn, tm), lambda i, j: (j, i)) if transposed
                else pl.BlockSpec((tm, tn), lambda i, j: (i, j)))
    out_shape = (nblk * tn, m) if transposed else (m, nblk * tn)
    return pl.pallas_call(
        functools.partial(_proj_kernel, transposed=transposed, first_scale=first_scale),
        grid=(m // tm, nblk),
        in_specs=[pl.BlockSpec((tm, k), lambda i, j: (i, 0)),
                  pl.BlockSpec((k, tn), wmap)],
        out_specs=out_spec,
        out_shape=jax.ShapeDtypeStruct(out_shape, BF16),
        scratch_shapes=[pltpu.VMEM((tm, k), BF16)],
        compiler_params=_cparams(("arbitrary", "arbitrary")),
        name="proj_t" if transposed else "proj",
    )(x, w)


def _gmlp_kernel(u_ref, v_ref, ws_ref, g_ref, b_ref, bs_ref, o_ref, *, tg):
    t_chunk = lax.broadcasted_iota(I32, (GM_BLOCK, GM_BLOCK), 0) // CHUNK
    s_chunk = lax.broadcasted_iota(I32, (GM_BLOCK, GM_BLOCK), 1) // CHUNK
    causal = t_chunk >= s_chunk
    for g in range(GM_GROUPS):
        wm = jnp.where(causal, ws_ref[g], 0.0).astype(BF16)
        ln_g = g_ref[g:g + 1, :]
        ln_b = b_ref[g:g + 1, :]
        bias = bs_ref[:, g:g + 1]
        cols = slice(g * LANES, (g + 1) * LANES)
        for n in range(tg // GM_BLOCK):
            rows = slice(n * GM_BLOCK, (n + 1) * GM_BLOCK)
            v = jax.nn.gelu(v_ref[rows, cols].astype(F32))
            vg = _layer_norm(v, ln_g, ln_b).astype(BF16)
            mixed = jnp.dot(wm, vg, preferred_element_type=F32) + bias
            u = jax.nn.gelu(u_ref[rows, cols].astype(F32))
            o_ref[rows, cols] = (u * mixed).astype(o_ref.dtype)


def _gmlp(uvk, gm_w_s, gm_ln_g, gm_ln_b, gm_b_s_t, tg):
    s = uvk.shape[0]
    return pl.pallas_call(
        functools.partial(_gmlp_kernel, tg=tg),
        grid=(s // tg,),
        in_specs=[pl.BlockSpec((tg, GM_WIDTH), lambda i: (i, 0)),
                  pl.BlockSpec((tg, GM_WIDTH), lambda i: (i, 1)),
                  pl.BlockSpec((GM_GROUPS, GM_BLOCK, GM_BLOCK), lambda i: (0, 0, 0)),
                  pl.BlockSpec((GM_GROUPS, LANES), lambda i: (0, 0)),
                  pl.BlockSpec((GM_GROUPS, LANES), lambda i: (0, 0)),
                  pl.BlockSpec((GM_BLOCK, GM_GROUPS), lambda i: (0, 0))],
        out_specs=pl.BlockSpec((tg, GM_WIDTH), lambda i: (i, 0)),
        out_shape=jax.ShapeDtypeStruct((s, GM_WIDTH), BF16),
        compiler_params=_cparams(("arbitrary",)),
        name="gmlp",
    )(uvk, uvk, gm_w_s, gm_ln_g, gm_ln_b, gm_b_s_t)


def _softplus2(u):
    return jnp.maximum(u, 0.0) + jnp.log2(1.0 + jnp.exp2(-jnp.abs(u)))


def _sb_kernel(qt_ref, k_ref, vt_ref, o_ref, *acc_refs, tq, bk):
    q0 = pl.program_id(0) * tq
    upper = (lax.broadcasted_iota(I32, (bk, bk), 1)
             >= lax.broadcasted_iota(I32, (bk, bk), 0)).astype(BF16)
    upper2 = jnp.concatenate([upper, upper], axis=1)

    heads = range(SB_HEADS)
    hrows = [slice(h * SB_HEAD_DIM, (h + 1) * SB_HEAD_DIM) for h in heads]

    def tiles(ks, carries, mask):
        zs = [jnp.dot(k_ref[pl.ds(ks, bk), hrows[h]], qt_ref[hrows[h], :],
                      preferred_element_type=F32) for h in heads]
        sps = [_softplus2(z) for z in zs]
        if mask is not None:
            sps = [jnp.where(mask, sp, 0.0) for sp in sps]
        hls = [jnp.concatenate(_split_bf16(sp), axis=0) for sp in sps]
        csums = [jnp.dot(upper2, hls[h], preferred_element_type=F32) + carries[h] for h in heads]
        probs = [jnp.exp2(zs[h] - csums[h]) for h in heads]
        if mask is not None:
            probs = [jnp.where(mask, a, 0.0) for a in probs]
        for h in heads:
            acc_refs[h][...] += jnp.dot(vt_ref[hrows[h], pl.ds(ks, bk)], probs[h].astype(BF16),
                                        preferred_element_type=F32)
        return [csum[0:1, :] for csum in csums]

    for acc_ref in acc_refs:
        acc_ref[...] = jnp.zeros_like(acc_ref)
    carries = [jnp.zeros((1, tq), F32) for _ in range(SB_HEADS)]
    key_in_tile = lax.broadcasted_iota(I32, (bk, tq), 0)
    q_in_tile = lax.broadcasted_iota(I32, (bk, tq), 1)
    for d in reversed(range(tq // bk)):
        mask = (key_in_tile + d * bk) < q_in_tile
        ks = pl.multiple_of(q0 + d * bk, bk)
        carries = tiles(ks, carries, mask)

    def cond(state):
        j = state[0]
        lowest = functools.reduce(jnp.minimum, state[1:])
        return jnp.logical_and(j >= 0, jnp.min(lowest) < SB_UNDERFLOW)

    def body(state):
        j = state[0]
        ks = pl.multiple_of(j * bk, bk)
        return (j - 1,) + tuple(tiles(ks, state[1:], None))

    lax.while_loop(cond, body, (q0 // bk - 1,) + tuple(carries))
    for h in range(SB_HEADS):
        o_ref[:, h * SB_HEAD_DIM:(h + 1) * SB_HEAD_DIM] = acc_refs[h][...].T.astype(o_ref.dtype)


def _stick_breaking(qvt, uvk, tq, bk):
    s = uvk.shape[0]
    return pl.pallas_call(
        functools.partial(_sb_kernel, tq=tq, bk=bk),
        grid=(s // tq,),
        in_specs=[pl.BlockSpec((SB_WIDTH, tq), lambda i: (0, i)),
                  _resident((s, SB_WIDTH), lambda i: (0, 2 * GM_WIDTH // SB_WIDTH)),
                  _resident((SB_WIDTH, s), lambda i: (1, 0))],
        out_specs=pl.BlockSpec((tq, SB_WIDTH), lambda i: (i, 0)),
        out_shape=jax.ShapeDtypeStruct((s, SB_WIDTH), BF16),
        scratch_shapes=[pltpu.VMEM((SB_HEAD_DIM, tq), F32) for _ in range(SB_HEADS)],
        compiler_params=_cparams(("arbitrary",)),
        name="stick_breaking",
    )(qvt, uvk, qvt)


def _mixout_kernel(a_ref, b_ref, wa_ref, wb_ref, x_ref, g_ref, beta_ref, o_ref):
    mixed = (jnp.dot(a_ref[...], wa_ref[...], preferred_element_type=F32)
             + jnp.dot(b_ref[...], wb_ref[...], preferred_element_type=F32))
    y = DN_ALPHA * x_ref[...] + mixed
    o_ref[...] = _layer_norm(y, g_ref[...], beta_ref[...])


def _mixout(mix_a, mix_b, w, x, g, b, tm):
    s = x.shape[0]
    return pl.pallas_call(
        _mixout_kernel,
        grid=(s // tm,),
        in_specs=[pl.BlockSpec((tm, GM_WIDTH), lambda i: (i, 0)),
                  pl.BlockSpec((tm, SB_WIDTH), lambda i: (i, 0)),
                  pl.BlockSpec((GM_WIDTH, D_MODEL), lambda i: (0, 0)),
                  pl.BlockSpec((SB_WIDTH, D_MODEL), lambda i: (1, 0)),
                  pl.BlockSpec((tm, D_MODEL), lambda i: (i, 0)),
                  pl.BlockSpec((1, D_MODEL), lambda i: (0, 0)),
                  pl.BlockSpec((1, D_MODEL), lambda i: (0, 0))],
        out_specs=pl.BlockSpec((tm, D_MODEL), lambda i: (i, 0)),
        out_shape=jax.ShapeDtypeStruct((s, D_MODEL), F32),
        compiler_params=_cparams(("arbitrary",)),
        name="mixout_ln1",
    )(mix_a, mix_b, w, w, x, g, b)


def _route(logits):
    lane = lax.broadcasted_iota(I32, logits.shape, 1)
    neg = jnp.float32(-jnp.inf)
    big = jnp.int32(1 << 20)

    def first_argmax(vals):
        m = jnp.max(vals, axis=-1, keepdims=True)
        idx = jnp.min(jnp.where(vals == m, lane, big), axis=-1, keepdims=True)
        return m, idx

    is_group = jnp.logical_and(lane >= N_EXPERTS, lane < N_EXPERTS + N_GROUPS)
    gl = jnp.where(is_group, logits, neg)
    gmax, glane = first_argmax(gl)
    g_val = 1.0 / jnp.sum(jnp.exp(gl - gmax), axis=-1, keepdims=True)
    gidx = glane - N_EXPERTS
    in_group = jnp.logical_and(lane >= gidx * EXPERTS_PER_GROUP,
                               lane < (gidx + 1) * EXPERTS_PER_GROUP)
    el = jnp.where(in_group, logits, neg)
    m1, i1 = first_argmax(el)
    m2, i2 = first_argmax(jnp.where(lane == i1, neg, el))
    e2 = jnp.exp(m2 - m1)
    gate1 = g_val / (1.0 + e2)
    gate2 = g_val * e2 / (1.0 + e2)
    return jnp.where(lane == 0, i1.astype(F32),
                     jnp.where(lane == 1, i2.astype(F32),
                               jnp.where(lane == 2, gate1,
                                         jnp.where(lane == 3, gate2, 0.0))))


def _memattn_kernel(h_ref, wq_ref, km_ref, vm_ref, wo_ref, g_ref, beta_ref, wrh_ref, wrl_ref, br_ref,
                    o_ref, r_ref, orow_ref):
    h = h_ref[...]
    q = jnp.dot(h.astype(BF16), wq_ref[...], preferred_element_type=F32).astype(BF16)
    outs = []
    for hd in range(MEM_HEADS):
        cols = slice(hd * MEM_HEAD_DIM, (hd + 1) * MEM_HEAD_DIM)
        kh = km_ref[:, cols]
        vh = vm_ref[:, cols]
        s = lax.dot_general(q[:, cols], kh, (((1,), (1,)), ((), ())),
                            preferred_element_type=F32) * (MEM_HEAD_DIM ** -0.5)
        s = s - jnp.max(s, axis=-1, keepdims=True)
        e = jnp.exp(s)
        p = e / jnp.sum(e, axis=-1, keepdims=True)
        outs.append(jnp.dot(p.astype(BF16), vh, preferred_element_type=F32).astype(BF16))
    o = jnp.concatenate(outs, axis=-1)
    attn = jnp.dot(o, wo_ref[...], preferred_element_type=F32)
    h2 = _layer_norm(DN_ALPHA * h + attn, g_ref[...], beta_ref[...])
    o_ref[...] = h2
    for c in range(ROW_TILES):
        orow_ref[:, c, :] = h2[:, c * LANES:(c + 1) * LANES]
    h_hi, h_lo = _split_bf16(h2)
    logits = (jnp.dot(h_hi, wrh_ref[...], preferred_element_type=F32)
              + jnp.dot(h_lo, wrh_ref[...], preferred_element_type=F32)
              + jnp.dot(h_hi, wrl_ref[...], preferred_element_type=F32)) + br_ref[...]
    r_ref[...] = _route(logits)


def _memattn(h1, wq, k_mem, v_mem, wo, g, b, wr_hi, wr_lo, b_r, tm):
    s = h1.shape[0]
    return pl.pallas_call(
        _memattn_kernel,
        grid=(s // tm,),
        in_specs=[pl.BlockSpec((tm, D_MODEL), lambda i: (i, 0)),
                  _resident((D_MODEL, D_MODEL), lambda i: (0, 0)),
                  _resident((MEM_LEN, D_MODEL), lambda i: (0, 0)),
                  _resident((MEM_LEN, D_MODEL), lambda i: (0, 0)),
                  _resident((D_MODEL, D_MODEL), lambda i: (0, 0)),
                  pl.BlockSpec((1, D_MODEL), lambda i: (0, 0)),
                  pl.BlockSpec((1, D_MODEL), lambda i: (0, 0)),
                  _resident((D_MODEL, LANES), lambda i: (0, 0)),
                  _resident((D_MODEL, LANES), lambda i: (0, 0)),
                  pl.BlockSpec((1, LANES), lambda i: (0, 0))],
        out_specs=[pl.BlockSpec((tm, D_MODEL), lambda i: (i, 0)),
                   pl.BlockSpec((tm, LANES), lambda i: (i, 0)),
                   pl.BlockSpec((tm, ROW_TILES, LANES), lambda i: (i, 0, 0))],
        out_shape=[jax.ShapeDtypeStruct((s, D_MODEL), F32),
                   jax.ShapeDtypeStruct((s, LANES), F32),
                   jax.ShapeDtypeStruct((s, ROW_TILES, LANES), F32)],
        compiler_params=_cparams(("arbitrary",)),
        name="memattn_ln2_router",
    )(h1, wq, k_mem, v_mem, wo, g, b, wr_hi, wr_lo, b_r)


def _dispatch_kernel(e_ref, tok_ref, pos_ref, meta_ref):
    ca = DISPATCH_CHUNK
    n_chunks = N_ASSIGN // ca
    e_iota = lax.broadcasted_iota(I32, (N_EXPERTS, ca), 0)

    def onehot(c):
        e_c = e_ref[:, pl.ds(pl.multiple_of(c * ca, ca), ca)]
        return (e_iota == e_c).astype(F32)

    counts = lax.fori_loop(
        0, n_chunks, lambda c, acc: acc + jnp.sum(onehot(c), axis=1, keepdims=True),
        jnp.zeros((N_EXPERTS, 1), F32))
    nblk = jnp.right_shift(counts.astype(I32) + (MOE_ROWS - 1), MOE_ROWS_LOG2)
    strict_lower = (lax.broadcasted_iota(I32, (N_EXPERTS, N_EXPERTS), 1)
                    < lax.broadcasted_iota(I32, (N_EXPERTS, N_EXPERTS), 0)).astype(BF16)
    nblk_lanes = jnp.broadcast_to(nblk.astype(F32), (N_EXPERTS, LANES)).astype(BF16)
    blk_start = jnp.dot(strict_lower, nblk_lanes, preferred_element_type=F32)
    pad_start = blk_start[:, 0:1] * MOE_ROWS
    lane = lax.broadcasted_iota(I32, (N_EXPERTS, LANES), 1)
    meta_ref[...] = jnp.where(lane == 0, blk_start.astype(I32),
                              jnp.where(lane == 1, jnp.broadcast_to(nblk, (N_EXPERTS, LANES)), 0))

    earlier = (lax.broadcasted_iota(I32, (ca, ca), 0)
               < lax.broadcasted_iota(I32, (ca, ca), 1)).astype(BF16)
    blk_iota = lax.broadcasted_iota(I32, (N_ROW_BLOCKS, ca), 0)
    row_iota = lax.broadcasted_iota(I32, (MOE_ROWS, ca), 0)
    a_iota = lax.broadcasted_iota(I32, (1, ca), 1)

    def body(c, state):
        seen, acc_hi, acc_lo = state
        oh = onehot(c)
        before = jnp.dot(oh.astype(BF16), earlier, preferred_element_type=F32) + seen
        dest = jnp.sum(oh * (before + pad_start), axis=0, keepdims=True).astype(I32)
        pos_ref[:, pl.ds(pl.multiple_of(c * ca, ca), ca)] = dest
        in_blk = (blk_iota == jnp.right_shift(dest, MOE_ROWS_LOG2)).astype(F32)
        in_row = (row_iota == jnp.bitwise_and(dest, MOE_ROWS - 1)).astype(BF16)
        tok = jnp.right_shift(c * ca + a_iota, 1)
        tok_hi = jnp.right_shift(tok, 7).astype(F32)
        tok_lo = jnp.bitwise_and(tok, 127).astype(F32)
        nt = (((1,), (1,)), ((), ()))
        acc_hi = acc_hi + lax.dot_general((in_blk * tok_hi).astype(BF16), in_row, nt,
                                          preferred_element_type=F32)
        acc_lo = acc_lo + lax.dot_general((in_blk * tok_lo).astype(BF16), in_row, nt,
                                          preferred_element_type=F32)
        return seen + jnp.sum(oh, axis=1, keepdims=True), acc_hi, acc_lo

    zeros = jnp.zeros((N_ROW_BLOCKS, MOE_ROWS), F32)
    _, acc_hi, acc_lo = lax.fori_loop(0, n_chunks, body,
                                      (jnp.zeros((N_EXPERTS, 1), F32), zeros, zeros))
    tok_ref[...] = (acc_hi * 128.0 + acc_lo).astype(I32)


def _dispatch(flat_e):
    return pl.pallas_call(
        _dispatch_kernel,
        out_shape=[jax.ShapeDtypeStruct((N_ROW_BLOCKS, MOE_ROWS), I32),
                   jax.ShapeDtypeStruct((1, N_ASSIGN), I32),
                   jax.ShapeDtypeStruct((N_EXPERTS, LANES), I32)],
        compiler_params=pltpu.CompilerParams(vmem_limit_bytes=VMEM_LIMIT),
        name="moe_dispatch",
    )(flat_e)


def _expert_kernel(bs_ref, nb_ref, tot_ref, tok_ref, h_hbm, hflat_hbm, w1_hbm, w3_hbm, w2_hbm, y_hbm,
                   xbuf, ybuf, gsem, ysem, wsem, w1f, w3f, w2f, w1b, w3b, w2b):
    e = pl.program_id(0)
    n_exp = pl.num_programs(0)
    total = tot_ref[0]
    first = bs_ref[e]
    nblk = nb_ref[e]

    def issue_rows(blk, rows):
        base = jnp.minimum(blk, N_ROW_BLOCKS - 1) * MOE_ROWS
        slot = blk % GATHER_SLOTS
        for r in rows:
            tok = tok_ref[base + r]
            pltpu.make_async_copy(h_hbm.at[tok], xbuf.at[slot, pl.ds(r * ROW_TILES, ROW_TILES)],
                                  gsem.at[slot]).start()

    def wait_gather(blk):
        slot = blk % GATHER_SLOTS
        pltpu.make_async_copy(hflat_hbm.at[pl.ds(0, MOE_ROWS * ROW_TILES)], xbuf.at[slot],
                              gsem.at[slot]).wait()

    def y_copy(blk, slot):
        return pltpu.make_async_copy(ybuf.at[slot], y_hbm.at[pl.ds(blk * MOE_ROWS, MOE_ROWS)],
                                     ysem.at[slot])

    def weight_copies(ex):
        slot = ex % 2
        return [pltpu.make_async_copy(src.at[ex], dst.at[slot], wsem.at[slot])
                for src, dst in ((w1_hbm, w1f), (w3_hbm, w3f), (w2_hbm, w2f))]

    @pl.when(e == 0)
    def _():
        for cp in weight_copies(0):
            cp.start(priority=1)

        def prime(r, c):
            for blk in range(GATHER_SLOTS - 1):
                tok = tok_ref[blk * MOE_ROWS + r]
                pltpu.make_async_copy(h_hbm.at[tok],
                                      xbuf.at[blk, pl.ds(r * ROW_TILES, ROW_TILES)],
                                      gsem.at[blk]).start()
            return c
        lax.fori_loop(0, MOE_ROWS, prime, 0, unroll=8)

    @pl.when(e + 1 < n_exp)
    def _():
        for cp in weight_copies(e + 1):
            cp.start(priority=1)

    for cp in weight_copies(e):
        cp.wait()

    @pl.when(nblk > 0)
    def _():
        wslot = e % 2
        w1b[...] = w1f[wslot].astype(BF16)
        w3b[...] = w3f[wslot].astype(BF16)
        w2b[...] = w2f[wslot].astype(BF16)

    n_piece = 4
    rows_per_piece = MOE_ROWS // n_piece
    cols_per_piece = D_MODEL // n_piece

    def block(b, c):
        blk = first + b
        slot = blk % 2
        wait_gather(blk)

        @pl.when(blk >= 2)
        def _():
            y_copy(blk - 2, slot).wait()

        xs = xbuf.at[blk % GATHER_SLOTS]
        xb = jnp.concatenate([xs[pl.ds(c, MOE_ROWS, stride=ROW_TILES), :]
                              for c in range(ROW_TILES)], axis=1).astype(BF16)
        a = jnp.dot(xb, w1b[...], preferred_element_type=F32)
        g = jnp.dot(xb, w3b[...], preferred_element_type=F32)
        hid = (a * jax.nn.sigmoid(a) * g).astype(BF16)
        for p in range(n_piece):
            issue_rows(blk + GATHER_SLOTS - 1,
                       range(p * rows_per_piece, (p + 1) * rows_per_piece))
            cols = slice(p * cols_per_piece, (p + 1) * cols_per_piece)
            ybuf[slot, :, cols] = jnp.dot(hid, w2b[:, cols], preferred_element_type=F32)
        y_copy(blk, slot).start()
        return c

    lax.fori_loop(0, nblk, block, 0)

    @pl.when(e == n_exp - 1)
    def _():
        for ahead in range(GATHER_SLOTS - 1):
            wait_gather(total + ahead)
        for back in (1, 2):
            @pl.when(total >= back)
            def _():
                y_copy(total - back, (total - back) % 2).wait()

        ybuf[0] = jnp.zeros((MOE_ROWS, D_MODEL), F32)

        def zero_block(blk, c):
            y_copy(blk, 0).start()
            y_copy(blk, 0).wait()
            return c

        lax.fori_loop(total, N_ROW_BLOCKS, zero_block, 0)


def _experts(blk_start, nblk, total, row_tok, h2_rows, w1, w3, w2):
    grid_spec = pltpu.PrefetchScalarGridSpec(
        num_scalar_prefetch=4,
        grid=(N_EXPERTS,),
        in_specs=[pl.BlockSpec(memory_space=pl.ANY)] * 5,
        out_specs=pl.BlockSpec(memory_space=pl.ANY),
        scratch_shapes=[pltpu.VMEM((GATHER_SLOTS, MOE_ROWS * ROW_TILES, LANES), F32),
                        pltpu.VMEM((2, MOE_ROWS, D_MODEL), F32),
                        pltpu.SemaphoreType.DMA((GATHER_SLOTS,)),
                        pltpu.SemaphoreType.DMA((2,)),
                        pltpu.SemaphoreType.DMA((2,)),
                        pltpu.VMEM((2, D_MODEL, D_EXPERT), F32),
                        pltpu.VMEM((2, D_MODEL, D_EXPERT), F32),
                        pltpu.VMEM((2, D_EXPERT, D_MODEL), F32),
                        pltpu.VMEM((D_MODEL, D_EXPERT), BF16),
                        pltpu.VMEM((D_MODEL, D_EXPERT), BF16),
                        pltpu.VMEM((D_EXPERT, D_MODEL), BF16)],
    )
    return pl.pallas_call(
        _expert_kernel,
        grid_spec=grid_spec,
        out_shape=jax.ShapeDtypeStruct((N_ROWS, D_MODEL), F32),
        compiler_params=_cparams(("arbitrary",)),
        name="experts",
    )(blk_start, nblk, total, row_tok, h2_rows, h2_rows.reshape(SEQ * ROW_TILES, LANES), w1, w3, w2)


def _combine_kernel(pos_ref, y_hbm, h_ref, r_ref, g_ref, beta_ref, o_ref, ybuf, sem, *, tc):
    i = pl.program_id(0)
    n = pl.num_programs(0)
    slot = i % 2

    def start_gather(blk, slot_):
        def issue(r, c):
            for kk in range(TOP_K):
                p = pos_ref[(blk * tc + r) * TOP_K + kk]
                pltpu.make_async_copy(y_hbm.at[pl.ds(p, 1)], ybuf.at[slot_, kk, pl.ds(r, 1)],
                                      sem.at[slot_]).start()
            return c
        lax.fori_loop(0, tc, issue, 0, unroll=8)

    @pl.when(i == 0)
    def _():
        start_gather(0, 0)

    @pl.when(i + 1 < n)
    def _():
        start_gather(i + 1, 1 - slot)

    for kk in range(TOP_K):
        pltpu.make_async_copy(y_hbm.at[pl.ds(0, tc)], ybuf.at[slot, kk], sem.at[slot]).wait()
    gates = r_ref[...]
    moe = (ybuf[slot, 0] * gates[:, TOP_K:TOP_K + 1]
           + ybuf[slot, 1] * gates[:, TOP_K + 1:TOP_K + 2])
    o_ref[...] = _layer_norm(DN_ALPHA * h_ref[...] + moe, g_ref[...], beta_ref[...])


def _combine(pos, y_rows, h2, rout, g, b, tc):
    s = h2.shape[0]
    grid_spec = pltpu.PrefetchScalarGridSpec(
        num_scalar_prefetch=1,
        grid=(s // tc,),
        in_specs=[pl.BlockSpec(memory_space=pl.ANY),
                  pl.BlockSpec((tc, D_MODEL), lambda i, pos: (i, 0)),
                  pl.BlockSpec((tc, LANES), lambda i, pos: (i, 0)),
                  pl.BlockSpec((1, D_MODEL), lambda i, pos: (0, 0)),
                  pl.BlockSpec((1, D_MODEL), lambda i, pos: (0, 0))],
        out_specs=pl.BlockSpec((tc, D_MODEL), lambda i, pos: (i, 0)),
        scratch_shapes=[pltpu.VMEM((2, TOP_K, tc, D_MODEL), F32),
                        pltpu.SemaphoreType.DMA((2,))],
    )
    return pl.pallas_call(
        functools.partial(_combine_kernel, tc=tc),
        grid_spec=grid_spec,
        out_shape=jax.ShapeDtypeStruct((s, D_MODEL), F32),
        compiler_params=_cparams(("arbitrary",)),
        name="combine_ln3",
    )(pos, y_rows, h2, rout, g, b)


def kernel(x, mem, w_in, gm_ln_g, gm_ln_b, gm_w_s, gm_b_s, w_mix_out, ln1_g, ln1_b,
           mem_w_q, mem_w_k, mem_w_v, mem_w_o, ln2_g, ln2_b,
           w_group, b_group, w_router, b_router, w1, w3, w2, ln3_g, ln3_b):
    assert x.shape == (1, SEQ, D_MODEL) and w_in.shape[0] == 1
    xs = x[0]
    l = 0
    w_mix = w_mix_out[l].astype(BF16)
    w_q = mem_w_q[l].astype(BF16)
    w_o = mem_w_o[l].astype(BF16)
    w_r = jnp.concatenate(
        [w_router[l].transpose(1, 0, 2).reshape(D_MODEL, N_EXPERTS), w_group[l],
         jnp.zeros((D_MODEL, LANES - N_EXPERTS - N_GROUPS), F32)], axis=1)
    wr_hi, wr_lo = _split_bf16(w_r)
    b_r = jnp.concatenate([b_router[l].reshape(-1), b_group[l],
                           jnp.zeros((LANES - N_EXPERTS - N_GROUPS,), F32)]).reshape(1, LANES)
    row = lambda v: v.reshape(1, D_MODEL)

    uvk = _proj(xs, w_in[l], (0, 1, 3), tm=1024, tn=1024)
    qvt = _proj(xs, w_in[l], (2, 4), tm=1024, tn=1024, transposed=True,
                first_scale=SB_HEAD_DIM ** -0.5 * math.log2(math.e))
    mix_a = _gmlp(uvk, gm_w_s[l], gm_ln_g[l], gm_ln_b[l], gm_b_s[l].T, tg=512)
    mix_b = _stick_breaking(qvt, uvk, tq=256, bk=128)
    h1 = _mixout(mix_a, mix_b, w_mix, xs, row(ln1_g[l]), row(ln1_b[l]), tm=512)
    k_mem = _proj(mem[0], mem_w_k[l], (0, 1), tm=MEM_LEN, tn=1024)
    v_mem = _proj(mem[0], mem_w_v[l], (0, 1), tm=MEM_LEN, tn=1024)
    h2, rout, h2_rows = _memattn(h1, w_q, k_mem, v_mem, w_o, row(ln2_g[l]), row(ln2_b[l]), wr_hi, wr_lo, b_r, tm=256)
    flat_e = rout[:, 0:TOP_K].astype(I32).reshape(1, N_ASSIGN)
    row_tok, pos, meta = _dispatch(flat_e)
    blk_start, nblk = meta[:, 0], meta[:, 1]
    total = (blk_start[N_EXPERTS - 1] + nblk[N_EXPERTS - 1]).reshape(1)
    y_rows = _experts(blk_start, nblk, total, row_tok.reshape(N_ROWS), h2_rows, w1[l], w3[l], w2[l])
    out = _combine(pos.reshape(N_ASSIGN), y_rows, h2, rout, row(ln3_g[l]), row(ln3_b[l]), tc=256)
    return out.reshape(1, SEQ, D_MODEL)
```

```python
import functools
import math

import jax
import jax.numpy as jnp
from jax import lax
from jax.experimental import pallas as pl
from jax.experimental.pallas import tpu as pltpu

D_MODEL = 2048
SEQ = 8192
CHUNK = 64
MEM_LEN = 256
GM_BLOCK = 128
GM_GROUPS = 8
GM_WIDTH = 1024
SB_HEADS = 8
SB_HEAD_DIM = 128
SB_WIDTH = 1024
MEM_HEADS = 4
MEM_HEAD_DIM = 512
N_GROUPS = 8
EXPERTS_PER_GROUP = 8
N_EXPERTS = 64
TOP_K = 2
D_EXPERT = 512
DN_ALPHA = 2.0 ** 0.25
LN_EPS = 1e-5

LANES = 128
MOE_ROWS = 128
MOE_ROWS_LOG2 = 7
N_ASSIGN = SEQ * TOP_K
N_ROW_BLOCKS = N_ASSIGN // MOE_ROWS + N_EXPERTS
N_ROWS = N_ROW_BLOCKS * MOE_ROWS
DISPATCH_CHUNK = 256
GATHER_SLOTS = 3
SB_UNDERFLOW = 160.0
VMEM_LIMIT = 56 * 1024 * 1024

BF16 = jnp.bfloat16
F32 = jnp.float32
I32 = jnp.int32


def _cparams(sem):
    return pltpu.CompilerParams(dimension_semantics=sem, vmem_limit_bytes=VMEM_LIMIT)


def _resident(shape, index_map):
    return pl.BlockSpec(shape, index_map, pipeline_mode=pl.Buffered(1))


def _layer_norm(y, g, b):
    mu = jnp.mean(y, axis=-1, keepdims=True)
    yc = y - mu
    var = jnp.mean(yc * yc, axis=-1, keepdims=True)
    return yc * lax.rsqrt(var + LN_EPS) * g + b


def _split_bf16(v):
    hi = v.astype(BF16)
    lo = (v - hi.astype(F32)).astype(BF16)
    return hi, lo


def _proj_kernel(x_ref, w_ref, o_ref, xb_ref, *, transposed, first_scale):
    j = pl.program_id(1)

    @pl.when(j == 0)
    def _():
        xb_ref[...] = x_ref[...].astype(BF16)

    wb = w_ref[...].astype(BF16)
    if transposed:
        acc = lax.dot_general(wb, xb_ref[...], (((0,), (1,)), ((), ())), preferred_element_type=F32)
    else:
        acc = jnp.dot(xb_ref[...], wb, preferred_element_type=F32)
    if first_scale is not None:
        acc = acc * jnp.where(j == 0, first_scale, 1.0).astype(F32)
    o_ref[...] = acc.astype(o_ref.dtype)


def _proj(x, w, col_blocks, tm, tn, transposed=False, first_scale=None):
    m, k = x.shape
    nblk = len(col_blocks)
    wmap = lambda i, j: (0, sum(jnp.where(j == t, c, 0) for t, c in enumerate(col_blocks)))
    out_spec = (pl.BlockSpec((tn, tm), lambda i, j: (j, i)) if transposed
                else pl.BlockSpec((tm, tn), lambda i, j: (i, j)))
    out_shape = (nblk * tn, m) if transposed else (m, nblk * tn)
    return pl.pallas_call(
        functools.partial(_proj_kernel, transposed=transposed, first_scale=first_scale),
        grid=(m // tm, nblk),
        in_specs=[pl.BlockSpec((tm, k), lambda i, j: (i, 0)),
                  pl.BlockSpec((k, tn), wmap)],
        out_specs=out_spec,
        out_shape=jax.ShapeDtypeStruct(out_shape, BF16),
        scratch_shapes=[pltpu.VMEM((tm, k), BF16)],
        compiler_params=_cparams(("arbitrary", "arbitrary")),
        name="proj_t" if transposed else "proj",
    )(x, w)


def _gmlp_kernel(u_ref, v_ref, ws_ref, g_ref, b_ref, bs_ref, o_ref, *, tg):
    t_chunk = lax.broadcasted_iota(I32, (GM_BLOCK, GM_BLOCK), 0) // CHUNK
    s_chunk = lax.broadcasted_iota(I32, (GM_BLOCK, GM_BLOCK), 1) // CHUNK
    causal = t_chunk >= s_chunk
    for g in range(GM_GROUPS):
        wm = jnp.where(causal, ws_ref[g], 0.0).astype(BF16)
        ln_g = g_ref[g:g + 1, :]
        ln_b = b_ref[g:g + 1, :]
        bias = bs_ref[:, g:g + 1]
        cols = slice(g * LANES, (g + 1) * LANES)
        for n in range(tg // GM_BLOCK):
            rows = slice(n * GM_BLOCK, (n + 1) * GM_BLOCK)
            v = jax.nn.gelu(v_ref[rows, cols].astype(F32))
            vg = _layer_norm(v, ln_g, ln_b).astype(BF16)
            mixed = jnp.dot(wm, vg, preferred_element_type=F32) + bias
            u = jax.nn.gelu(u_ref[rows, cols].astype(F32))
            o_ref[rows, cols] = (u * mixed).astype(o_ref.dtype)


def _gmlp(uvk, gm_w_s, gm_ln_g, gm_ln_b, gm_b_s_t, tg):
    s = uvk.shape[0]
    return pl.pallas_call(
        functools.partial(_gmlp_kernel, tg=tg),
        grid=(s // tg,),
        in_specs=[pl.BlockSpec((tg, GM_WIDTH), lambda i: (i, 0)),
                  pl.BlockSpec((tg, GM_WIDTH), lambda i: (i, 1)),
                  pl.BlockSpec((GM_GROUPS, GM_BLOCK, GM_BLOCK), lambda i: (0, 0, 0)),
                  pl.BlockSpec((GM_GROUPS, LANES), lambda i: (0, 0)),
                  pl.BlockSpec((GM_GROUPS, LANES), lambda i: (0, 0)),
                  pl.BlockSpec((GM_BLOCK, GM_GROUPS), lambda i: (0, 0))],
        out_specs=pl.BlockSpec((tg, GM_WIDTH), lambda i: (i, 0)),
        out_shape=jax.ShapeDtypeStruct((s, GM_WIDTH), BF16),
        compiler_params=_cparams(("arbitrary",)),
        name="gmlp",
    )(uvk, uvk, gm_w_s, gm_ln_g, gm_ln_b, gm_b_s_t)


def _softplus2(u):
    return jnp.maximum(u, 0.0) + jnp.log2(1.0 + jnp.exp2(-jnp.abs(u)))


def _sb_kernel(qt_ref, k_ref, vt_ref, o_ref, *acc_refs, tq, bk):
    q0 = pl.program_id(0) * tq
    upper = (lax.broadcasted_iota(I32, (bk, bk), 1)
             >= lax.broadcasted_iota(I32, (bk, bk), 0)).astype(BF16)
    upper2 = jnp.concatenate([upper, upper], axis=1)

    heads = range(SB_HEADS)
    hrows = [slice(h * SB_HEAD_DIM, (h + 1) * SB_HEAD_DIM) for h in heads]

    def tiles(ks, carries, mask):
        zs = [jnp.dot(k_ref[pl.ds(ks, bk), hrows[h]], qt_ref[hrows[h], :],
                      preferred_element_type=F32) for h in heads]
        sps = [_softplus2(z) for z in zs]
        if mask is not None:
            sps = [jnp.where(mask, sp, 0.0) for sp in sps]
        hls = [jnp.concatenate(_split_bf16(sp), axis=0) for sp in sps]
        csums = [jnp.dot(upper2, hls[h], preferred_element_type=F32) + carries[h] for h in heads]
        probs = [jnp.exp2(zs[h] - csums[h]) for h in heads]
        if mask is not None:
            probs = [jnp.where(mask, a, 0.0) for a in probs]
        for h in heads:
            acc_refs[h][...] += jnp.dot(vt_ref[hrows[h], pl.ds(ks, bk)], probs[h].astype(BF16),
                                        preferred_element_type=F32)
        return [csum[0:1, :] for csum in csums]

    for acc_ref in acc_refs:
        acc_ref[...] = jnp.zeros_like(acc_ref)
    carries = [jnp.zeros((1, tq), F32) for _ in range(SB_HEADS)]
    key_in_tile = lax.broadcasted_iota(I32, (bk, tq), 0)
    q_in_tile = lax.broadcasted_iota(I32, (bk, tq), 1)
    for d in reversed(range(tq // bk)):
        mask = (key_in_tile + d * bk) < q_in_tile
        ks = pl.multiple_of(q0 + d * bk, bk)
        carries = tiles(ks, carries, mask)

    def cond(state):
        j = state[0]
        lowest = functools.reduce(jnp.minimum, state[1:])
        return jnp.logical_and(j >= 0, jnp.min(lowest) < SB_UNDERFLOW)

    def body(state):
        j = state[0]
        ks = pl.multiple_of(j * bk, bk)
        return (j - 1,) + tuple(tiles(ks, state[1:], None))

    lax.while_loop(cond, body, (q0 // bk - 1,) + tuple(carries))
    for h in range(SB_HEADS):
        o_ref[:, h * SB_HEAD_DIM:(h + 1) * SB_HEAD_DIM] = acc_refs[h][...].T.astype(o_ref.dtype)


def _stick_breaking(qvt, uvk, tq, bk):
    s = uvk.shape[0]
    return pl.pallas_call(
        functools.partial(_sb_kernel, tq=tq, bk=bk),
        grid=(s // tq,),
        in_specs=[pl.BlockSpec((SB_WIDTH, tq), lambda i: (0, i)),
                  _resident((s, SB_WIDTH), lambda i: (0, 2 * GM_WIDTH // SB_WIDTH)),
                  _resident((SB_WIDTH, s), lambda i: (1, 0))],
        out_specs=pl.BlockSpec((tq, SB_WIDTH), lambda i: (i, 0)),
        out_shape=jax.ShapeDtypeStruct((s, SB_WIDTH), BF16),
        scratch_shapes=[pltpu.VMEM((SB_HEAD_DIM, tq), F32) for _ in range(SB_HEADS)],
        compiler_params=_cparams(("arbitrary",)),
        name="stick_breaking",
    )(qvt, uvk, qvt)


def _mixout_kernel(a_ref, b_ref, wa_ref, wb_ref, x_ref, g_ref, beta_ref, o_ref):
    mixed = (jnp.dot(a_ref[...], wa_ref[...], preferred_element_type=F32)
             + jnp.dot(b_ref[...], wb_ref[...], preferred_element_type=F32))
    y = DN_ALPHA * x_ref[...] + mixed
    o_ref[...] = _layer_norm(y, g_ref[...], beta_ref[...])


def _mixout(mix_a, mix_b, w, x, g, b, tm):
    s = x.shape[0]
    return pl.pallas_call(
        _mixout_kernel,
        grid=(s // tm,),
        in_specs=[pl.BlockSpec((tm, GM_WIDTH), lambda i: (i, 0)),
                  pl.BlockSpec((tm, SB_WIDTH), lambda i: (i, 0)),
                  pl.BlockSpec((GM_WIDTH, D_MODEL), lambda i: (0, 0)),
                  pl.BlockSpec((SB_WIDTH, D_MODEL), lambda i: (1, 0)),
                  pl.BlockSpec((tm, D_MODEL), lambda i: (i, 0)),
                  pl.BlockSpec((1, D_MODEL), lambda i: (0, 0)),
                  pl.BlockSpec((1, D_MODEL), lambda i: (0, 0))],
        out_specs=pl.BlockSpec((tm, D_MODEL), lambda i: (i, 0)),
        out_shape=jax.ShapeDtypeStruct((s, D_MODEL), F32),
        compiler_params=_cparams(("arbitrary",)),
        name="mixout_ln1",
    )(mix_a, mix_b, w, w, x, g, b)


def _route(logits):
    lane = lax.broadcasted_iota(I32, logits.shape, 1)
    neg = jnp.float32(-jnp.inf)
    big = jnp.int32(1 << 20)

    def first_argmax(vals):
        m = jnp.max(vals, axis=-1, keepdims=True)
        idx = jnp.min(jnp.where(vals == m, lane, big), axis=-1, keepdims=True)
        return m, idx

    is_group = jnp.logical_and(lane >= N_EXPERTS, lane < N_EXPERTS + N_GROUPS)
    gl = jnp.where(is_group, logits, neg)
    gmax, glane = first_argmax(gl)
    g_val = 1.0 / jnp.sum(jnp.exp(gl - gmax), axis=-1, keepdims=True)
    gidx = glane - N_EXPERTS
    in_group = jnp.logical_and(lane >= gidx * EXPERTS_PER_GROUP,
                               lane < (gidx + 1) * EXPERTS_PER_GROUP)
    el = jnp.where(in_group, logits, neg)
    m1, i1 = first_argmax(el)
    m2, i2 = first_argmax(jnp.where(lane == i1, neg, el))
    e2 = jnp.exp(m2 - m1)
    gate1 = g_val / (1.0 + e2)
    gate2 = g_val * e2 / (1.0 + e2)
    return jnp.where(lane == 0, i1.astype(F32),
                     jnp.where(lane == 1, i2.astype(F32),
                               jnp.where(lane == 2, gate1,
                                         jnp.where(lane == 3, gate2, 0.0))))


def _memattn_kernel(h_ref, wq_ref, km_ref, vm_ref, wo_ref, g_ref, beta_ref, wrh_ref, wrl_ref, br_ref,
                    o_ref, r_ref):
    h = h_ref[...]
    q = jnp.dot(h.astype(BF16), wq_ref[...], preferred_element_type=F32).astype(BF16)
    outs = []
    for hd in range(MEM_HEADS):
        cols = slice(hd * MEM_HEAD_DIM, (hd + 1) * MEM_HEAD_DIM)
        kh = km_ref[:, cols]
        vh = vm_ref[:, cols]
        s = lax.dot_general(q[:, cols], kh, (((1,), (1,)), ((), ())),
                            preferred_element_type=F32) * (MEM_HEAD_DIM ** -0.5)
        s = s - jnp.max(s, axis=-1, keepdims=True)
        e = jnp.exp(s)
        p = e / jnp.sum(e, axis=-1, keepdims=True)
        outs.append(jnp.dot(p.astype(BF16), vh, preferred_element_type=F32).astype(BF16))
    o = jnp.concatenate(outs, axis=-1)
    attn = jnp.dot(o, wo_ref[...], preferred_element_type=F32)
    h2 = _layer_norm(DN_ALPHA * h + attn, g_ref[...], beta_ref[...])
    o_ref[...] = h2
    h_hi, h_lo = _split_bf16(h2)
    logits = (jnp.dot(h_hi, wrh_ref[...], preferred_element_type=F32)
              + jnp.dot(h_lo, wrh_ref[...], preferred_element_type=F32)
              + jnp.dot(h_hi, wrl_ref[...], preferred_element_type=F32)) + br_ref[...]
    r_ref[...] = _route(logits)


def _memattn(h1, wq, k_mem, v_mem, wo, g, b, wr_hi, wr_lo, b_r, tm):
    s = h1.shape[0]
    return pl.pallas_call(
        _memattn_kernel,
        grid=(s // tm,),
        in_specs=[pl.BlockSpec((tm, D_MODEL), lambda i: (i, 0)),
                  _resident((D_MODEL, D_MODEL), lambda i: (0, 0)),
                  _resident((MEM_LEN, D_MODEL), lambda i: (0, 0)),
                  _resident((MEM_LEN, D_MODEL), lambda i: (0, 0)),
                  _resident((D_MODEL, D_MODEL), lambda i: (0, 0)),
                  pl.BlockSpec((1, D_MODEL), lambda i: (0, 0)),
                  pl.BlockSpec((1, D_MODEL), lambda i: (0, 0)),
                  _resident((D_MODEL, LANES), lambda i: (0, 0)),
                  _resident((D_MODEL, LANES), lambda i: (0, 0)),
                  pl.BlockSpec((1, LANES), lambda i: (0, 0))],
        out_specs=[pl.BlockSpec((tm, D_MODEL), lambda i: (i, 0)),
                   pl.BlockSpec((tm, LANES), lambda i: (i, 0))],
        out_shape=[jax.ShapeDtypeStruct((s, D_MODEL), F32),
                   jax.ShapeDtypeStruct((s, LANES), F32)],
        compiler_params=_cparams(("arbitrary",)),
        name="memattn_ln2_router",
    )(h1, wq, k_mem, v_mem, wo, g, b, wr_hi, wr_lo, b_r)


def _dispatch_kernel(e_ref, tok_ref, pos_ref, meta_ref):
    ca = DISPATCH_CHUNK
    n_chunks = N_ASSIGN // ca
    e_iota = lax.broadcasted_iota(I32, (N_EXPERTS, ca), 0)

    def onehot(c):
        e_c = e_ref[:, pl.ds(pl.multiple_of(c * ca, ca), ca)]
        return (e_iota == e_c).astype(F32)

    counts = lax.fori_loop(
        0, n_chunks, lambda c, acc: acc + jnp.sum(onehot(c), axis=1, keepdims=True),
        jnp.zeros((N_EXPERTS, 1), F32))
    nblk = jnp.right_shift(counts.astype(I32) + (MOE_ROWS - 1), MOE_ROWS_LOG2)
    strict_lower = (lax.broadcasted_iota(I32, (N_EXPERTS, N_EXPERTS), 1)
                    < lax.broadcasted_iota(I32, (N_EXPERTS, N_EXPERTS), 0)).astype(BF16)
    nblk_lanes = jnp.broadcast_to(nblk.astype(F32), (N_EXPERTS, LANES)).astype(BF16)
    blk_start = jnp.dot(strict_lower, nblk_lanes, preferred_element_type=F32)
    pad_start = blk_start[:, 0:1] * MOE_ROWS
    lane = lax.broadcasted_iota(I32, (N_EXPERTS, LANES), 1)
    meta_ref[...] = jnp.where(lane == 0, blk_start.astype(I32),
                              jnp.where(lane == 1, jnp.broadcast_to(nblk, (N_EXPERTS, LANES)), 0))

    earlier = (lax.broadcasted_iota(I32, (ca, ca), 0)
               < lax.broadcasted_iota(I32, (ca, ca), 1)).astype(BF16)
    blk_iota = lax.broadcasted_iota(I32, (N_ROW_BLOCKS, ca), 0)
    row_iota = lax.broadcasted_iota(I32, (MOE_ROWS, ca), 0)
    a_iota = lax.broadcasted_iota(I32, (1, ca), 1)

    def body(c, state):
        seen, acc_hi, acc_lo = state
        oh = onehot(c)
        before = jnp.dot(oh.astype(BF16), earlier, preferred_element_type=F32) + seen
        dest = jnp.sum(oh * (before + pad_start), axis=0, keepdims=True).astype(I32)
        pos_ref[:, pl.ds(pl.multiple_of(c * ca, ca), ca)] = dest
        in_blk = (blk_iota == jnp.right_shift(dest, MOE_ROWS_LOG2)).astype(F32)
        in_row = (row_iota == jnp.bitwise_and(dest, MOE_ROWS - 1)).astype(BF16)
        tok = jnp.right_shift(c * ca + a_iota, 1)
        tok_hi = (jnp.right_shift(tok, 7) + 1).astype(F32)
        tok_lo = jnp.bitwise_and(tok, 127).astype(F32)
        nt = (((1,), (1,)), ((), ()))
        acc_hi = acc_hi + lax.dot_general((in_blk * tok_hi).astype(BF16), in_row, nt,
                                          preferred_element_type=F32)
        acc_lo = acc_lo + lax.dot_general((in_blk * tok_lo).astype(BF16), in_row, nt,
                                          preferred_element_type=F32)
        return seen + jnp.sum(oh, axis=1, keepdims=True), acc_hi, acc_lo

    zeros = jnp.zeros((N_ROW_BLOCKS, MOE_ROWS), F32)
    _, acc_hi, acc_lo = lax.fori_loop(0, n_chunks, body,
                                      (jnp.zeros((N_EXPERTS, 1), F32), zeros, zeros))
    pad_tok = jnp.bitwise_and(
        lax.broadcasted_iota(I32, (N_ROW_BLOCKS, MOE_ROWS), 0) * MOE_ROWS
        + lax.broadcasted_iota(I32, (N_ROW_BLOCKS, MOE_ROWS), 1), SEQ - 1)
    tok_ref[...] = jnp.where(acc_hi > 0.5, ((acc_hi - 1.0) * 128.0 + acc_lo).astype(I32), pad_tok)


def _dispatch(flat_e):
    return pl.pallas_call(
        _dispatch_kernel,
        out_shape=[jax.ShapeDtypeStruct((N_ROW_BLOCKS, MOE_ROWS), I32),
                   jax.ShapeDtypeStruct((1, N_ASSIGN), I32),
                   jax.ShapeDtypeStruct((N_EXPERTS, LANES), I32)],
        compiler_params=pltpu.CompilerParams(vmem_limit_bytes=VMEM_LIMIT),
        name="moe_dispatch",
    )(flat_e)


def _expert_kernel(bs_ref, nb_ref, tot_ref, tok_hbm, h_hbm, w1_hbm, w3_hbm, w2_hbm, y_hbm,
                   idx, xbuf, ybuf, isem, gsem, ysem, wsem, w1f, w3f, w2f, w1b, w3b, w2b):
    e = pl.program_id(0)
    n_exp = pl.num_programs(0)
    total = tot_ref[0]
    first = bs_ref[e]
    nblk = nb_ref[e]

    def idx_copy(blk):
        row = jnp.minimum(blk, N_ROW_BLOCKS - 1)
        slot = blk % GATHER_SLOTS
        return pltpu.make_async_copy(tok_hbm.at[pl.ds(row, 1)], idx.at[pl.ds(slot, 1)],
                                     isem.at[slot])

    def issue_rows(blk, rows):
        slot = blk % GATHER_SLOTS
        for r in rows:
            tok = idx[slot, r]
            pltpu.make_async_copy(h_hbm.at[pl.ds(tok, 1)], xbuf.at[slot, pl.ds(r, 1)],
                                  gsem.at[slot]).start()

    def wait_gather(blk):
        slot = blk % GATHER_SLOTS
        pltpu.make_async_copy(h_hbm.at[pl.ds(0, MOE_ROWS)], xbuf.at[slot], gsem.at[slot]).wait()

    def y_copy(blk, slot):
        return pltpu.make_async_copy(ybuf.at[slot], y_hbm.at[pl.ds(blk * MOE_ROWS, MOE_ROWS)],
                                     ysem.at[slot])

    def weight_copies(ex):
        slot = ex % 2
        return [pltpu.make_async_copy(src.at[ex], dst.at[slot], wsem.at[slot])
                for src, dst in ((w1_hbm, w1f), (w3_hbm, w3f), (w2_hbm, w2f))]

    @pl.when(e == 0)
    def _():
        for cp in weight_copies(0):
            cp.start(priority=1)

        for blk in range(GATHER_SLOTS):
            idx_copy(blk).start()
        for blk in range(GATHER_SLOTS - 1):
            idx_copy(blk).wait()

        def prime(r, c):
            for blk in range(GATHER_SLOTS - 1):
                pltpu.make_async_copy(h_hbm.at[pl.ds(idx[blk, r], 1)], xbuf.at[blk, pl.ds(r, 1)],
                                      gsem.at[blk]).start()
            return c
        lax.fori_loop(0, MOE_ROWS, prime, 0, unroll=8)

    @pl.when(e + 1 < n_exp)
    def _():
        for cp in weight_copies(e + 1):
            cp.start(priority=1)

    for cp in weight_copies(e):
        cp.wait()

    @pl.when(nblk > 0)
    def _():
        wslot = e % 2
        w1b[...] = w1f[wslot].astype(BF16)
        w3b[...] = w3f[wslot].astype(BF16)
        w2b[...] = w2f[wslot].astype(BF16)

    n_piece = 4
    rows_per_piece = MOE_ROWS // n_piece
    cols_per_piece = D_MODEL // n_piece

    def block(b, c):
        blk = first + b
        slot = blk % 2
        wait_gather(blk)
        idx_copy(blk + GATHER_SLOTS - 1).wait()
        idx_copy(blk + GATHER_SLOTS).start()

        @pl.when(blk >= 2)
        def _():
            y_copy(blk - 2, slot).wait()

        xb = xbuf[blk % GATHER_SLOTS].astype(BF16)
        a = jnp.dot(xb, w1b[...], preferred_element_type=F32)
        g = jnp.dot(xb, w3b[...], preferred_element_type=F32)
        hid = (a * jax.nn.sigmoid(a) * g).astype(BF16)
        for p in range(n_piece):
            issue_rows(blk + GATHER_SLOTS - 1,
                       range(p * rows_per_piece, (p + 1) * rows_per_piece))
            cols = slice(p * cols_per_piece, (p + 1) * cols_per_piece)
            ybuf[slot, :, cols] = jnp.dot(hid, w2b[:, cols], preferred_element_type=F32)
        y_copy(blk, slot).start()
        return c

    lax.fori_loop(0, nblk, block, 0)

    @pl.when(e == n_exp - 1)
    def _():
        for ahead in range(GATHER_SLOTS - 1):
            wait_gather(total + ahead)
        idx_copy(total + GATHER_SLOTS - 1).wait()
        for back in (1, 2):
            @pl.when(total >= back)
            def _():
                y_copy(total - back, (total - back) % 2).wait()

        ybuf[0] = jnp.zeros((MOE_ROWS, D_MODEL), F32)

        def zero_block(blk, c):
            y_copy(blk, 0).start()
            y_copy(blk, 0).wait()
            return c

        lax.fori_loop(total, N_ROW_BLOCKS, zero_block, 0)


def _experts(blk_start, nblk, total, row_tok, h2, w1, w3, w2):
    grid_spec = pltpu.PrefetchScalarGridSpec(
        num_scalar_prefetch=3,
        grid=(N_EXPERTS,),
        in_specs=[pl.BlockSpec(memory_space=pl.ANY)] * 5,
        out_specs=pl.BlockSpec(memory_space=pl.ANY),
        scratch_shapes=[pltpu.SMEM((GATHER_SLOTS, MOE_ROWS), I32),
                        pltpu.VMEM((GATHER_SLOTS, MOE_ROWS, D_MODEL), F32),
                        pltpu.VMEM((2, MOE_ROWS, D_MODEL), F32),
                        pltpu.SemaphoreType.DMA((GATHER_SLOTS,)),
                        pltpu.SemaphoreType.DMA((GATHER_SLOTS,)),
                        pltpu.SemaphoreType.DMA((2,)),
                        pltpu.SemaphoreType.DMA((2,)),
                        pltpu.VMEM((2, D_MODEL, D_EXPERT), F32),
                        pltpu.VMEM((2, D_MODEL, D_EXPERT), F32),
                        pltpu.VMEM((2, D_EXPERT, D_MODEL), F32),
                        pltpu.VMEM((D_MODEL, D_EXPERT), BF16),
                        pltpu.VMEM((D_MODEL, D_EXPERT), BF16),
                        pltpu.VMEM((D_EXPERT, D_MODEL), BF16)],
    )
    return pl.pallas_call(
        _expert_kernel,
        grid_spec=grid_spec,
        out_shape=jax.ShapeDtypeStruct((N_ROWS, D_MODEL), F32),
        compiler_params=_cparams(("arbitrary",)),
        name="experts",
    )(blk_start, nblk, total, row_tok, h2, w1, w3, w2)


def _combine_kernel(y_hbm, pos_ref, pos_next_ref, h_ref, r_ref, g_ref, beta_ref, o_ref, ybuf, sem,
                    *, tc):
    i = pl.program_id(0)
    n = pl.num_programs(0)
    slot = i % 2

    def start_gather(table, slot_):
        def issue(r, c):
            for kk in range(TOP_K):
                p = table[0, 0, r * TOP_K + kk]
                pltpu.make_async_copy(y_hbm.at[pl.ds(p, 1)], ybuf.at[slot_, kk, pl.ds(r, 1)],
                                      sem.at[slot_]).start()
            return c
        lax.fori_loop(0, tc, issue, 0, unroll=8)

    @pl.when(i == 0)
    def _():
        start_gather(pos_ref, 0)

    @pl.when(i + 1 < n)
    def _():
        start_gather(pos_next_ref, 1 - slot)

    for kk in range(TOP_K):
        pltpu.make_async_copy(y_hbm.at[pl.ds(0, tc)], ybuf.at[slot, kk], sem.at[slot]).wait()
    gates = r_ref[...]
    moe = (ybuf[slot, 0] * gates[:, TOP_K:TOP_K + 1]
           + ybuf[slot, 1] * gates[:, TOP_K + 1:TOP_K + 2])
    o_ref[...] = _layer_norm(DN_ALPHA * h_ref[...] + moe, g_ref[...], beta_ref[...])


def _combine(pos, y_rows, h2, rout, g, b, tc):
    s = h2.shape[0]
    n = s // tc
    pos = pos.reshape(n, 1, tc * TOP_K)
    smem_block = functools.partial(pl.BlockSpec, (1, 1, tc * TOP_K), memory_space=pltpu.SMEM)
    return pl.pallas_call(
        functools.partial(_combine_kernel, tc=tc),
        grid=(n,),
        in_specs=[pl.BlockSpec(memory_space=pl.ANY),
                  smem_block(lambda i: (i, 0, 0)),
                  smem_block(lambda i: (jnp.minimum(i + 1, n - 1), 0, 0)),
                  pl.BlockSpec((tc, D_MODEL), lambda i: (i, 0)),
                  pl.BlockSpec((tc, LANES), lambda i: (i, 0)),
                  pl.BlockSpec((1, D_MODEL), lambda i: (0, 0)),
                  pl.BlockSpec((1, D_MODEL), lambda i: (0, 0))],
        out_specs=pl.BlockSpec((tc, D_MODEL), lambda i: (i, 0)),
        out_shape=jax.ShapeDtypeStruct((s, D_MODEL), F32),
        scratch_shapes=[pltpu.VMEM((2, TOP_K, tc, D_MODEL), F32),
                        pltpu.SemaphoreType.DMA((2,))],
        compiler_params=_cparams(("arbitrary",)),
        name="combine_ln3",
    )(y_rows, pos, pos, h2, rout, g, b)


def kernel(x, mem, w_in, gm_ln_g, gm_ln_b, gm_w_s, gm_b_s, w_mix_out, ln1_g, ln1_b,
           mem_w_q, mem_w_k, mem_w_v, mem_w_o, ln2_g, ln2_b,
           w_group, b_group, w_router, b_router, w1, w3, w2, ln3_g, ln3_b):
    assert x.shape == (1, SEQ, D_MODEL) and w_in.shape[0] == 1
    xs = x[0]
    l = 0
    w_mix = w_mix_out[l].astype(BF16)
    w_q = mem_w_q[l].astype(BF16)
    w_o = mem_w_o[l].astype(BF16)
    w_r = jnp.concatenate(
        [w_router[l].transpose(1, 0, 2).reshape(D_MODEL, N_EXPERTS), w_group[l],
         jnp.zeros((D_MODEL, LANES - N_EXPERTS - N_GROUPS), F32)], axis=1)
    wr_hi, wr_lo = _split_bf16(w_r)
    b_r = jnp.concatenate([b_router[l].reshape(-1), b_group[l],
                           jnp.zeros((LANES - N_EXPERTS - N_GROUPS,), F32)]).reshape(1, LANES)
    row = lambda v: v.reshape(1, D_MODEL)

    uvk = _proj(xs, w_in[l], (0, 1, 3), tm=1024, tn=1024)
    qvt = _proj(xs, w_in[l], (2, 4), tm=1024, tn=1024, transposed=True,
                first_scale=SB_HEAD_DIM ** -0.5 * math.log2(math.e))
    mix_a = _gmlp(uvk, gm_w_s[l], gm_ln_g[l], gm_ln_b[l], gm_b_s[l].T, tg=512)
    mix_b = _stick_breaking(qvt, uvk, tq=256, bk=128)
    h1 = _mixout(mix_a, mix_b, w_mix, xs, row(ln1_g[l]), row(ln1_b[l]), tm=512)
    k_mem = _proj(mem[0], mem_w_k[l], (0, 1), tm=MEM_LEN, tn=1024)
    v_mem = _proj(mem[0], mem_w_v[l], (0, 1), tm=MEM_LEN, tn=1024)
    h2, rout = _memattn(h1, w_q, k_mem, v_mem, w_o, row(ln2_g[l]), row(ln2_b[l]), wr_hi, wr_lo, b_r, tm=256)
    flat_e = rout[:, 0:TOP_K].astype(I32).reshape(1, N_ASSIGN)
    row_tok, pos, meta = _dispatch(flat_e)
    blk_start, nblk = meta[:, 0], meta[:, 1]
    total = (blk_start[N_EXPERTS - 1] + nblk[N_EXPERTS - 1]).reshape(1)
    y_rows = _experts(blk_start, nblk, total, row_tok, h2, w1[l], w3[l], w2[l])
    out = _combine(pos.reshape(N_ASSIGN), y_rows, h2, rout, row(ln3_g[l]), row(ln3_b[l]), tc=256)
    return out.reshape(1, SEQ, D_MODEL)
```

```python
import functools
import math

import jax
import jax.numpy as jnp
from jax import lax
from jax.experimental import pallas as pl
from jax.experimental.pallas import tpu as pltpu

D_MODEL = 2048
SEQ = 8192
CHUNK = 64
MEM_LEN = 256
GM_BLOCK = 128
GM_GROUPS = 8
GM_WIDTH = 1024
SB_HEADS = 8
SB_HEAD_DIM = 128
SB_WIDTH = 1024
MEM_HEADS = 4
MEM_HEAD_DIM = 512
N_GROUPS = 8
EXPERTS_PER_GROUP = 8
N_EXPERTS = 64
TOP_K = 2
D_EXPERT = 512
DN_ALPHA = 2.0 ** 0.25
LN_EPS = 1e-5

LANES = 128
MOE_ROWS = 128
MOE_ROWS_LOG2 = 7
N_ASSIGN = SEQ * TOP_K
N_ROW_BLOCKS = N_ASSIGN // MOE_ROWS + N_EXPERTS
N_ROWS = N_ROW_BLOCKS * MOE_ROWS
DISPATCH_CHUNK = 256
GATHER_SLOTS = 3
SB_UNDERFLOW = 160.0
SB_LOOP_BLOCKS = 1
VMEM_LIMIT = 56 * 1024 * 1024

BF16 = jnp.bfloat16
F32 = jnp.float32
I32 = jnp.int32


def _cparams(sem):
    return pltpu.CompilerParams(dimension_semantics=sem, vmem_limit_bytes=VMEM_LIMIT)


def _resident(shape, index_map):
    return pl.BlockSpec(shape, index_map, pipeline_mode=pl.Buffered(1))


def _layer_norm(y, g, b):
    mu = jnp.mean(y, axis=-1, keepdims=True)
    yc = y - mu
    var = jnp.mean(yc * yc, axis=-1, keepdims=True)
    return yc * lax.rsqrt(var + LN_EPS) * g + b


def _split_bf16(v):
    hi = v.astype(BF16)
    lo = (v - hi.astype(F32)).astype(BF16)
    return hi, lo


def _proj_kernel(x_ref, w_ref, o_ref, xb_ref):
    @pl.when(pl.program_id(1) == 0)
    def _():
        xb_ref[...] = x_ref[...].astype(BF16)

    o_ref[...] = jnp.dot(xb_ref[...], w_ref[...].astype(BF16),
                         preferred_element_type=F32).astype(o_ref.dtype)


def _proj(x, w, tm, tn):
    m, k = x.shape
    n = w.shape[1]
    return pl.pallas_call(
        _proj_kernel,
        grid=(m // tm, n // tn),
        in_specs=[pl.BlockSpec((tm, k), lambda i, j: (i, 0)),
                  pl.BlockSpec((k, tn), lambda i, j: (0, j))],
        out_specs=pl.BlockSpec((tm, tn), lambda i, j: (i, j)),
        out_shape=jax.ShapeDtypeStruct((m, n), BF16),
        scratch_shapes=[pltpu.VMEM((tm, k), BF16)],
        compiler_params=_cparams(("arbitrary", "arbitrary")),
        name="proj",
    )(x, w)


IN_BLOCKS = 5
IN_Q, IN_K, IN_V = 2, 3, 4


def _in_proj_kernel(x_ref, w_ref, o_ref, ot_ref, xb_ref, *, q_scale):
    j = pl.program_id(1)

    @pl.when(j == 0)
    def _():
        xb_ref[...] = x_ref[...].astype(BF16)

    wb = w_ref[...].astype(BF16)
    is_t = jnp.logical_or(j == IN_Q, j == IN_V)

    @pl.when(jnp.logical_not(is_t))
    def _():
        o_ref[...] = jnp.dot(xb_ref[...], wb, preferred_element_type=F32).astype(o_ref.dtype)

    @pl.when(is_t)
    def _():
        acc = lax.dot_general(wb, xb_ref[...], (((0,), (1,)), ((), ())), preferred_element_type=F32)
        ot_ref[...] = (acc * jnp.where(j == IN_Q, q_scale, 1.0).astype(F32)).astype(ot_ref.dtype)


def _in_proj(x, w, tm, q_scale):
    m, k = x.shape
    tn = SB_WIDTH
    omap = lambda i, j: (i, j - (j >= IN_Q).astype(I32) - (j >= IN_V).astype(I32))
    otmap = lambda i, j: ((j >= IN_V).astype(I32), i)
    return pl.pallas_call(
        functools.partial(_in_proj_kernel, q_scale=q_scale),
        grid=(m // tm, IN_BLOCKS),
        in_specs=[pl.BlockSpec((tm, k), lambda i, j: (i, 0)),
                  pl.BlockSpec((k, tn), lambda i, j: (0, j))],
        out_specs=[pl.BlockSpec((tm, tn), omap),
                   pl.BlockSpec((tn, tm), otmap)],
        out_shape=[jax.ShapeDtypeStruct((m, 3 * tn), BF16),
                   jax.ShapeDtypeStruct((2 * tn, m), BF16)],
        scratch_shapes=[pltpu.VMEM((tm, k), BF16)],
        compiler_params=_cparams(("arbitrary", "arbitrary")),
        name="in_proj",
    )(x, w)


def _gmlp_kernel(u_ref, v_ref, ws_ref, g_ref, b_ref, bs_ref, o_ref, *, tg):
    t_chunk = lax.broadcasted_iota(I32, (GM_BLOCK, GM_BLOCK), 0) // CHUNK
    s_chunk = lax.broadcasted_iota(I32, (GM_BLOCK, GM_BLOCK), 1) // CHUNK
    causal = t_chunk >= s_chunk
    for g in range(GM_GROUPS):
        wm = jnp.where(causal, ws_ref[g], 0.0).astype(BF16)
        ln_g = g_ref[g:g + 1, :]
        ln_b = b_ref[g:g + 1, :]
        bias = bs_ref[:, g:g + 1]
        cols = slice(g * LANES, (g + 1) * LANES)
        for n in range(tg // GM_BLOCK):
            rows = slice(n * GM_BLOCK, (n + 1) * GM_BLOCK)
            v = jax.nn.gelu(v_ref[rows, cols].astype(F32))
            vg = _layer_norm(v, ln_g, ln_b).astype(BF16)
            mixed = jnp.dot(wm, vg, preferred_element_type=F32) + bias
            u = jax.nn.gelu(u_ref[rows, cols].astype(F32))
            o_ref[rows, cols] = (u * mixed).astype(o_ref.dtype)


def _gmlp(uvk, gm_w_s, gm_ln_g, gm_ln_b, gm_b_s_t, tg):
    s = uvk.shape[0]
    return pl.pallas_call(
        functools.partial(_gmlp_kernel, tg=tg),
        grid=(s // tg,),
        in_specs=[pl.BlockSpec((tg, GM_WIDTH), lambda i: (i, 0)),
                  pl.BlockSpec((tg, GM_WIDTH), lambda i: (i, 1)),
                  pl.BlockSpec((GM_GROUPS, GM_BLOCK, GM_BLOCK), lambda i: (0, 0, 0)),
                  pl.BlockSpec((GM_GROUPS, LANES), lambda i: (0, 0)),
                  pl.BlockSpec((GM_GROUPS, LANES), lambda i: (0, 0)),
                  pl.BlockSpec((GM_BLOCK, GM_GROUPS), lambda i: (0, 0))],
        out_specs=pl.BlockSpec((tg, GM_WIDTH), lambda i: (i, 0)),
        out_shape=jax.ShapeDtypeStruct((s, GM_WIDTH), BF16),
        compiler_params=_cparams(("arbitrary",)),
        name="gmlp",
    )(uvk, uvk, gm_w_s, gm_ln_g, gm_ln_b, gm_b_s_t)


def _softplus2(u):
    return jnp.maximum(u, 0.0) + jnp.log2(1.0 + jnp.exp2(-jnp.abs(u)))


def _sb_kernel(qt_ref, k_ref, vt_ref, o_ref, *acc_refs, tq, bk):
    q0 = pl.program_id(0) * tq
    upper = (lax.broadcasted_iota(I32, (bk, bk), 1)
             >= lax.broadcasted_iota(I32, (bk, bk), 0)).astype(BF16)
    upper2 = jnp.concatenate([upper, upper], axis=1)

    heads = range(SB_HEADS)
    hrows = [slice(h * SB_HEAD_DIM, (h + 1) * SB_HEAD_DIM) for h in heads]

    def tiles(blocks, carries):
        zs, sums, masks = [], [], []
        for ks, lo, masked in blocks:
            width = tq - lo
            zs.append([jnp.dot(k_ref[pl.ds(ks, bk), hrows[h]], qt_ref[hrows[h], lo:],
                               preferred_element_type=F32) for h in heads])
            masks.append((lax.broadcasted_iota(I32, (bk, width), 0)
                          < lax.broadcasted_iota(I32, (bk, width), 1)) if masked else None)
        for zb, mask in zip(zs, masks):
            sps = [_softplus2(z) for z in zb]
            if mask is not None:
                sps = [jnp.where(mask, sp, 0.0) for sp in sps]
            sums.append([jnp.dot(upper2, jnp.concatenate(_split_bf16(sp), axis=0),
                                 preferred_element_type=F32) for sp in sps])
        csums = []
        for (ks, lo, masked), sb in zip(blocks, sums):
            cb = [sb[h] + carries[h][:, lo:] for h in heads]
            carries = [cb[h][0:1, :] if lo == 0
                       else jnp.concatenate([carries[h][:, :lo], cb[h][0:1, :]], axis=1)
                       for h in heads]
            csums.append(cb)
        probs = []
        for zb, cb, mask in zip(zs, csums, masks):
            pb = [jnp.exp2(zb[h] - cb[h]) for h in heads]
            if mask is not None:
                pb = [jnp.where(mask, a, 0.0) for a in pb]
            probs.append([a.astype(BF16) for a in pb])
        for (ks, lo, masked), pb in zip(blocks, probs):
            for h in heads:
                acc_refs[h][:, lo:] += jnp.dot(vt_ref[hrows[h], pl.ds(ks, bk)], pb[h],
                                               preferred_element_type=F32)
        return carries

    for acc_ref in acc_refs:
        acc_ref[...] = jnp.zeros_like(acc_ref)
    carries = [jnp.zeros((1, tq), F32) for _ in range(SB_HEADS)]
    carries = tiles([(pl.multiple_of(q0 + d * bk, bk), d * bk, True)
                     for d in reversed(range(tq // bk))], carries)

    def cond(state):
        j = state[0]
        lowest = functools.reduce(jnp.minimum, state[1:])
        return jnp.logical_and(j >= 0, jnp.min(lowest) < SB_UNDERFLOW)

    def body(state):
        j = state[0]
        blocks = [(pl.multiple_of((j - b) * bk, bk), 0, False) for b in range(SB_LOOP_BLOCKS)]
        return (j - SB_LOOP_BLOCKS,) + tuple(tiles(blocks, list(state[1:])))

    assert (tq // bk) % SB_LOOP_BLOCKS == 0
    lax.while_loop(cond, body, (q0 // bk - 1,) + tuple(carries))
    for h in range(SB_HEADS):
        o_ref[:, h * SB_HEAD_DIM:(h + 1) * SB_HEAD_DIM] = acc_refs[h][...].T.astype(o_ref.dtype)


def _stick_breaking(qvt, uvk, tq, bk):
    s = uvk.shape[0]
    return pl.pallas_call(
        functools.partial(_sb_kernel, tq=tq, bk=bk),
        grid=(s // tq,),
        in_specs=[pl.BlockSpec((SB_WIDTH, tq), lambda i: (0, i)),
                  _resident((s, SB_WIDTH), lambda i: (0, 2 * GM_WIDTH // SB_WIDTH)),
                  _resident((SB_WIDTH, s), lambda i: (1, 0))],
        out_specs=pl.BlockSpec((tq, SB_WIDTH), lambda i: (i, 0)),
        out_shape=jax.ShapeDtypeStruct((s, SB_WIDTH), BF16),
        scratch_shapes=[pltpu.VMEM((SB_HEAD_DIM, tq), F32) for _ in range(SB_HEADS)],
        compiler_params=_cparams(("arbitrary",)),
        name="stick_breaking",
    )(qvt, uvk, qvt)


def _mixout_kernel(a_ref, b_ref, wa_ref, wb_ref, x_ref, g_ref, beta_ref, o_ref):
    mixed = (jnp.dot(a_ref[...], wa_ref[...], preferred_element_type=F32)
             + jnp.dot(b_ref[...], wb_ref[...], preferred_element_type=F32))
    y = DN_ALPHA * x_ref[...] + mixed
    o_ref[...] = _layer_norm(y, g_ref[...], beta_ref[...])


def _mixout(mix_a, mix_b, w, x, g, b, tm):
    s = x.shape[0]
    return pl.pallas_call(
        _mixout_kernel,
        grid=(s // tm,),
        in_specs=[pl.BlockSpec((tm, GM_WIDTH), lambda i: (i, 0)),
                  pl.BlockSpec((tm, SB_WIDTH), lambda i: (i, 0)),
                  pl.BlockSpec((GM_WIDTH, D_MODEL), lambda i: (0, 0)),
                  pl.BlockSpec((SB_WIDTH, D_MODEL), lambda i: (1, 0)),
                  pl.BlockSpec((tm, D_MODEL), lambda i: (i, 0)),
                  pl.BlockSpec((1, D_MODEL), lambda i: (0, 0)),
                  pl.BlockSpec((1, D_MODEL), lambda i: (0, 0))],
        out_specs=pl.BlockSpec((tm, D_MODEL), lambda i: (i, 0)),
        out_shape=jax.ShapeDtypeStruct((s, D_MODEL), F32),
        compiler_params=_cparams(("arbitrary",)),
        name="mixout_ln1",
    )(mix_a, mix_b, w, w, x, g, b)


def _route(logits):
    lane = lax.broadcasted_iota(I32, logits.shape, 1)
    neg = jnp.float32(-jnp.inf)
    big = jnp.int32(1 << 20)

    def first_argmax(vals):
        m = jnp.max(vals, axis=-1, keepdims=True)
        idx = jnp.min(jnp.where(vals == m, lane, big), axis=-1, keepdims=True)
        return m, idx

    is_group = jnp.logical_and(lane >= N_EXPERTS, lane < N_EXPERTS + N_GROUPS)
    gl = jnp.where(is_group, logits, neg)
    gmax, glane = first_argmax(gl)
    g_val = 1.0 / jnp.sum(jnp.exp(gl - gmax), axis=-1, keepdims=True)
    gidx = glane - N_EXPERTS
    in_group = jnp.logical_and(lane >= gidx * EXPERTS_PER_GROUP,
                               lane < (gidx + 1) * EXPERTS_PER_GROUP)
    el = jnp.where(in_group, logits, neg)
    m1, i1 = first_argmax(el)
    m2, i2 = first_argmax(jnp.where(lane == i1, neg, el))
    e2 = jnp.exp(m2 - m1)
    gate1 = g_val / (1.0 + e2)
    gate2 = g_val * e2 / (1.0 + e2)
    return jnp.where(lane == 0, i1.astype(F32),
                     jnp.where(lane == 1, i2.astype(F32),
                               jnp.where(lane == 2, gate1,
                                         jnp.where(lane == 3, gate2, 0.0))))


def _memattn_kernel(h_ref, wq_ref, km_ref, vm_ref, wo_ref, g_ref, beta_ref, wrh_ref, wrl_ref, br_ref,
                    o_ref, r_ref):
    h = h_ref[...]
    q = jnp.dot(h.astype(BF16), wq_ref[...], preferred_element_type=F32).astype(BF16)
    outs = []
    for hd in range(MEM_HEADS):
        cols = slice(hd * MEM_HEAD_DIM, (hd + 1) * MEM_HEAD_DIM)
        kh = km_ref[:, cols]
        vh = vm_ref[:, cols]
        s = lax.dot_general(q[:, cols], kh, (((1,), (1,)), ((), ())),
                            preferred_element_type=F32) * (MEM_HEAD_DIM ** -0.5)
        s = s - jnp.max(s, axis=-1, keepdims=True)
        e = jnp.exp(s)
        p = e / jnp.sum(e, axis=-1, keepdims=True)
        outs.append(jnp.dot(p.astype(BF16), vh, preferred_element_type=F32).astype(BF16))
    o = jnp.concatenate(outs, axis=-1)
    attn = jnp.dot(o, wo_ref[...], preferred_element_type=F32)
    h2 = _layer_norm(DN_ALPHA * h + attn, g_ref[...], beta_ref[...])
    o_ref[...] = h2
    h_hi, h_lo = _split_bf16(h2)
    logits = (jnp.dot(h_hi, wrh_ref[...], preferred_element_type=F32)
              + jnp.dot(h_lo, wrh_ref[...], preferred_element_type=F32)
              + jnp.dot(h_hi, wrl_ref[...], preferred_element_type=F32)) + br_ref[...]
    r_ref[...] = _route(logits)


def _memattn(h1, wq, k_mem, v_mem, wo, g, b, wr_hi, wr_lo, b_r, tm):
    s = h1.shape[0]
    return pl.pallas_call(
        _memattn_kernel,
        grid=(s // tm,),
        in_specs=[pl.BlockSpec((tm, D_MODEL), lambda i: (i, 0)),
                  _resident((D_MODEL, D_MODEL), lambda i: (0, 0)),
                  _resident((MEM_LEN, D_MODEL), lambda i: (0, 0)),
                  _resident((MEM_LEN, D_MODEL), lambda i: (0, 0)),
                  _resident((D_MODEL, D_MODEL), lambda i: (0, 0)),
                  pl.BlockSpec((1, D_MODEL), lambda i: (0, 0)),
                  pl.BlockSpec((1, D_MODEL), lambda i: (0, 0)),
                  _resident((D_MODEL, LANES), lambda i: (0, 0)),
                  _resident((D_MODEL, LANES), lambda i: (0, 0)),
                  pl.BlockSpec((1, LANES), lambda i: (0, 0))],
        out_specs=[pl.BlockSpec((tm, D_MODEL), lambda i: (i, 0)),
                   pl.BlockSpec((tm, LANES), lambda i: (i, 0))],
        out_shape=[jax.ShapeDtypeStruct((s, D_MODEL), F32),
                   jax.ShapeDtypeStruct((s, LANES), F32)],
        compiler_params=_cparams(("arbitrary",)),
        name="memattn_ln2_router",
    )(h1, wq, k_mem, v_mem, wo, g, b, wr_hi, wr_lo, b_r)


def _dispatch_kernel(e_ref, tok_ref, pos_ref, meta_ref):
    ca = DISPATCH_CHUNK
    n_chunks = N_ASSIGN // ca
    e_iota = lax.broadcasted_iota(I32, (N_EXPERTS, ca), 0)

    def onehot(c):
        e_c = e_ref[:, pl.ds(pl.multiple_of(c * ca, ca), ca)]
        return (e_iota == e_c).astype(F32)

    counts = lax.fori_loop(
        0, n_chunks, lambda c, acc: acc + jnp.sum(onehot(c), axis=1, keepdims=True),
        jnp.zeros((N_EXPERTS, 1), F32))
    nblk = jnp.right_shift(counts.astype(I32) + (MOE_ROWS - 1), MOE_ROWS_LOG2)
    strict_lower = (lax.broadcasted_iota(I32, (N_EXPERTS, N_EXPERTS), 1)
                    < lax.broadcasted_iota(I32, (N_EXPERTS, N_EXPERTS), 0)).astype(BF16)
    nblk_lanes = jnp.broadcast_to(nblk.astype(F32), (N_EXPERTS, LANES)).astype(BF16)
    blk_start = jnp.dot(strict_lower, nblk_lanes, preferred_element_type=F32)
    pad_start = blk_start[:, 0:1] * MOE_ROWS
    lane = lax.broadcasted_iota(I32, (N_EXPERTS, LANES), 1)
    meta_ref[...] = jnp.where(lane == 0, blk_start.astype(I32),
                              jnp.where(lane == 1, jnp.broadcast_to(nblk, (N_EXPERTS, LANES)), 0))

    earlier = (lax.broadcasted_iota(I32, (ca, ca), 0)
               < lax.broadcasted_iota(I32, (ca, ca), 1)).astype(BF16)
    blk_iota = lax.broadcasted_iota(I32, (N_ROW_BLOCKS, ca), 0)
    row_iota = lax.broadcasted_iota(I32, (MOE_ROWS, ca), 0)
    a_iota = lax.broadcasted_iota(I32, (1, ca), 1)

    def body(c, state):
        seen, acc_hi, acc_lo = state
        oh = onehot(c)
        before = jnp.dot(oh.astype(BF16), earlier, preferred_element_type=F32) + seen
        dest = jnp.sum(oh * (before + pad_start), axis=0, keepdims=True).astype(I32)
        pos_ref[:, pl.ds(pl.multiple_of(c * ca, ca), ca)] = dest
        in_blk = (blk_iota == jnp.right_shift(dest, MOE_ROWS_LOG2)).astype(F32)
        in_row = (row_iota == jnp.bitwise_and(dest, MOE_ROWS - 1)).astype(BF16)
        tok = jnp.right_shift(c * ca + a_iota, 1)
        tok_hi = (jnp.right_shift(tok, 7) + 1).astype(F32)
        tok_lo = jnp.bitwise_and(tok, 127).astype(F32)
        nt = (((1,), (1,)), ((), ()))
        acc_hi = acc_hi + lax.dot_general((in_blk * tok_hi).astype(BF16), in_row, nt,
                                          preferred_element_type=F32)
        acc_lo = acc_lo + lax.dot_general((in_blk * tok_lo).astype(BF16), in_row, nt,
                                          preferred_element_type=F32)
        return seen + jnp.sum(oh, axis=1, keepdims=True), acc_hi, acc_lo

    zeros = jnp.zeros((N_ROW_BLOCKS, MOE_ROWS), F32)
    _, acc_hi, acc_lo = lax.fori_loop(0, n_chunks, body,
                                      (jnp.zeros((N_EXPERTS, 1), F32), zeros, zeros))
    pad_tok = jnp.bitwise_and(
        lax.broadcasted_iota(I32, (N_ROW_BLOCKS, MOE_ROWS), 0) * MOE_ROWS
        + lax.broadcasted_iota(I32, (N_ROW_BLOCKS, MOE_ROWS), 1), SEQ - 1)
    tok_ref[...] = jnp.where(acc_hi > 0.5, ((acc_hi - 1.0) * 128.0 + acc_lo).astype(I32), pad_tok)


def _dispatch(flat_e):
    return pl.pallas_call(
        _dispatch_kernel,
        out_shape=[jax.ShapeDtypeStruct((N_ROW_BLOCKS, MOE_ROWS), I32),
                   jax.ShapeDtypeStruct((1, N_ASSIGN), I32),
                   jax.ShapeDtypeStruct((N_EXPERTS, LANES), I32)],
        compiler_params=pltpu.CompilerParams(vmem_limit_bytes=VMEM_LIMIT),
        name="moe_dispatch",
    )(flat_e)


def _expert_kernel(bs_ref, nb_ref, tot_ref, tok_hbm, h_hbm, w1_hbm, w3_hbm, w2_hbm, y_hbm,
                   idx, xbuf, ybuf, isem, gsem, ysem, wsem, w1f, w3f, w2f, w1b, w3b, w2b):
    e = pl.program_id(0)
    n_exp = pl.num_programs(0)
    total = tot_ref[0]
    first = bs_ref[e]
    nblk = nb_ref[e]

    def idx_copy(blk):
        row = jnp.minimum(blk, N_ROW_BLOCKS - 1)
        slot = blk % GATHER_SLOTS
        return pltpu.make_async_copy(tok_hbm.at[pl.ds(row, 1)], idx.at[pl.ds(slot, 1)],
                                     isem.at[slot])

    def issue_rows(blk, rows):
        slot = blk % GATHER_SLOTS
        for r in rows:
            tok = idx[slot, r]
            pltpu.make_async_copy(h_hbm.at[pl.ds(tok, 1)], xbuf.at[slot, pl.ds(r, 1)],
                                  gsem.at[slot]).start()

    def wait_gather(blk):
        slot = blk % GATHER_SLOTS
        pltpu.make_async_copy(h_hbm.at[pl.ds(0, MOE_ROWS)], xbuf.at[slot], gsem.at[slot]).wait()

    def y_copy(blk, slot):
        return pltpu.make_async_copy(ybuf.at[slot], y_hbm.at[pl.ds(blk * MOE_ROWS, MOE_ROWS)],
                                     ysem.at[slot])

    def weight_copies(ex):
        slot = ex % 2
        return [pltpu.make_async_copy(src.at[ex], dst.at[slot], wsem.at[slot])
                for src, dst in ((w1_hbm, w1f), (w3_hbm, w3f), (w2_hbm, w2f))]

    @pl.when(e == 0)
    def _():
        for cp in weight_copies(0):
            cp.start(priority=1)

        for blk in range(GATHER_SLOTS):
            idx_copy(blk).start()
        for blk in range(GATHER_SLOTS - 1):
            idx_copy(blk).wait()

        def prime(r, c):
            for blk in range(GATHER_SLOTS - 1):
                pltpu.make_async_copy(h_hbm.at[pl.ds(idx[blk, r], 1)], xbuf.at[blk, pl.ds(r, 1)],
                                      gsem.at[blk]).start()
            return c
        lax.fori_loop(0, MOE_ROWS, prime, 0, unroll=8)

    @pl.when(e + 1 < n_exp)
    def _():
        for cp in weight_copies(e + 1):
            cp.start(priority=1)

    for cp in weight_copies(e):
        cp.wait()

    @pl.when(nblk > 0)
    def _():
        wslot = e % 2
        w1b[...] = w1f[wslot].astype(BF16)
        w3b[...] = w3f[wslot].astype(BF16)
        w2b[...] = w2f[wslot].astype(BF16)

    n_piece = 4
    rows_per_piece = MOE_ROWS // n_piece
    cols_per_piece = D_MODEL // n_piece

    def block(b, c):
        blk = first + b
        slot = blk % 2
        wait_gather(blk)
        idx_copy(blk + GATHER_SLOTS - 1).wait()
        idx_copy(blk + GATHER_SLOTS).start()

        @pl.when(blk >= 2)
        def _():
            y_copy(blk - 2, slot).wait()

        xb = xbuf[blk % GATHER_SLOTS].astype(BF16)
        a = jnp.dot(xb, w1b[...], preferred_element_type=F32)
        g = jnp.dot(xb, w3b[...], preferred_element_type=F32)
        hid = (a * jax.nn.sigmoid(a) * g).astype(BF16)
        for p in range(n_piece):
            issue_rows(blk + GATHER_SLOTS - 1,
                       range(p * rows_per_piece, (p + 1) * rows_per_piece))
            cols = slice(p * cols_per_piece, (p + 1) * cols_per_piece)
            ybuf[slot, :, cols] = jnp.dot(hid, w2b[:, cols], preferred_element_type=F32)
        y_copy(blk, slot).start()
        return c

    lax.fori_loop(0, nblk, block, 0)

    @pl.when(e == n_exp - 1)
    def _():
        for ahead in range(GATHER_SLOTS - 1):
            wait_gather(total + ahead)
        idx_copy(total + GATHER_SLOTS - 1).wait()
        for back in (1, 2):
            @pl.when(total >= back)
            def _():
                y_copy(total - back, (total - back) % 2).wait()

        ybuf[0] = jnp.zeros((MOE_ROWS, D_MODEL), F32)

        def zero_block(blk, c):
            y_copy(blk, 0).start()
            y_copy(blk, 0).wait()
            return c

        lax.fori_loop(total, N_ROW_BLOCKS, zero_block, 0)


def _experts(blk_start, nblk, total, row_tok, h2, w1, w3, w2):
    grid_spec = pltpu.PrefetchScalarGridSpec(
        num_scalar_prefetch=3,
        grid=(N_EXPERTS,),
        in_specs=[pl.BlockSpec(memory_space=pl.ANY)] * 5,
        out_specs=pl.BlockSpec(memory_space=pl.ANY),
        scratch_shapes=[pltpu.SMEM((GATHER_SLOTS, MOE_ROWS), I32),
                        pltpu.VMEM((GATHER_SLOTS, MOE_ROWS, D_MODEL), F32),
                        pltpu.VMEM((2, MOE_ROWS, D_MODEL), F32),
                        pltpu.SemaphoreType.DMA((GATHER_SLOTS,)),
                        pltpu.SemaphoreType.DMA((GATHER_SLOTS,)),
                        pltpu.SemaphoreType.DMA((2,)),
                        pltpu.SemaphoreType.DMA((2,)),
                        pltpu.VMEM((2, D_MODEL, D_EXPERT), F32),
                        pltpu.VMEM((2, D_MODEL, D_EXPERT), F32),
                        pltpu.VMEM((2, D_EXPERT, D_MODEL), F32),
                        pltpu.VMEM((D_MODEL, D_EXPERT), BF16),
                        pltpu.VMEM((D_MODEL, D_EXPERT), BF16),
                        pltpu.VMEM((D_EXPERT, D_MODEL), BF16)],
    )
    return pl.pallas_call(
        _expert_kernel,
        grid_spec=grid_spec,
        out_shape=jax.ShapeDtypeStruct((N_ROWS, D_MODEL), F32),
        compiler_params=_cparams(("arbitrary",)),
        name="experts",
    )(blk_start, nblk, total, row_tok, h2, w1, w3, w2)


def _combine_kernel(y_hbm, pos_ref, pos_next_ref, h_ref, r_ref, g_ref, beta_ref, o_ref, ybuf, sem,
                    *, tc):
    i = pl.program_id(0)
    n = pl.num_programs(0)
    slot = i % 2

    def start_gather(table, slot_):
        def issue(r, c):
            for kk in range(TOP_K):
                p = table[0, 0, r * TOP_K + kk]
                pltpu.make_async_copy(y_hbm.at[pl.ds(p, 1)], ybuf.at[slot_, kk, pl.ds(r, 1)],
                                      sem.at[slot_]).start(priority=kk)
            return c
        lax.fori_loop(0, tc, issue, 0, unroll=8)

    @pl.when(i == 0)
    def _():
        start_gather(pos_ref, 0)

    @pl.when(i + 1 < n)
    def _():
        start_gather(pos_next_ref, 1 - slot)

    for kk in range(TOP_K):
        pltpu.make_async_copy(y_hbm.at[pl.ds(0, tc)], ybuf.at[slot, kk], sem.at[slot]).wait()
    gates = r_ref[...]
    moe = (ybuf[slot, 0] * gates[:, TOP_K:TOP_K + 1]
           + ybuf[slot, 1] * gates[:, TOP_K + 1:TOP_K + 2])
    o_ref[...] = _layer_norm(DN_ALPHA * h_ref[...] + moe, g_ref[...], beta_ref[...])


def _combine(pos, y_rows, h2, rout, g, b, tc):
    s = h2.shape[0]
    n = s // tc
    pos = pos.reshape(n, 1, tc * TOP_K)
    smem_block = functools.partial(pl.BlockSpec, (1, 1, tc * TOP_K), memory_space=pltpu.SMEM)
    return pl.pallas_call(
        functools.partial(_combine_kernel, tc=tc),
        grid=(n,),
        in_specs=[pl.BlockSpec(memory_space=pl.ANY),
                  smem_block(lambda i: (i, 0, 0)),
                  smem_block(lambda i: (jnp.minimum(i + 1, n - 1), 0, 0)),
                  pl.BlockSpec((tc, D_MODEL), lambda i: (i, 0)),
                  pl.BlockSpec((tc, LANES), lambda i: (i, 0)),
                  pl.BlockSpec((1, D_MODEL), lambda i: (0, 0)),
                  pl.BlockSpec((1, D_MODEL), lambda i: (0, 0))],
        out_specs=pl.BlockSpec((tc, D_MODEL), lambda i: (i, 0)),
        out_shape=jax.ShapeDtypeStruct((s, D_MODEL), F32),
        scratch_shapes=[pltpu.VMEM((2, TOP_K, tc, D_MODEL), F32),
                        pltpu.SemaphoreType.DMA((2,))],
        compiler_params=_cparams(("arbitrary",)),
        name="combine_ln3",
    )(y_rows, pos, pos, h2, rout, g, b)


def kernel(x, mem, w_in, gm_ln_g, gm_ln_b, gm_w_s, gm_b_s, w_mix_out, ln1_g, ln1_b,
           mem_w_q, mem_w_k, mem_w_v, mem_w_o, ln2_g, ln2_b,
           w_group, b_group, w_router, b_router, w1, w3, w2, ln3_g, ln3_b):
    assert x.shape == (1, SEQ, D_MODEL) and w_in.shape[0] == 1
    xs = x[0]
    l = 0
    w_mix = w_mix_out[l].astype(BF16)
    w_q = mem_w_q[l].astype(BF16)
    w_o = mem_w_o[l].astype(BF16)
    w_r = jnp.concatenate(
        [w_router[l].transpose(1, 0, 2).reshape(D_MODEL, N_EXPERTS), w_group[l],
         jnp.zeros((D_MODEL, LANES - N_EXPERTS - N_GROUPS), F32)], axis=1)
    wr_hi, wr_lo = _split_bf16(w_r)
    b_r = jnp.concatenate([b_router[l].reshape(-1), b_group[l],
                           jnp.zeros((LANES - N_EXPERTS - N_GROUPS,), F32)]).reshape(1, LANES)
    row = lambda v: v.reshape(1, D_MODEL)

    uvk, qvt = _in_proj(xs, w_in[l], tm=1024, q_scale=SB_HEAD_DIM ** -0.5 * math.log2(math.e))
    mix_a = _gmlp(uvk, gm_w_s[l], gm_ln_g[l], gm_ln_b[l], gm_b_s[l].T, tg=512)
    mix_b = _stick_breaking(qvt, uvk, tq=256, bk=128)
    h1 = _mixout(mix_a, mix_b, w_mix, xs, row(ln1_g[l]), row(ln1_b[l]), tm=512)
    k_mem = _proj(mem[0], mem_w_k[l], tm=MEM_LEN, tn=1024)
    v_mem = _proj(mem[0], mem_w_v[l], tm=MEM_LEN, tn=1024)
    h2, rout = _memattn(h1, w_q, k_mem, v_mem, w_o, row(ln2_g[l]), row(ln2_b[l]), wr_hi, wr_lo, b_r, tm=256)
    flat_e = rout[:, 0:TOP_K].astype(I32).reshape(1, N_ASSIGN)
    row_tok, pos, meta = _dispatch(flat_e)
    blk_start, nblk = meta[:, 0], meta[:, 1]
    total = (blk_start[N_EXPERTS - 1] + nblk[N_EXPERTS - 1]).reshape(1)
    y_rows = _experts(blk_start, nblk, total, row_tok, h2, w1[l], w3[l], w2[l])
    out = _combine(pos.reshape(N_ASSIGN), y_rows, h2, rout, row(ln3_g[l]), row(ln3_b[l]), tc=256)
    return out.reshape(1, SEQ, D_MODEL)
```

```python
import functools
import math

import jax
import jax.numpy as jnp
from jax import lax
from jax.experimental import pallas as pl
from jax.experimental.pallas import tpu as pltpu

D_MODEL = 2048
SEQ = 8192
CHUNK = 64
MEM_LEN = 256
GM_BLOCK = 128
GM_GROUPS = 8
GM_WIDTH = 1024
SB_HEADS = 8
SB_HEAD_DIM = 128
SB_WIDTH = 1024
MEM_HEADS = 4
MEM_HEAD_DIM = 512
N_GROUPS = 8
EXPERTS_PER_GROUP = 8
N_EXPERTS = 64
TOP_K = 2
D_EXPERT = 512
DN_ALPHA = 2.0 ** 0.25
LN_EPS = 1e-5

LANES = 128
MOE_ROWS = 128
MOE_ROWS_LOG2 = 7
N_ASSIGN = SEQ * TOP_K
N_ROW_BLOCKS = N_ASSIGN // MOE_ROWS + N_EXPERTS
N_ROWS = N_ROW_BLOCKS * MOE_ROWS
DISPATCH_CHUNK = 512
GATHER_SLOTS = 3
SB_UNDERFLOW = 160.0
SB_LOOP_BLOCKS = 1
VMEM_LIMIT = 56 * 1024 * 1024

BF16 = jnp.bfloat16
F32 = jnp.float32
I32 = jnp.int32


def _cparams(sem):
    return pltpu.CompilerParams(dimension_semantics=sem, vmem_limit_bytes=VMEM_LIMIT)


def _resident(shape, index_map):
    return pl.BlockSpec(shape, index_map, pipeline_mode=pl.Buffered(1))


def _layer_norm(y, g, b):
    mu = jnp.mean(y, axis=-1, keepdims=True)
    yc = y - mu
    var = jnp.mean(yc * yc, axis=-1, keepdims=True)
    return yc * lax.rsqrt(var + LN_EPS) * g + b


def _split_bf16(v):
    hi = v.astype(BF16)
    lo = (v - hi.astype(F32)).astype(BF16)
    return hi, lo


def _proj_kernel(x_ref, w_ref, o_ref, xb_ref):
    @pl.when(pl.program_id(1) == 0)
    def _():
        xb_ref[...] = x_ref[...].astype(BF16)

    o_ref[...] = jnp.dot(xb_ref[...], w_ref[...].astype(BF16),
                         preferred_element_type=F32).astype(o_ref.dtype)


def _proj(x, w, tm, tn):
    m, k = x.shape
    n = w.shape[1]
    return pl.pallas_call(
        _proj_kernel,
        grid=(m // tm, n // tn),
        in_specs=[pl.BlockSpec((tm, k), lambda i, j: (i, 0)),
                  pl.BlockSpec((k, tn), lambda i, j: (0, j))],
        out_specs=pl.BlockSpec((tm, tn), lambda i, j: (i, j)),
        out_shape=jax.ShapeDtypeStruct((m, n), BF16),
        scratch_shapes=[pltpu.VMEM((tm, k), BF16)],
        compiler_params=_cparams(("arbitrary", "arbitrary")),
        name="proj",
    )(x, w)


IN_BLOCKS = 5
IN_Q, IN_K, IN_V = 2, 3, 4


def _in_proj_kernel(x_ref, w_ref, o_ref, ot_ref, xb_ref, *, q_scale):
    j = pl.program_id(1)

    @pl.when(j == 0)
    def _():
        xb_ref[...] = x_ref[...].astype(BF16)

    wb = w_ref[...].astype(BF16)
    is_t = jnp.logical_or(j == IN_Q, j == IN_V)

    @pl.when(jnp.logical_not(is_t))
    def _():
        o_ref[...] = jnp.dot(xb_ref[...], wb, preferred_element_type=F32).astype(o_ref.dtype)

    @pl.when(is_t)
    def _():
        acc = lax.dot_general(wb, xb_ref[...], (((0,), (1,)), ((), ())), preferred_element_type=F32)
        ot_ref[...] = (acc * jnp.where(j == IN_Q, q_scale, 1.0).astype(F32)).astype(ot_ref.dtype)


def _in_proj(x, w, tm, q_scale):
    m, k = x.shape
    tn = SB_WIDTH
    omap = lambda i, j: (i, j - (j >= IN_Q).astype(I32) - (j >= IN_V).astype(I32))
    otmap = lambda i, j: ((j >= IN_V).astype(I32), i)
    return pl.pallas_call(
        functools.partial(_in_proj_kernel, q_scale=q_scale),
        grid=(m // tm, IN_BLOCKS),
        in_specs=[pl.BlockSpec((tm, k), lambda i, j: (i, 0)),
                  pl.BlockSpec((k, tn), lambda i, j: (0, j))],
        out_specs=[pl.BlockSpec((tm, tn), omap),
                   pl.BlockSpec((tn, tm), otmap)],
        out_shape=[jax.ShapeDtypeStruct((m, 3 * tn), BF16),
                   jax.ShapeDtypeStruct((2 * tn, m), BF16)],
        scratch_shapes=[pltpu.VMEM((tm, k), BF16)],
        compiler_params=_cparams(("arbitrary", "arbitrary")),
        name="in_proj",
    )(x, w)


def _gmlp_kernel(u_ref, v_ref, ws_ref, g_ref, b_ref, bs_ref, o_ref, *, tg):
    t_chunk = lax.broadcasted_iota(I32, (GM_BLOCK, GM_BLOCK), 0) // CHUNK
    s_chunk = lax.broadcasted_iota(I32, (GM_BLOCK, GM_BLOCK), 1) // CHUNK
    causal = t_chunk >= s_chunk
    for g in range(GM_GROUPS):
        wm = jnp.where(causal, ws_ref[g], 0.0).astype(BF16)
        ln_g = g_ref[g:g + 1, :]
        ln_b = b_ref[g:g + 1, :]
        bias = bs_ref[:, g:g + 1]
        cols = slice(g * LANES, (g + 1) * LANES)
        for n in range(tg // GM_BLOCK):
            rows = slice(n * GM_BLOCK, (n + 1) * GM_BLOCK)
            v = jax.nn.gelu(v_ref[rows, cols].astype(F32))
            vg = _layer_norm(v, ln_g, ln_b).astype(BF16)
            mixed = jnp.dot(wm, vg, preferred_element_type=F32) + bias
            u = jax.nn.gelu(u_ref[rows, cols].astype(F32))
            o_ref[rows, cols] = (u * mixed).astype(o_ref.dtype)


def _gmlp(uvk, gm_w_s, gm_ln_g, gm_ln_b, gm_b_s_t, tg):
    s = uvk.shape[0]
    return pl.pallas_call(
        functools.partial(_gmlp_kernel, tg=tg),
        grid=(s // tg,),
        in_specs=[pl.BlockSpec((tg, GM_WIDTH), lambda i: (i, 0)),
                  pl.BlockSpec((tg, GM_WIDTH), lambda i: (i, 1)),
                  pl.BlockSpec((GM_GROUPS, GM_BLOCK, GM_BLOCK), lambda i: (0, 0, 0)),
                  pl.BlockSpec((GM_GROUPS, LANES), lambda i: (0, 0)),
                  pl.BlockSpec((GM_GROUPS, LANES), lambda i: (0, 0)),
                  pl.BlockSpec((GM_BLOCK, GM_GROUPS), lambda i: (0, 0))],
        out_specs=pl.BlockSpec((tg, GM_WIDTH), lambda i: (i, 0)),
        out_shape=jax.ShapeDtypeStruct((s, GM_WIDTH), BF16),
        compiler_params=_cparams(("arbitrary",)),
        name="gmlp",
    )(uvk, uvk, gm_w_s, gm_ln_g, gm_ln_b, gm_b_s_t)


def _softplus2(u):
    return jnp.maximum(u, 0.0) + jnp.log2(1.0 + jnp.exp2(-jnp.abs(u)))


def _sb_kernel(qt_ref, k_ref, vt_ref, o_ref, *acc_refs, tq, bk):
    q0 = pl.program_id(0) * tq
    upper = (lax.broadcasted_iota(I32, (bk, bk), 1)
             >= lax.broadcasted_iota(I32, (bk, bk), 0)).astype(BF16)
    upper2 = jnp.concatenate([upper, upper], axis=1)

    heads = range(SB_HEADS)
    hrows = [slice(h * SB_HEAD_DIM, (h + 1) * SB_HEAD_DIM) for h in heads]

    def tiles(blocks, carries):
        zs, sums, masks = [], [], []
        for ks, lo, masked in blocks:
            width = tq - lo
            zs.append([jnp.dot(k_ref[pl.ds(ks, bk), hrows[h]], qt_ref[hrows[h], lo:],
                               preferred_element_type=F32) for h in heads])
            masks.append((lax.broadcasted_iota(I32, (bk, width), 0)
                          < lax.broadcasted_iota(I32, (bk, width), 1)) if masked else None)
        for zb, mask in zip(zs, masks):
            sps = [_softplus2(z) for z in zb]
            if mask is not None:
                sps = [jnp.where(mask, sp, 0.0) for sp in sps]
            sums.append([jnp.dot(upper2, jnp.concatenate(_split_bf16(sp), axis=0),
                                 preferred_element_type=F32) for sp in sps])
        csums = []
        for (ks, lo, masked), sb in zip(blocks, sums):
            cb = [sb[h] + carries[h][:, lo:] for h in heads]
            carries = [cb[h][0:1, :] if lo == 0
                       else jnp.concatenate([carries[h][:, :lo], cb[h][0:1, :]], axis=1)
                       for h in heads]
            csums.append(cb)
        probs = []
        for zb, cb, mask in zip(zs, csums, masks):
            pb = [jnp.exp2(zb[h] - cb[h]) for h in heads]
            if mask is not None:
                pb = [jnp.where(mask, a, 0.0) for a in pb]
            probs.append([a.astype(BF16) for a in pb])
        for (ks, lo, masked), pb in zip(blocks, probs):
            for h in heads:
                acc_refs[h][:, lo:] += jnp.dot(vt_ref[hrows[h], pl.ds(ks, bk)], pb[h],
                                               preferred_element_type=F32)
        return carries

    for acc_ref in acc_refs:
        acc_ref[...] = jnp.zeros_like(acc_ref)
    carries = [jnp.zeros((1, tq), F32) for _ in range(SB_HEADS)]
    carries = tiles([(pl.multiple_of(q0 + d * bk, bk), d * bk, True)
                     for d in reversed(range(tq // bk))], carries)

    def cond(state):
        j = state[0]
        lowest = functools.reduce(jnp.minimum, state[1:])
        return jnp.logical_and(j >= 0, jnp.min(lowest) < SB_UNDERFLOW)

    def body(state):
        j = state[0]
        blocks = [(pl.multiple_of((j - b) * bk, bk), 0, False) for b in range(SB_LOOP_BLOCKS)]
        return (j - SB_LOOP_BLOCKS,) + tuple(tiles(blocks, list(state[1:])))

    assert (tq // bk) % SB_LOOP_BLOCKS == 0
    lax.while_loop(cond, body, (q0 // bk - 1,) + tuple(carries))
    for h in range(SB_HEADS):
        o_ref[:, h * SB_HEAD_DIM:(h + 1) * SB_HEAD_DIM] = acc_refs[h][...].T.astype(o_ref.dtype)


def _stick_breaking(qvt, uvk, tq, bk):
    s = uvk.shape[0]
    return pl.pallas_call(
        functools.partial(_sb_kernel, tq=tq, bk=bk),
        grid=(s // tq,),
        in_specs=[pl.BlockSpec((SB_WIDTH, tq), lambda i: (0, i)),
                  _resident((s, SB_WIDTH), lambda i: (0, 2 * GM_WIDTH // SB_WIDTH)),
                  _resident((SB_WIDTH, s), lambda i: (1, 0))],
        out_specs=pl.BlockSpec((tq, SB_WIDTH), lambda i: (i, 0)),
        out_shape=jax.ShapeDtypeStruct((s, SB_WIDTH), BF16),
        scratch_shapes=[pltpu.VMEM((SB_HEAD_DIM, tq), F32) for _ in range(SB_HEADS)],
        compiler_params=_cparams(("arbitrary",)),
        name="stick_breaking",
    )(qvt, uvk, qvt)


def _mixout_kernel(a_ref, b_ref, wa_ref, wb_ref, x_ref, g_ref, beta_ref, o_ref):
    mixed = (jnp.dot(a_ref[...], wa_ref[...], preferred_element_type=F32)
             + jnp.dot(b_ref[...], wb_ref[...], preferred_element_type=F32))
    y = DN_ALPHA * x_ref[...] + mixed
    o_ref[...] = _layer_norm(y, g_ref[...], beta_ref[...])


def _mixout(mix_a, mix_b, w, x, g, b, tm):
    s = x.shape[0]
    return pl.pallas_call(
        _mixout_kernel,
        grid=(s // tm,),
        in_specs=[pl.BlockSpec((tm, GM_WIDTH), lambda i: (i, 0)),
                  pl.BlockSpec((tm, SB_WIDTH), lambda i: (i, 0)),
                  _resident((GM_WIDTH, D_MODEL), lambda i: (0, 0)),
                  _resident((SB_WIDTH, D_MODEL), lambda i: (1, 0)),
                  pl.BlockSpec((tm, D_MODEL), lambda i: (i, 0)),
                  pl.BlockSpec((1, D_MODEL), lambda i: (0, 0)),
                  pl.BlockSpec((1, D_MODEL), lambda i: (0, 0))],
        out_specs=pl.BlockSpec((tm, D_MODEL), lambda i: (i, 0)),
        out_shape=jax.ShapeDtypeStruct((s, D_MODEL), F32),
        compiler_params=_cparams(("arbitrary",)),
        name="mixout_ln1",
    )(mix_a, mix_b, w, w, x, g, b)


def _route(logits):
    lane = lax.broadcasted_iota(I32, logits.shape, 1)
    neg = jnp.float32(-jnp.inf)
    big = jnp.int32(1 << 20)

    def first_argmax(vals):
        m = jnp.max(vals, axis=-1, keepdims=True)
        idx = jnp.min(jnp.where(vals == m, lane, big), axis=-1, keepdims=True)
        return m, idx

    is_group = jnp.logical_and(lane >= N_EXPERTS, lane < N_EXPERTS + N_GROUPS)
    gl = jnp.where(is_group, logits, neg)
    gmax, glane = first_argmax(gl)
    g_val = 1.0 / jnp.sum(jnp.exp(gl - gmax), axis=-1, keepdims=True)
    gidx = glane - N_EXPERTS
    in_group = jnp.logical_and(lane >= gidx * EXPERTS_PER_GROUP,
                               lane < (gidx + 1) * EXPERTS_PER_GROUP)
    el = jnp.where(in_group, logits, neg)
    m1, i1 = first_argmax(el)
    m2, i2 = first_argmax(jnp.where(lane == i1, neg, el))
    e2 = jnp.exp(m2 - m1)
    gate1 = g_val / (1.0 + e2)
    gate2 = g_val * e2 / (1.0 + e2)
    return jnp.where(lane == 0, i1.astype(F32),
                     jnp.where(lane == 1, i2.astype(F32),
                               jnp.where(lane == 2, gate1,
                                         jnp.where(lane == 3, gate2, 0.0))))


def _memattn_kernel(h_ref, wq_ref, km_ref, vm_ref, wo_ref, g_ref, beta_ref, wrh_ref, wrl_ref, br_ref,
                    o_ref, r_ref):
    h = h_ref[...]
    q = jnp.dot(h.astype(BF16), wq_ref[...], preferred_element_type=F32).astype(BF16)
    outs = []
    for hd in range(MEM_HEADS):
        cols = slice(hd * MEM_HEAD_DIM, (hd + 1) * MEM_HEAD_DIM)
        kh = km_ref[:, cols]
        vh = vm_ref[:, cols]
        s = lax.dot_general(q[:, cols], kh, (((1,), (1,)), ((), ())),
                            preferred_element_type=F32) * (MEM_HEAD_DIM ** -0.5)
        s = s - jnp.max(s, axis=-1, keepdims=True)
        e = jnp.exp(s)
        p = e / jnp.sum(e, axis=-1, keepdims=True)
        outs.append(jnp.dot(p.astype(BF16), vh, preferred_element_type=F32).astype(BF16))
    o = jnp.concatenate(outs, axis=-1)
    attn = jnp.dot(o, wo_ref[...], preferred_element_type=F32)
    h2 = _layer_norm(DN_ALPHA * h + attn, g_ref[...], beta_ref[...])
    o_ref[...] = h2
    h_hi, h_lo = _split_bf16(h2)
    logits = (jnp.dot(h_hi, wrh_ref[...], preferred_element_type=F32)
              + jnp.dot(h_lo, wrh_ref[...], preferred_element_type=F32)
              + jnp.dot(h_hi, wrl_ref[...], preferred_element_type=F32)) + br_ref[...]
    r_ref[...] = _route(logits)


def _memattn(h1, wq, k_mem, v_mem, wo, g, b, wr_hi, wr_lo, b_r, tm):
    s = h1.shape[0]
    return pl.pallas_call(
        _memattn_kernel,
        grid=(s // tm,),
        in_specs=[pl.BlockSpec((tm, D_MODEL), lambda i: (i, 0)),
                  _resident((D_MODEL, D_MODEL), lambda i: (0, 0)),
                  _resident((MEM_LEN, D_MODEL), lambda i: (0, 0)),
                  _resident((MEM_LEN, D_MODEL), lambda i: (0, 0)),
                  _resident((D_MODEL, D_MODEL), lambda i: (0, 0)),
                  pl.BlockSpec((1, D_MODEL), lambda i: (0, 0)),
                  pl.BlockSpec((1, D_MODEL), lambda i: (0, 0)),
                  _resident((D_MODEL, LANES), lambda i: (0, 0)),
                  _resident((D_MODEL, LANES), lambda i: (0, 0)),
                  pl.BlockSpec((1, LANES), lambda i: (0, 0))],
        out_specs=[pl.BlockSpec((tm, D_MODEL), lambda i: (i, 0)),
                   pl.BlockSpec((tm, LANES), lambda i: (i, 0))],
        out_shape=[jax.ShapeDtypeStruct((s, D_MODEL), F32),
                   jax.ShapeDtypeStruct((s, LANES), F32)],
        compiler_params=_cparams(("arbitrary",)),
        name="memattn_ln2_router",
    )(h1, wq, k_mem, v_mem, wo, g, b, wr_hi, wr_lo, b_r)


def _dispatch_kernel(e_ref, tok_ref, pos_ref, meta_ref):
    ca = DISPATCH_CHUNK
    n_chunks = N_ASSIGN // ca
    e_iota = lax.broadcasted_iota(I32, (N_EXPERTS, ca), 0)

    def onehot(c):
        e_c = e_ref[:, pl.ds(pl.multiple_of(c * ca, ca), ca)]
        return (e_iota == e_c).astype(F32)

    counts = lax.fori_loop(
        0, n_chunks, lambda c, acc: acc + jnp.sum(onehot(c), axis=1, keepdims=True),
        jnp.zeros((N_EXPERTS, 1), F32))
    nblk = jnp.right_shift(counts.astype(I32) + (MOE_ROWS - 1), MOE_ROWS_LOG2)
    strict_lower = (lax.broadcasted_iota(I32, (N_EXPERTS, N_EXPERTS), 1)
                    < lax.broadcasted_iota(I32, (N_EXPERTS, N_EXPERTS), 0)).astype(BF16)
    nblk_lanes = jnp.broadcast_to(nblk.astype(F32), (N_EXPERTS, LANES)).astype(BF16)
    blk_start = jnp.dot(strict_lower, nblk_lanes, preferred_element_type=F32)
    pad_start = blk_start[:, 0:1] * MOE_ROWS
    lane = lax.broadcasted_iota(I32, (N_EXPERTS, LANES), 1)
    meta_ref[...] = jnp.where(lane == 0, blk_start.astype(I32),
                              jnp.where(lane == 1, jnp.broadcast_to(nblk, (N_EXPERTS, LANES)), 0))

    earlier = (lax.broadcasted_iota(I32, (ca, ca), 0)
               < lax.broadcasted_iota(I32, (ca, ca), 1)).astype(BF16)
    blk_iota = lax.broadcasted_iota(I32, (N_ROW_BLOCKS, ca), 0)
    row_iota = lax.broadcasted_iota(I32, (MOE_ROWS, ca), 0)
    a_iota = lax.broadcasted_iota(I32, (1, ca), 1)

    def body(c, state):
        seen, acc_hi, acc_lo = state
        oh = onehot(c)
        before = jnp.dot(oh.astype(BF16), earlier, preferred_element_type=F32) + seen
        dest = jnp.sum(oh * (before + pad_start), axis=0, keepdims=True).astype(I32)
        pos_ref[:, pl.ds(pl.multiple_of(c * ca, ca), ca)] = dest
        in_blk = (blk_iota == jnp.right_shift(dest, MOE_ROWS_LOG2)).astype(F32)
        in_row = (row_iota == jnp.bitwise_and(dest, MOE_ROWS - 1)).astype(BF16)
        tok = jnp.right_shift(c * ca + a_iota, 1)
        tok_hi = (jnp.right_shift(tok, 7) + 1).astype(F32)
        tok_lo = jnp.bitwise_and(tok, 127).astype(F32)
        nt = (((1,), (1,)), ((), ()))
        acc_hi = acc_hi + lax.dot_general((in_blk * tok_hi).astype(BF16), in_row, nt,
                                          preferred_element_type=F32)
        acc_lo = acc_lo + lax.dot_general((in_blk * tok_lo).astype(BF16), in_row, nt,
                                          preferred_element_type=F32)
        return seen + jnp.sum(oh, axis=1, keepdims=True), acc_hi, acc_lo

    zeros = jnp.zeros((N_ROW_BLOCKS, MOE_ROWS), F32)
    _, acc_hi, acc_lo = lax.fori_loop(0, n_chunks, body,
                                      (jnp.zeros((N_EXPERTS, 1), F32), zeros, zeros))
    pad_tok = jnp.bitwise_and(
        lax.broadcasted_iota(I32, (N_ROW_BLOCKS, MOE_ROWS), 0) * MOE_ROWS
        + lax.broadcasted_iota(I32, (N_ROW_BLOCKS, MOE_ROWS), 1), SEQ - 1)
    tok_ref[...] = jnp.where(acc_hi > 0.5, ((acc_hi - 1.0) * 128.0 + acc_lo).astype(I32), pad_tok)


def _dispatch(flat_e):
    return pl.pallas_call(
        _dispatch_kernel,
        out_shape=[jax.ShapeDtypeStruct((N_ROW_BLOCKS, MOE_ROWS), I32),
                   jax.ShapeDtypeStruct((1, N_ASSIGN), I32),
                   jax.ShapeDtypeStruct((N_EXPERTS, LANES), I32)],
        compiler_params=pltpu.CompilerParams(vmem_limit_bytes=VMEM_LIMIT),
        name="moe_dispatch",
    )(flat_e)


def _expert_kernel(bs_ref, nb_ref, tot_ref, tok_hbm, h_hbm, w1_hbm, w3_hbm, w2_hbm, y_hbm,
                   idx, xbuf, ybuf, isem, gsem, ysem, wsem, w1f, w3f, w2f, w1b, w3b, w2b):
    e = pl.program_id(0)
    n_exp = pl.num_programs(0)
    total = tot_ref[0]
    first = bs_ref[e]
    nblk = nb_ref[e]

    def idx_copy(blk):
        row = jnp.minimum(blk, N_ROW_BLOCKS - 1)
        slot = blk % GATHER_SLOTS
        return pltpu.make_async_copy(tok_hbm.at[pl.ds(row, 1)], idx.at[pl.ds(slot, 1)],
                                     isem.at[slot])

    def issue_rows(blk, rows):
        slot = blk % GATHER_SLOTS
        for r in rows:
            tok = idx[slot, r]
            pltpu.make_async_copy(h_hbm.at[pl.ds(tok, 1)], xbuf.at[slot, pl.ds(r, 1)],
                                  gsem.at[slot]).start()

    def wait_gather(blk):
        slot = blk % GATHER_SLOTS
        pltpu.make_async_copy(h_hbm.at[pl.ds(0, MOE_ROWS)], xbuf.at[slot], gsem.at[slot]).wait()

    def y_copy(blk, slot):
        return pltpu.make_async_copy(ybuf.at[slot], y_hbm.at[pl.ds(blk * MOE_ROWS, MOE_ROWS)],
                                     ysem.at[slot])

    def weight_copies(ex):
        slot = ex % 2
        return [pltpu.make_async_copy(src.at[ex], dst.at[slot], wsem.at[slot])
                for src, dst in ((w1_hbm, w1f), (w3_hbm, w3f), (w2_hbm, w2f))]

    @pl.when(e == 0)
    def _():
        for cp in weight_copies(0):
            cp.start(priority=1)

        for blk in range(GATHER_SLOTS):
            idx_copy(blk).start()
        for blk in range(GATHER_SLOTS - 1):
            idx_copy(blk).wait()

        def prime(r, c):
            for blk in range(GATHER_SLOTS - 1):
                pltpu.make_async_copy(h_hbm.at[pl.ds(idx[blk, r], 1)], xbuf.at[blk, pl.ds(r, 1)],
                                      gsem.at[blk]).start()
            return c
        lax.fori_loop(0, MOE_ROWS, prime, 0, unroll=8)

    @pl.when(e + 1 < n_exp)
    def _():
        for cp in weight_copies(e + 1):
            cp.start(priority=1)

    for cp in weight_copies(e):
        cp.wait()

    @pl.when(nblk > 0)
    def _():
        wslot = e % 2
        w1b[...] = w1f[wslot].astype(BF16)
        w3b[...] = w3f[wslot].astype(BF16)
        w2b[...] = w2f[wslot].astype(BF16)

    n_piece = 4
    rows_per_piece = MOE_ROWS // n_piece
    cols_per_piece = D_MODEL // n_piece

    def block(b, c):
        blk = first + b
        slot = blk % 2
        wait_gather(blk)
        idx_copy(blk + GATHER_SLOTS - 1).wait()
        idx_copy(blk + GATHER_SLOTS).start()

        @pl.when(blk >= 2)
        def _():
            y_copy(blk - 2, slot).wait()

        xb = xbuf[blk % GATHER_SLOTS].astype(BF16)
        a = jnp.dot(xb, w1b[...], preferred_element_type=F32)
        g = jnp.dot(xb, w3b[...], preferred_element_type=F32)
        hid = (a * jax.nn.sigmoid(a) * g).astype(BF16)
        for p in range(n_piece):
            issue_rows(blk + GATHER_SLOTS - 1,
                       range(p * rows_per_piece, (p + 1) * rows_per_piece))
            cols = slice(p * cols_per_piece, (p + 1) * cols_per_piece)
            ybuf[slot, :, cols] = jnp.dot(hid, w2b[:, cols], preferred_element_type=F32)
        y_copy(blk, slot).start()
        return c

    lax.fori_loop(0, nblk, block, 0)

    @pl.when(e == n_exp - 1)
    def _():
        for ahead in range(GATHER_SLOTS - 1):
            wait_gather(total + ahead)
        idx_copy(total + GATHER_SLOTS - 1).wait()
        for back in (1, 2):
            @pl.when(total >= back)
            def _():
                y_copy(total - back, (total - back) % 2).wait()

        ybuf[0] = jnp.zeros((MOE_ROWS, D_MODEL), F32)

        def zero_block(blk, c):
            y_copy(blk, 0).start()
            y_copy(blk, 0).wait()
            return c

        lax.fori_loop(total, N_ROW_BLOCKS, zero_block, 0)


def _experts(blk_start, nblk, total, row_tok, h2, w1, w3, w2):
    grid_spec = pltpu.PrefetchScalarGridSpec(
        num_scalar_prefetch=3,
        grid=(N_EXPERTS,),
        in_specs=[pl.BlockSpec(memory_space=pl.ANY)] * 5,
        out_specs=pl.BlockSpec(memory_space=pl.ANY),
        scratch_shapes=[pltpu.SMEM((GATHER_SLOTS, MOE_ROWS), I32),
                        pltpu.VMEM((GATHER_SLOTS, MOE_ROWS, D_MODEL), F32),
                        pltpu.VMEM((2, MOE_ROWS, D_MODEL), F32),
                        pltpu.SemaphoreType.DMA((GATHER_SLOTS,)),
                        pltpu.SemaphoreType.DMA((GATHER_SLOTS,)),
                        pltpu.SemaphoreType.DMA((2,)),
                        pltpu.SemaphoreType.DMA((2,)),
                        pltpu.VMEM((2, D_MODEL, D_EXPERT), F32),
                        pltpu.VMEM((2, D_MODEL, D_EXPERT), F32),
                        pltpu.VMEM((2, D_EXPERT, D_MODEL), F32),
                        pltpu.VMEM((D_MODEL, D_EXPERT), BF16),
                        pltpu.VMEM((D_MODEL, D_EXPERT), BF16),
                        pltpu.VMEM((D_EXPERT, D_MODEL), BF16)],
    )
    return pl.pallas_call(
        _expert_kernel,
        grid_spec=grid_spec,
        out_shape=jax.ShapeDtypeStruct((N_ROWS, D_MODEL), F32),
        compiler_params=_cparams(("arbitrary",)),
        name="experts",
    )(blk_start, nblk, total, row_tok, h2, w1, w3, w2)


def _combine_kernel(y_hbm, pos_ref, pos_next_ref, h_ref, r_ref, g_ref, beta_ref, o_ref, ybuf, sem,
                    *, tc):
    i = pl.program_id(0)
    n = pl.num_programs(0)
    slot = i % 2

    def start_gather(table, slot_):
        def issue(r, c):
            for kk in range(TOP_K):
                p = table[0, 0, r * TOP_K + kk]
                pltpu.make_async_copy(y_hbm.at[pl.ds(p, 1)], ybuf.at[slot_, kk, pl.ds(r, 1)],
                                      sem.at[slot_]).start(priority=kk)
            return c
        lax.fori_loop(0, tc, issue, 0, unroll=8)

    @pl.when(i == 0)
    def _():
        start_gather(pos_ref, 0)

    @pl.when(i + 1 < n)
    def _():
        start_gather(pos_next_ref, 1 - slot)

    for kk in range(TOP_K):
        pltpu.make_async_copy(y_hbm.at[pl.ds(0, tc)], ybuf.at[slot, kk], sem.at[slot]).wait()
    gates = r_ref[...]
    moe = (ybuf[slot, 0] * gates[:, TOP_K:TOP_K + 1]
           + ybuf[slot, 1] * gates[:, TOP_K + 1:TOP_K + 2])
    o_ref[...] = _layer_norm(DN_ALPHA * h_ref[...] + moe, g_ref[...], beta_ref[...])


def _combine(pos, y_rows, h2, rout, g, b, tc):
    s = h2.shape[0]
    n = s // tc
    pos = pos.reshape(n, 1, tc * TOP_K)
    smem_block = functools.partial(pl.BlockSpec, (1, 1, tc * TOP_K), memory_space=pltpu.SMEM)
    return pl.pallas_call(
        functools.partial(_combine_kernel, tc=tc),
        grid=(n,),
        in_specs=[pl.BlockSpec(memory_space=pl.ANY),
                  smem_block(lambda i: (i, 0, 0)),
                  smem_block(lambda i: (jnp.minimum(i + 1, n - 1), 0, 0)),
                  pl.BlockSpec((tc, D_MODEL), lambda i: (i, 0)),
                  pl.BlockSpec((tc, LANES), lambda i: (i, 0)),
                  pl.BlockSpec((1, D_MODEL), lambda i: (0, 0)),
                  pl.BlockSpec((1, D_MODEL), lambda i: (0, 0))],
        out_specs=pl.BlockSpec((tc, D_MODEL), lambda i: (i, 0)),
        out_shape=jax.ShapeDtypeStruct((s, D_MODEL), F32),
        scratch_shapes=[pltpu.VMEM((2, TOP_K, tc, D_MODEL), F32),
                        pltpu.SemaphoreType.DMA((2,))],
        compiler_params=_cparams(("arbitrary",)),
        name="combine_ln3",
    )(y_rows, pos, pos, h2, rout, g, b)


def kernel(x, mem, w_in, gm_ln_g, gm_ln_b, gm_w_s, gm_b_s, w_mix_out, ln1_g, ln1_b,
           mem_w_q, mem_w_k, mem_w_v, mem_w_o, ln2_g, ln2_b,
           w_group, b_group, w_router, b_router, w1, w3, w2, ln3_g, ln3_b):
    assert x.shape == (1, SEQ, D_MODEL) and w_in.shape[0] == 1
    xs = x[0]
    l = 0
    w_mix = w_mix_out[l].astype(BF16)
    w_q = mem_w_q[l].astype(BF16)
    w_o = mem_w_o[l].astype(BF16)
    w_r = jnp.concatenate(
        [w_router[l].transpose(1, 0, 2).reshape(D_MODEL, N_EXPERTS), w_group[l],
         jnp.zeros((D_MODEL, LANES - N_EXPERTS - N_GROUPS), F32)], axis=1)
    wr_hi, wr_lo = _split_bf16(w_r)
    b_r = jnp.concatenate([b_router[l].reshape(-1), b_group[l],
                           jnp.zeros((LANES - N_EXPERTS - N_GROUPS,), F32)]).reshape(1, LANES)
    row = lambda v: v.reshape(1, D_MODEL)

    uvk, qvt = _in_proj(xs, w_in[l], tm=1024, q_scale=SB_HEAD_DIM ** -0.5 * math.log2(math.e))
    mix_a = _gmlp(uvk, gm_w_s[l], gm_ln_g[l], gm_ln_b[l], gm_b_s[l].T, tg=512)
    mix_b = _stick_breaking(qvt, uvk, tq=256, bk=128)
    h1 = _mixout(mix_a, mix_b, w_mix, xs, row(ln1_g[l]), row(ln1_b[l]), tm=512)
    k_mem = _proj(mem[0], mem_w_k[l], tm=MEM_LEN, tn=1024)
    v_mem = _proj(mem[0], mem_w_v[l], tm=MEM_LEN, tn=1024)
    h2, rout = _memattn(h1, w_q, k_mem, v_mem, w_o, row(ln2_g[l]), row(ln2_b[l]), wr_hi, wr_lo, b_r, tm=512)
    flat_e = rout[:, 0:TOP_K].astype(I32).reshape(1, N_ASSIGN)
    row_tok, pos, meta = _dispatch(flat_e)
    blk_start, nblk = meta[:, 0], meta[:, 1]
    total = (blk_start[N_EXPERTS - 1] + nblk[N_EXPERTS - 1]).reshape(1)
    y_rows = _experts(blk_start, nblk, total, row_tok, h2, w1[l], w3[l], w2[l])
    out = _combine(pos.reshape(N_ASSIGN), y_rows, h2, rout, row(ln3_g[l]), row(ln3_b[l]), tc=256)
    return out.reshape(1, SEQ, D_MODEL)
```

```python
import functools
import math

import jax
import jax.numpy as jnp
from jax import lax
from jax.experimental import pallas as pl
from jax.experimental.pallas import tpu as pltpu

D_MODEL = 2048
SEQ = 8192
CHUNK = 64
MEM_LEN = 256
GM_BLOCK = 128
GM_GROUPS = 8
GM_WIDTH = 1024
SB_HEADS = 8
SB_HEAD_DIM = 128
SB_WIDTH = 1024
MEM_HEADS = 4
MEM_HEAD_DIM = 512
N_GROUPS = 8
EXPERTS_PER_GROUP = 8
N_EXPERTS = 64
TOP_K = 2
D_EXPERT = 512
DN_ALPHA = 2.0 ** 0.25
LN_EPS = 1e-5

LANES = 128
MOE_ROWS = 128
MOE_ROWS_LOG2 = 7
N_ASSIGN = SEQ * TOP_K
N_ROW_BLOCKS = N_ASSIGN // MOE_ROWS + N_EXPERTS
N_ROWS = N_ROW_BLOCKS * MOE_ROWS
DISPATCH_CHUNK = 512
GATHER_SLOTS = 3
SB_UNDERFLOW = 160.0
SB_LOOP_BLOCKS = 1
SB_STATIC_BELOW = 2
VMEM_LIMIT = 56 * 1024 * 1024

BF16 = jnp.bfloat16
F32 = jnp.float32
I32 = jnp.int32


def _cparams(sem):
    return pltpu.CompilerParams(dimension_semantics=sem, vmem_limit_bytes=VMEM_LIMIT)


def _resident(shape, index_map):
    return pl.BlockSpec(shape, index_map, pipeline_mode=pl.Buffered(1))


def _layer_norm(y, g, b):
    mu = jnp.mean(y, axis=-1, keepdims=True)
    yc = y - mu
    var = jnp.mean(yc * yc, axis=-1, keepdims=True)
    return yc * lax.rsqrt(var + LN_EPS) * g + b


def _split_bf16(v):
    hi = v.astype(BF16)
    lo = (v - hi.astype(F32)).astype(BF16)
    return hi, lo


def _proj_kernel(x_ref, w_ref, o_ref, xb_ref):
    @pl.when(pl.program_id(1) == 0)
    def _():
        xb_ref[...] = x_ref[...].astype(BF16)

    o_ref[...] = jnp.dot(xb_ref[...], w_ref[...].astype(BF16),
                         preferred_element_type=F32).astype(o_ref.dtype)


def _proj(x, w, tm, tn):
    m, k = x.shape
    n = w.shape[1]
    return pl.pallas_call(
        _proj_kernel,
        grid=(m // tm, n // tn),
        in_specs=[pl.BlockSpec((tm, k), lambda i, j: (i, 0)),
                  pl.BlockSpec((k, tn), lambda i, j: (0, j))],
        out_specs=pl.BlockSpec((tm, tn), lambda i, j: (i, j)),
        out_shape=jax.ShapeDtypeStruct((m, n), BF16),
        scratch_shapes=[pltpu.VMEM((tm, k), BF16)],
        compiler_params=_cparams(("arbitrary", "arbitrary")),
        name="proj",
    )(x, w)


IN_BLOCKS = 5
IN_Q, IN_K, IN_V = 2, 3, 4


def _in_proj_kernel(x_ref, w_ref, o_ref, ot_ref, xb_ref, *, q_scale):
    j = pl.program_id(1)

    @pl.when(j == 0)
    def _():
        xb_ref[...] = x_ref[...].astype(BF16)

    wb = w_ref[...].astype(BF16)
    is_t = jnp.logical_or(j == IN_Q, j == IN_V)

    @pl.when(jnp.logical_not(is_t))
    def _():
        o_ref[...] = jnp.dot(xb_ref[...], wb, preferred_element_type=F32).astype(o_ref.dtype)

    @pl.when(is_t)
    def _():
        acc = lax.dot_general(wb, xb_ref[...], (((0,), (1,)), ((), ())), preferred_element_type=F32)
        ot_ref[...] = (acc * jnp.where(j == IN_Q, q_scale, 1.0).astype(F32)).astype(ot_ref.dtype)


def _in_proj(x, w, tm, q_scale):
    m, k = x.shape
    tn = SB_WIDTH
    omap = lambda i, j: (i, j - (j >= IN_Q).astype(I32) - (j >= IN_V).astype(I32))
    otmap = lambda i, j: ((j >= IN_V).astype(I32), i)
    return pl.pallas_call(
        functools.partial(_in_proj_kernel, q_scale=q_scale),
        grid=(m // tm, IN_BLOCKS),
        in_specs=[pl.BlockSpec((tm, k), lambda i, j: (i, 0)),
                  pl.BlockSpec((k, tn), lambda i, j: (0, j))],
        out_specs=[pl.BlockSpec((tm, tn), omap),
                   pl.BlockSpec((tn, tm), otmap)],
        out_shape=[jax.ShapeDtypeStruct((m, 3 * tn), BF16),
                   jax.ShapeDtypeStruct((2 * tn, m), BF16)],
        scratch_shapes=[pltpu.VMEM((tm, k), BF16)],
        compiler_params=_cparams(("arbitrary", "arbitrary")),
        name="in_proj",
    )(x, w)


def _gmlp_kernel(u_ref, v_ref, ws_ref, g_ref, b_ref, bs_ref, o_ref, *, tg):
    t_chunk = lax.broadcasted_iota(I32, (GM_BLOCK, GM_BLOCK), 0) // CHUNK
    s_chunk = lax.broadcasted_iota(I32, (GM_BLOCK, GM_BLOCK), 1) // CHUNK
    causal = t_chunk >= s_chunk
    for g in range(GM_GROUPS):
        wm = jnp.where(causal, ws_ref[g], 0.0).astype(BF16)
        ln_g = g_ref[g:g + 1, :]
        ln_b = b_ref[g:g + 1, :]
        bias = bs_ref[:, g:g + 1]
        cols = slice(g * LANES, (g + 1) * LANES)
        for n in range(tg // GM_BLOCK):
            rows = slice(n * GM_BLOCK, (n + 1) * GM_BLOCK)
            v = jax.nn.gelu(v_ref[rows, cols].astype(F32))
            vg = _layer_norm(v, ln_g, ln_b).astype(BF16)
            mixed = jnp.dot(wm, vg, preferred_element_type=F32) + bias
            u = jax.nn.gelu(u_ref[rows, cols].astype(F32))
            o_ref[rows, cols] = (u * mixed).astype(o_ref.dtype)


def _gmlp(uvk, gm_w_s, gm_ln_g, gm_ln_b, gm_b_s_t, tg):
    s = uvk.shape[0]
    return pl.pallas_call(
        functools.partial(_gmlp_kernel, tg=tg),
        grid=(s // tg,),
        in_specs=[pl.BlockSpec((tg, GM_WIDTH), lambda i: (i, 0)),
                  pl.BlockSpec((tg, GM_WIDTH), lambda i: (i, 1)),
                  pl.BlockSpec((GM_GROUPS, GM_BLOCK, GM_BLOCK), lambda i: (0, 0, 0)),
                  pl.BlockSpec((GM_GROUPS, LANES), lambda i: (0, 0)),
                  pl.BlockSpec((GM_GROUPS, LANES), lambda i: (0, 0)),
                  pl.BlockSpec((GM_BLOCK, GM_GROUPS), lambda i: (0, 0))],
        out_specs=pl.BlockSpec((tg, GM_WIDTH), lambda i: (i, 0)),
        out_shape=jax.ShapeDtypeStruct((s, GM_WIDTH), BF16),
        compiler_params=_cparams(("arbitrary",)),
        name="gmlp",
    )(uvk, uvk, gm_w_s, gm_ln_g, gm_ln_b, gm_b_s_t)


def _softplus2(u):
    return jnp.maximum(u, 0.0) + jnp.log2(1.0 + jnp.exp2(-jnp.abs(u)))


def _sb_kernel(qt_ref, k_ref, vt_ref, o_ref, *acc_refs, tq, bk):
    q0 = pl.program_id(0) * tq
    upper = (lax.broadcasted_iota(I32, (bk, bk), 1)
             >= lax.broadcasted_iota(I32, (bk, bk), 0)).astype(BF16)
    upper2 = jnp.concatenate([upper, upper], axis=1)

    heads = range(SB_HEADS)
    hrows = [slice(h * SB_HEAD_DIM, (h + 1) * SB_HEAD_DIM) for h in heads]

    def tiles(blocks, carries):
        zs, sums, masks = [], [], []
        for ks, lo, masked in blocks:
            width = tq - lo
            zs.append([jnp.dot(k_ref[pl.ds(ks, bk), hrows[h]], qt_ref[hrows[h], lo:],
                               preferred_element_type=F32) for h in heads])
            if masked is True:
                masks.append(lax.broadcasted_iota(I32, (bk, width), 0)
                             < lax.broadcasted_iota(I32, (bk, width), 1))
            elif masked is False:
                masks.append(None)
            else:
                masks.append(jnp.broadcast_to(masked, (bk, width)))
        for zb, mask in zip(zs, masks):
            sps = [_softplus2(z) for z in zb]
            if mask is not None:
                sps = [jnp.where(mask, sp, 0.0) for sp in sps]
            sums.append([jnp.dot(upper2, jnp.concatenate(_split_bf16(sp), axis=0),
                                 preferred_element_type=F32) for sp in sps])
        csums = []
        for (ks, lo, masked), sb in zip(blocks, sums):
            cb = [sb[h] + carries[h][:, lo:] for h in heads]
            carries = [cb[h][0:1, :] if lo == 0
                       else jnp.concatenate([carries[h][:, :lo], cb[h][0:1, :]], axis=1)
                       for h in heads]
            csums.append(cb)
        probs = []
        for zb, cb, mask in zip(zs, csums, masks):
            pb = [jnp.exp2(zb[h] - cb[h]) for h in heads]
            if mask is not None:
                pb = [jnp.where(mask, a, 0.0) for a in pb]
            probs.append([a.astype(BF16) for a in pb])
        for (ks, lo, masked), pb in zip(blocks, probs):
            for h in heads:
                acc_refs[h][:, lo:] += jnp.dot(vt_ref[hrows[h], pl.ds(ks, bk)], pb[h],
                                               preferred_element_type=F32)
        return carries

    for acc_ref in acc_refs:
        acc_ref[...] = jnp.zeros_like(acc_ref)
    carries = [jnp.zeros((1, tq), F32) for _ in range(SB_HEADS)]
    blocks = [(pl.multiple_of(q0 + d * bk, bk), d * bk, True) for d in reversed(range(tq // bk))]
    for b in range(1, SB_STATIC_BELOW + 1):
        below = q0 // bk - b
        blocks.append((pl.multiple_of(jnp.maximum(below, 0) * bk, bk), 0, below >= 0))
    carries = tiles(blocks, carries)

    def cond(state):
        j = state[0]
        lowest = functools.reduce(jnp.minimum, state[1:])
        return jnp.logical_and(j >= 0, jnp.min(lowest) < SB_UNDERFLOW)

    def body(state):
        j = state[0]
        blocks = [(pl.multiple_of((j - b) * bk, bk), 0, False) for b in range(SB_LOOP_BLOCKS)]
        return (j - SB_LOOP_BLOCKS,) + tuple(tiles(blocks, list(state[1:])))

    assert (tq // bk) % SB_LOOP_BLOCKS == 0
    lax.while_loop(cond, body, (q0 // bk - 1 - SB_STATIC_BELOW,) + tuple(carries))
    for h in range(SB_HEADS):
        o_ref[:, h * SB_HEAD_DIM:(h + 1) * SB_HEAD_DIM] = acc_refs[h][...].T.astype(o_ref.dtype)


def _stick_breaking(qvt, uvk, tq, bk):
    s = uvk.shape[0]
    return pl.pallas_call(
        functools.partial(_sb_kernel, tq=tq, bk=bk),
        grid=(s // tq,),
        in_specs=[pl.BlockSpec((SB_WIDTH, tq), lambda i: (0, i)),
                  _resident((s, SB_WIDTH), lambda i: (0, 2 * GM_WIDTH // SB_WIDTH)),
                  _resident((SB_WIDTH, s), lambda i: (1, 0))],
        out_specs=pl.BlockSpec((tq, SB_WIDTH), lambda i: (i, 0)),
        out_shape=jax.ShapeDtypeStruct((s, SB_WIDTH), BF16),
        scratch_shapes=[pltpu.VMEM((SB_HEAD_DIM, tq), F32) for _ in range(SB_HEADS)],
        compiler_params=_cparams(("arbitrary",)),
        name="stick_breaking",
    )(qvt, uvk, qvt)


def _mixout_kernel(a_ref, b_ref, wa_ref, wb_ref, x_ref, g_ref, beta_ref, o_ref):
    mixed = (jnp.dot(a_ref[...], wa_ref[...], preferred_element_type=F32)
             + jnp.dot(b_ref[...], wb_ref[...], preferred_element_type=F32))
    y = DN_ALPHA * x_ref[...] + mixed
    o_ref[...] = _layer_norm(y, g_ref[...], beta_ref[...])


def _mixout(mix_a, mix_b, w, x, g, b, tm):
    s = x.shape[0]
    return pl.pallas_call(
        _mixout_kernel,
        grid=(s // tm,),
        in_specs=[pl.BlockSpec((tm, GM_WIDTH), lambda i: (i, 0)),
                  pl.BlockSpec((tm, SB_WIDTH), lambda i: (i, 0)),
                  _resident((GM_WIDTH, D_MODEL), lambda i: (0, 0)),
                  _resident((SB_WIDTH, D_MODEL), lambda i: (1, 0)),
                  pl.BlockSpec((tm, D_MODEL), lambda i: (i, 0)),
                  pl.BlockSpec((1, D_MODEL), lambda i: (0, 0)),
                  pl.BlockSpec((1, D_MODEL), lambda i: (0, 0))],
        out_specs=pl.BlockSpec((tm, D_MODEL), lambda i: (i, 0)),
        out_shape=jax.ShapeDtypeStruct((s, D_MODEL), F32),
        compiler_params=_cparams(("arbitrary",)),
        name="mixout_ln1",
    )(mix_a, mix_b, w, w, x, g, b)


def _route(logits):
    lane = lax.broadcasted_iota(I32, logits.shape, 1)
    neg = jnp.float32(-jnp.inf)
    big = jnp.int32(1 << 20)

    def first_argmax(vals):
        m = jnp.max(vals, axis=-1, keepdims=True)
        idx = jnp.min(jnp.where(vals == m, lane, big), axis=-1, keepdims=True)
        return m, idx

    is_group = jnp.logical_and(lane >= N_EXPERTS, lane < N_EXPERTS + N_GROUPS)
    gl = jnp.where(is_group, logits, neg)
    gmax, glane = first_argmax(gl)
    g_val = 1.0 / jnp.sum(jnp.exp(gl - gmax), axis=-1, keepdims=True)
    gidx = glane - N_EXPERTS
    in_group = jnp.logical_and(lane >= gidx * EXPERTS_PER_GROUP,
                               lane < (gidx + 1) * EXPERTS_PER_GROUP)
    el = jnp.where(in_group, logits, neg)
    m1, i1 = first_argmax(el)
    m2, i2 = first_argmax(jnp.where(lane == i1, neg, el))
    e2 = jnp.exp(m2 - m1)
    gate1 = g_val / (1.0 + e2)
    gate2 = g_val * e2 / (1.0 + e2)
    return jnp.where(lane == 0, i1.astype(F32),
                     jnp.where(lane == 1, i2.astype(F32),
                               jnp.where(lane == 2, gate1,
                                         jnp.where(lane == 3, gate2, 0.0))))


def _memattn_kernel(h_ref, wq_ref, km_ref, vm_ref, wo_ref, g_ref, beta_ref, wrh_ref, wrl_ref, br_ref,
                    o_ref, r_ref):
    h = h_ref[...]
    q = jnp.dot(h.astype(BF16), wq_ref[...], preferred_element_type=F32).astype(BF16)
    outs = []
    for hd in range(MEM_HEADS):
        cols = slice(hd * MEM_HEAD_DIM, (hd + 1) * MEM_HEAD_DIM)
        kh = km_ref[:, cols]
        vh = vm_ref[:, cols]
        s = lax.dot_general(q[:, cols], kh, (((1,), (1,)), ((), ())),
                            preferred_element_type=F32) * (MEM_HEAD_DIM ** -0.5)
        s = s - jnp.max(s, axis=-1, keepdims=True)
        e = jnp.exp(s)
        p = e / jnp.sum(e, axis=-1, keepdims=True)
        outs.append(jnp.dot(p.astype(BF16), vh, preferred_element_type=F32).astype(BF16))
    o = jnp.concatenate(outs, axis=-1)
    attn = jnp.dot(o, wo_ref[...], preferred_element_type=F32)
    h2 = _layer_norm(DN_ALPHA * h + attn, g_ref[...], beta_ref[...])
    o_ref[...] = h2
    h_hi, h_lo = _split_bf16(h2)
    logits = (jnp.dot(h_hi, wrh_ref[...], preferred_element_type=F32)
              + jnp.dot(h_lo, wrh_ref[...], preferred_element_type=F32)
              + jnp.dot(h_hi, wrl_ref[...], preferred_element_type=F32)) + br_ref[...]
    r_ref[...] = _route(logits)


def _memattn(h1, wq, k_mem, v_mem, wo, g, b, wr_hi, wr_lo, b_r, tm):
    s = h1.shape[0]
    return pl.pallas_call(
        _memattn_kernel,
        grid=(s // tm,),
        in_specs=[pl.BlockSpec((tm, D_MODEL), lambda i: (i, 0)),
                  _resident((D_MODEL, D_MODEL), lambda i: (0, 0)),
                  _resident((MEM_LEN, D_MODEL), lambda i: (0, 0)),
                  _resident((MEM_LEN, D_MODEL), lambda i: (0, 0)),
                  _resident((D_MODEL, D_MODEL), lambda i: (0, 0)),
                  pl.BlockSpec((1, D_MODEL), lambda i: (0, 0)),
                  pl.BlockSpec((1, D_MODEL), lambda i: (0, 0)),
                  _resident((D_MODEL, LANES), lambda i: (0, 0)),
                  _resident((D_MODEL, LANES), lambda i: (0, 0)),
                  pl.BlockSpec((1, LANES), lambda i: (0, 0))],
        out_specs=[pl.BlockSpec((tm, D_MODEL), lambda i: (i, 0)),
                   pl.BlockSpec((tm, LANES), lambda i: (i, 0))],
        out_shape=[jax.ShapeDtypeStruct((s, D_MODEL), F32),
                   jax.ShapeDtypeStruct((s, LANES), F32)],
        compiler_params=_cparams(("arbitrary",)),
        name="memattn_ln2_router",
    )(h1, wq, k_mem, v_mem, wo, g, b, wr_hi, wr_lo, b_r)


def _dispatch_kernel(e_ref, tok_ref, pos_ref, meta_ref):
    ca = DISPATCH_CHUNK
    n_chunks = N_ASSIGN // ca
    e_iota = lax.broadcasted_iota(I32, (N_EXPERTS, ca), 0)

    def onehot(c):
        e_c = e_ref[:, pl.ds(pl.multiple_of(c * ca, ca), ca)]
        return (e_iota == e_c).astype(F32)

    counts = lax.fori_loop(
        0, n_chunks, lambda c, acc: acc + jnp.sum(onehot(c), axis=1, keepdims=True),
        jnp.zeros((N_EXPERTS, 1), F32))
    nblk = jnp.right_shift(counts.astype(I32) + (MOE_ROWS - 1), MOE_ROWS_LOG2)
    strict_lower = (lax.broadcasted_iota(I32, (N_EXPERTS, N_EXPERTS), 1)
                    < lax.broadcasted_iota(I32, (N_EXPERTS, N_EXPERTS), 0)).astype(BF16)
    nblk_lanes = jnp.broadcast_to(nblk.astype(F32), (N_EXPERTS, LANES)).astype(BF16)
    blk_start = jnp.dot(strict_lower, nblk_lanes, preferred_element_type=F32)
    pad_start = blk_start[:, 0:1] * MOE_ROWS
    lane = lax.broadcasted_iota(I32, (N_EXPERTS, LANES), 1)
    meta_ref[...] = jnp.where(lane == 0, blk_start.astype(I32),
                              jnp.where(lane == 1, jnp.broadcast_to(nblk, (N_EXPERTS, LANES)), 0))

    earlier = (lax.broadcasted_iota(I32, (ca, ca), 0)
               < lax.broadcasted_iota(I32, (ca, ca), 1)).astype(BF16)
    blk_iota = lax.broadcasted_iota(I32, (N_ROW_BLOCKS, ca), 0)
    row_iota = lax.broadcasted_iota(I32, (MOE_ROWS, ca), 0)
    a_iota = lax.broadcasted_iota(I32, (1, ca), 1)

    def body(c, state):
        seen, acc_hi, acc_lo = state
        oh = onehot(c)
        before = jnp.dot(oh.astype(BF16), earlier, preferred_element_type=F32) + seen
        dest = jnp.sum(oh * (before + pad_start), axis=0, keepdims=True).astype(I32)
        pos_ref[:, pl.ds(pl.multiple_of(c * ca, ca), ca)] = dest
        in_blk = (blk_iota == jnp.right_shift(dest, MOE_ROWS_LOG2)).astype(F32)
        in_row = (row_iota == jnp.bitwise_and(dest, MOE_ROWS - 1)).astype(BF16)
        tok = jnp.right_shift(c * ca + a_iota, 1)
        tok_hi = (jnp.right_shift(tok, 7) + 1).astype(F32)
        tok_lo = jnp.bitwise_and(tok, 127).astype(F32)
        nt = (((1,), (1,)), ((), ()))
        acc_hi = acc_hi + lax.dot_general((in_blk * tok_hi).astype(BF16), in_row, nt,
                                          preferred_element_type=F32)
        acc_lo = acc_lo + lax.dot_general((in_blk * tok_lo).astype(BF16), in_row, nt,
                                          preferred_element_type=F32)
        return seen + jnp.sum(oh, axis=1, keepdims=True), acc_hi, acc_lo

    zeros = jnp.zeros((N_ROW_BLOCKS, MOE_ROWS), F32)
    _, acc_hi, acc_lo = lax.fori_loop(0, n_chunks, body,
                                      (jnp.zeros((N_EXPERTS, 1), F32), zeros, zeros))
    pad_tok = jnp.bitwise_and(
        lax.broadcasted_iota(I32, (N_ROW_BLOCKS, MOE_ROWS), 0) * MOE_ROWS
        + lax.broadcasted_iota(I32, (N_ROW_BLOCKS, MOE_ROWS), 1), SEQ - 1)
    tok_ref[...] = jnp.where(acc_hi > 0.5, ((acc_hi - 1.0) * 128.0 + acc_lo).astype(I32), pad_tok)


def _dispatch(flat_e):
    return pl.pallas_call(
        _dispatch_kernel,
        out_shape=[jax.ShapeDtypeStruct((N_ROW_BLOCKS, MOE_ROWS), I32),
                   jax.ShapeDtypeStruct((1, N_ASSIGN), I32),
                   jax.ShapeDtypeStruct((N_EXPERTS, LANES), I32)],
        compiler_params=pltpu.CompilerParams(vmem_limit_bytes=VMEM_LIMIT),
        name="moe_dispatch",
    )(flat_e)


def _expert_kernel(bs_ref, nb_ref, tot_ref, tok_hbm, h_hbm, w1_hbm, w3_hbm, w2_hbm, y_hbm,
                   idx, xbuf, ybuf, isem, gsem, ysem, wsem, w1f, w3f, w2f, w1b, w3b, w2b):
    e = pl.program_id(0)
    n_exp = pl.num_programs(0)
    total = tot_ref[0]
    first = bs_ref[e]
    nblk = nb_ref[e]

    def idx_copy(blk):
        row = jnp.minimum(blk, N_ROW_BLOCKS - 1)
        slot = blk % GATHER_SLOTS
        return pltpu.make_async_copy(tok_hbm.at[pl.ds(row, 1)], idx.at[pl.ds(slot, 1)],
                                     isem.at[slot])

    def issue_rows(blk, rows):
        slot = blk % GATHER_SLOTS
        for r in rows:
            tok = idx[slot, r]
            pltpu.make_async_copy(h_hbm.at[pl.ds(tok, 1)], xbuf.at[slot, pl.ds(r, 1)],
                                  gsem.at[slot]).start()

    def wait_gather(blk):
        slot = blk % GATHER_SLOTS
        pltpu.make_async_copy(h_hbm.at[pl.ds(0, MOE_ROWS)], xbuf.at[slot], gsem.at[slot]).wait()

    def y_copy(blk, slot):
        return pltpu.make_async_copy(ybuf.at[slot], y_hbm.at[pl.ds(blk * MOE_ROWS, MOE_ROWS)],
                                     ysem.at[slot])

    def weight_copies(ex):
        slot = ex % 2
        return [pltpu.make_async_copy(src.at[ex], dst.at[slot], wsem.at[slot])
                for src, dst in ((w1_hbm, w1f), (w3_hbm, w3f), (w2_hbm, w2f))]

    @pl.when(e == 0)
    def _():
        for cp in weight_copies(0):
            cp.start(priority=1)

        for blk in range(GATHER_SLOTS):
            idx_copy(blk).start()
        for blk in range(GATHER_SLOTS - 1):
            idx_copy(blk).wait()

        def prime(r, c):
            for blk in range(GATHER_SLOTS - 1):
                pltpu.make_async_copy(h_hbm.at[pl.ds(idx[blk, r], 1)], xbuf.at[blk, pl.ds(r, 1)],
                                      gsem.at[blk]).start()
            return c
        lax.fori_loop(0, MOE_ROWS, prime, 0, unroll=8)

    @pl.when(e + 1 < n_exp)
    def _():
        for cp in weight_copies(e + 1):
            cp.start(priority=1)

    for cp in weight_copies(e):
        cp.wait()

    @pl.when(nblk > 0)
    def _():
        wslot = e % 2
        w1b[...] = w1f[wslot].astype(BF16)
        w3b[...] = w3f[wslot].astype(BF16)
        w2b[...] = w2f[wslot].astype(BF16)

    n_piece = 4
    rows_per_piece = MOE_ROWS // n_piece
    cols_per_piece = D_MODEL // n_piece

    def block(b, c):
        blk = first + b
        slot = blk % 2
        wait_gather(blk)
        idx_copy(blk + GATHER_SLOTS - 1).wait()
        idx_copy(blk + GATHER_SLOTS).start()

        @pl.when(blk >= 2)
        def _():
            y_copy(blk - 2, slot).wait()

        xb = xbuf[blk % GATHER_SLOTS].astype(BF16)
        a = jnp.dot(xb, w1b[...], preferred_element_type=F32)
        g = jnp.dot(xb, w3b[...], preferred_element_type=F32)
        hid = (a * jax.nn.sigmoid(a) * g).astype(BF16)
        for p in range(n_piece):
            issue_rows(blk + GATHER_SLOTS - 1,
                       range(p * rows_per_piece, (p + 1) * rows_per_piece))
            cols = slice(p * cols_per_piece, (p + 1) * cols_per_piece)
            ybuf[slot, :, cols] = jnp.dot(hid, w2b[:, cols], preferred_element_type=F32)
        y_copy(blk, slot).start()
        return c

    lax.fori_loop(0, nblk, block, 0)

    @pl.when(e == n_exp - 1)
    def _():
        for ahead in range(GATHER_SLOTS - 1):
            wait_gather(total + ahead)
        idx_copy(total + GATHER_SLOTS - 1).wait()
        for back in (1, 2):
            @pl.when(total >= back)
            def _():
                y_copy(total - back, (total - back) % 2).wait()

        ybuf[0] = jnp.zeros((MOE_ROWS, D_MODEL), F32)

        def zero_block(blk, c):
            y_copy(blk, 0).start()
            y_copy(blk, 0).wait()
            return c

        lax.fori_loop(total, N_ROW_BLOCKS, zero_block, 0)


def _experts(blk_start, nblk, total, row_tok, h2, w1, w3, w2):
    grid_spec = pltpu.PrefetchScalarGridSpec(
        num_scalar_prefetch=3,
        grid=(N_EXPERTS,),
        in_specs=[pl.BlockSpec(memory_space=pl.ANY)] * 5,
        out_specs=pl.BlockSpec(memory_space=pl.ANY),
        scratch_shapes=[pltpu.SMEM((GATHER_SLOTS, MOE_ROWS), I32),
                        pltpu.VMEM((GATHER_SLOTS, MOE_ROWS, D_MODEL), F32),
                        pltpu.VMEM((2, MOE_ROWS, D_MODEL), F32),
                        pltpu.SemaphoreType.DMA((GATHER_SLOTS,)),
                        pltpu.SemaphoreType.DMA((GATHER_SLOTS,)),
                        pltpu.SemaphoreType.DMA((2,)),
                        pltpu.SemaphoreType.DMA((2,)),
                        pltpu.VMEM((2, D_MODEL, D_EXPERT), F32),
                        pltpu.VMEM((2, D_MODEL, D_EXPERT), F32),
                        pltpu.VMEM((2, D_EXPERT, D_MODEL), F32),
                        pltpu.VMEM((D_MODEL, D_EXPERT), BF16),
                        pltpu.VMEM((D_MODEL, D_EXPERT), BF16),
                        pltpu.VMEM((D_EXPERT, D_MODEL), BF16)],
    )
    return pl.pallas_call(
        _expert_kernel,
        grid_spec=grid_spec,
        out_shape=jax.ShapeDtypeStruct((N_ROWS, D_MODEL), F32),
        compiler_params=_cparams(("arbitrary",)),
        name="experts",
    )(blk_start, nblk, total, row_tok, h2, w1, w3, w2)


def _combine_kernel(y_hbm, pos_ref, pos_next_ref, h_ref, r_ref, g_ref, beta_ref, o_ref, ybuf, sem,
                    *, tc):
    i = pl.program_id(0)
    n = pl.num_programs(0)
    slot = i % 2

    def start_gather(table, slot_):
        def issue(r, c):
            for kk in range(TOP_K):
                p = table[0, 0, r * TOP_K + kk]
                pltpu.make_async_copy(y_hbm.at[pl.ds(p, 1)], ybuf.at[slot_, kk, pl.ds(r, 1)],
                                      sem.at[slot_]).start(priority=kk)
            return c
        lax.fori_loop(0, tc, issue, 0, unroll=8)

    @pl.when(i == 0)
    def _():
        start_gather(pos_ref, 0)

    @pl.when(i + 1 < n)
    def _():
        start_gather(pos_next_ref, 1 - slot)

    for kk in range(TOP_K):
        pltpu.make_async_copy(y_hbm.at[pl.ds(0, tc)], ybuf.at[slot, kk], sem.at[slot]).wait()
    gates = r_ref[...]
    moe = (ybuf[slot, 0] * gates[:, TOP_K:TOP_K + 1]
           + ybuf[slot, 1] * gates[:, TOP_K + 1:TOP_K + 2])
    o_ref[...] = _layer_norm(DN_ALPHA * h_ref[...] + moe, g_ref[...], beta_ref[...])


def _combine(pos, y_rows, h2, rout, g, b, tc):
    s = h2.shape[0]
    n = s // tc
    pos = pos.reshape(n, 1, tc * TOP_K)
    smem_block = functools.partial(pl.BlockSpec, (1, 1, tc * TOP_K), memory_space=pltpu.SMEM)
    return pl.pallas_call(
        functools.partial(_combine_kernel, tc=tc),
        grid=(n,),
        in_specs=[pl.BlockSpec(memory_space=pl.ANY),
                  smem_block(lambda i: (i, 0, 0)),
                  smem_block(lambda i: (jnp.minimum(i + 1, n - 1), 0, 0)),
                  pl.BlockSpec((tc, D_MODEL), lambda i: (i, 0)),
                  pl.BlockSpec((tc, LANES), lambda i: (i, 0)),
                  pl.BlockSpec((1, D_MODEL), lambda i: (0, 0)),
                  pl.BlockSpec((1, D_MODEL), lambda i: (0, 0))],
        out_specs=pl.BlockSpec((tc, D_MODEL), lambda i: (i, 0)),
        out_shape=jax.ShapeDtypeStruct((s, D_MODEL), F32),
        scratch_shapes=[pltpu.VMEM((2, TOP_K, tc, D_MODEL), F32),
                        pltpu.SemaphoreType.DMA((2,))],
        compiler_params=_cparams(("arbitrary",)),
        name="combine_ln3",
    )(y_rows, pos, pos, h2, rout, g, b)


def kernel(x, mem, w_in, gm_ln_g, gm_ln_b, gm_w_s, gm_b_s, w_mix_out, ln1_g, ln1_b,
           mem_w_q, mem_w_k, mem_w_v, mem_w_o, ln2_g, ln2_b,
           w_group, b_group, w_router, b_router, w1, w3, w2, ln3_g, ln3_b):
    assert x.shape == (1, SEQ, D_MODEL) and w_in.shape[0] == 1
    xs = x[0]
    l = 0
    w_mix = w_mix_out[l].astype(BF16)
    w_q = mem_w_q[l].astype(BF16)
    w_o = mem_w_o[l].astype(BF16)
    w_r = jnp.concatenate(
        [w_router[l].transpose(1, 0, 2).reshape(D_MODEL, N_EXPERTS), w_group[l],
         jnp.zeros((D_MODEL, LANES - N_EXPERTS - N_GROUPS), F32)], axis=1)
    wr_hi, wr_lo = _split_bf16(w_r)
    b_r = jnp.concatenate([b_router[l].reshape(-1), b_group[l],
                           jnp.zeros((LANES - N_EXPERTS - N_GROUPS,), F32)]).reshape(1, LANES)
    row = lambda v: v.reshape(1, D_MODEL)

    uvk, qvt = _in_proj(xs, w_in[l], tm=1024, q_scale=SB_HEAD_DIM ** -0.5 * math.log2(math.e))
    mix_a = _gmlp(uvk, gm_w_s[l], gm_ln_g[l], gm_ln_b[l], gm_b_s[l].T, tg=512)
    mix_b = _stick_breaking(qvt, uvk, tq=256, bk=128)
    h1 = _mixout(mix_a, mix_b, w_mix, xs, row(ln1_g[l]), row(ln1_b[l]), tm=512)
    k_mem = _proj(mem[0], mem_w_k[l], tm=MEM_LEN, tn=1024)
    v_mem = _proj(mem[0], mem_w_v[l], tm=MEM_LEN, tn=1024)
    h2, rout = _memattn(h1, w_q, k_mem, v_mem, w_o, row(ln2_g[l]), row(ln2_b[l]), wr_hi, wr_lo, b_r, tm=512)
    flat_e = rout[:, 0:TOP_K].astype(I32).reshape(1, N_ASSIGN)
    row_tok, pos, meta = _dispatch(flat_e)
    blk_start, nblk = meta[:, 0], meta[:, 1]
    total = (blk_start[N_EXPERTS - 1] + nblk[N_EXPERTS - 1]).reshape(1)
    y_rows = _experts(blk_start, nblk, total, row_tok, h2, w1[l], w3[l], w2[l])
    out = _combine(pos.reshape(N_ASSIGN), y_rows, h2, rout, row(ln3_g[l]), row(ln3_b[l]), tc=256)
    return out.reshape(1, SEQ, D_MODEL)
```

```python
import functools
import math

import jax
import jax.numpy as jnp
from jax import lax
from jax.experimental import pallas as pl
from jax.experimental.pallas import tpu as pltpu

D_MODEL = 2048
SEQ = 8192
CHUNK = 64
MEM_LEN = 256
GM_BLOCK = 128
GM_GROUPS = 8
GM_WIDTH = 1024
SB_HEADS = 8
SB_HEAD_DIM = 128
SB_WIDTH = 1024
MEM_HEADS = 4
MEM_HEAD_DIM = 512
N_GROUPS = 8
EXPERTS_PER_GROUP = 8
N_EXPERTS = 64
TOP_K = 2
D_EXPERT = 512
DN_ALPHA = 2.0 ** 0.25
LN_EPS = 1e-5

LANES = 128
MOE_ROWS = 128
MOE_ROWS_LOG2 = 7
N_ASSIGN = SEQ * TOP_K
N_ROW_BLOCKS = N_ASSIGN // MOE_ROWS + N_EXPERTS
N_ROWS = N_ROW_BLOCKS * MOE_ROWS
DISPATCH_CHUNK = 512
GATHER_SLOTS = 3
WEIGHT_SLOTS = 3
SB_UNDERFLOW = 160.0
SB_LOOP_BLOCKS = 1
SB_STATIC_BELOW = 2
VMEM_LIMIT = 56 * 1024 * 1024

BF16 = jnp.bfloat16
F32 = jnp.float32
I32 = jnp.int32


def _cparams(sem):
    return pltpu.CompilerParams(dimension_semantics=sem, vmem_limit_bytes=VMEM_LIMIT)


def _resident(shape, index_map):
    return pl.BlockSpec(shape, index_map, pipeline_mode=pl.Buffered(1))


def _layer_norm(y, g, b):
    mu = jnp.mean(y, axis=-1, keepdims=True)
    yc = y - mu
    var = jnp.mean(yc * yc, axis=-1, keepdims=True)
    return yc * lax.rsqrt(var + LN_EPS) * g + b


def _split_bf16(v):
    hi = v.astype(BF16)
    lo = (v - hi.astype(F32)).astype(BF16)
    return hi, lo


def _proj_kernel(x_ref, w_ref, o_ref, xb_ref):
    @pl.when(pl.program_id(1) == 0)
    def _():
        xb_ref[...] = x_ref[...].astype(BF16)

    o_ref[...] = jnp.dot(xb_ref[...], w_ref[...].astype(BF16),
                         preferred_element_type=F32).astype(o_ref.dtype)


def _proj(x, w, tm, tn):
    m, k = x.shape
    n = w.shape[1]
    return pl.pallas_call(
        _proj_kernel,
        grid=(m // tm, n // tn),
        in_specs=[pl.BlockSpec((tm, k), lambda i, j: (i, 0)),
                  pl.BlockSpec((k, tn), lambda i, j: (0, j))],
        out_specs=pl.BlockSpec((tm, tn), lambda i, j: (i, j)),
        out_shape=jax.ShapeDtypeStruct((m, n), BF16),
        scratch_shapes=[pltpu.VMEM((tm, k), BF16)],
        compiler_params=_cparams(("arbitrary", "arbitrary")),
        name="proj",
    )(x, w)


IN_BLOCKS = 5
IN_Q, IN_K, IN_V = 2, 3, 4


def _in_proj_kernel(x_ref, w_ref, o_ref, ot_ref, xb_ref, *, q_scale):
    j = pl.program_id(1)

    @pl.when(j == 0)
    def _():
        xb_ref[...] = x_ref[...].astype(BF16)

    wb = w_ref[...].astype(BF16)
    is_t = jnp.logical_or(j == IN_Q, j == IN_V)

    @pl.when(jnp.logical_not(is_t))
    def _():
        o_ref[...] = jnp.dot(xb_ref[...], wb, preferred_element_type=F32).astype(o_ref.dtype)

    @pl.when(is_t)
    def _():
        acc = lax.dot_general(wb, xb_ref[...], (((0,), (1,)), ((), ())), preferred_element_type=F32)
        ot_ref[...] = (acc * jnp.where(j == IN_Q, q_scale, 1.0).astype(F32)).astype(ot_ref.dtype)


def _in_proj(x, w, tm, q_scale):
    m, k = x.shape
    tn = SB_WIDTH
    omap = lambda i, j: (i, j - (j >= IN_Q).astype(I32) - (j >= IN_V).astype(I32))
    otmap = lambda i, j: ((j >= IN_V).astype(I32), i)
    return pl.pallas_call(
        functools.partial(_in_proj_kernel, q_scale=q_scale),
        grid=(m // tm, IN_BLOCKS),
        in_specs=[pl.BlockSpec((tm, k), lambda i, j: (i, 0)),
                  pl.BlockSpec((k, tn), lambda i, j: (0, j))],
        out_specs=[pl.BlockSpec((tm, tn), omap),
                   pl.BlockSpec((tn, tm), otmap)],
        out_shape=[jax.ShapeDtypeStruct((m, 3 * tn), BF16),
                   jax.ShapeDtypeStruct((2 * tn, m), BF16)],
        scratch_shapes=[pltpu.VMEM((tm, k), BF16)],
        compiler_params=_cparams(("arbitrary", "arbitrary")),
        name="in_proj",
    )(x, w)


def _gmlp_kernel(u_ref, v_ref, ws_ref, g_ref, b_ref, bs_ref, o_ref, *, tg):
    t_chunk = lax.broadcasted_iota(I32, (GM_BLOCK, GM_BLOCK), 0) // CHUNK
    s_chunk = lax.broadcasted_iota(I32, (GM_BLOCK, GM_BLOCK), 1) // CHUNK
    causal = t_chunk >= s_chunk
    for g in range(GM_GROUPS):
        wm = jnp.where(causal, ws_ref[g], 0.0).astype(BF16)
        ln_g = g_ref[g:g + 1, :]
        ln_b = b_ref[g:g + 1, :]
        bias = bs_ref[:, g:g + 1]
        cols = slice(g * LANES, (g + 1) * LANES)
        for n in range(tg // GM_BLOCK):
            rows = slice(n * GM_BLOCK, (n + 1) * GM_BLOCK)
            v = jax.nn.gelu(v_ref[rows, cols].astype(F32))
            vg = _layer_norm(v, ln_g, ln_b).astype(BF16)
            mixed = jnp.dot(wm, vg, preferred_element_type=F32) + bias
            u = jax.nn.gelu(u_ref[rows, cols].astype(F32))
            o_ref[rows, cols] = (u * mixed).astype(o_ref.dtype)


def _gmlp(uvk, gm_w_s, gm_ln_g, gm_ln_b, gm_b_s_t, tg):
    s = uvk.shape[0]
    return pl.pallas_call(
        functools.partial(_gmlp_kernel, tg=tg),
        grid=(s // tg,),
        in_specs=[pl.BlockSpec((tg, GM_WIDTH), lambda i: (i, 0)),
                  pl.BlockSpec((tg, GM_WIDTH), lambda i: (i, 1)),
                  pl.BlockSpec((GM_GROUPS, GM_BLOCK, GM_BLOCK), lambda i: (0, 0, 0)),
                  pl.BlockSpec((GM_GROUPS, LANES), lambda i: (0, 0)),
                  pl.BlockSpec((GM_GROUPS, LANES), lambda i: (0, 0)),
                  pl.BlockSpec((GM_BLOCK, GM_GROUPS), lambda i: (0, 0))],
        out_specs=pl.BlockSpec((tg, GM_WIDTH), lambda i: (i, 0)),
        out_shape=jax.ShapeDtypeStruct((s, GM_WIDTH), BF16),
        compiler_params=_cparams(("arbitrary",)),
        name="gmlp",
    )(uvk, uvk, gm_w_s, gm_ln_g, gm_ln_b, gm_b_s_t)


def _softplus2(u):
    return jnp.maximum(u, 0.0) + jnp.log2(1.0 + jnp.exp2(-jnp.abs(u)))


def _sb_kernel(qt_ref, k_ref, vt_ref, o_ref, *acc_refs, tq, bk):
    q0 = pl.program_id(0) * tq
    upper = (lax.broadcasted_iota(I32, (bk, bk), 1)
             >= lax.broadcasted_iota(I32, (bk, bk), 0)).astype(BF16)
    upper2 = jnp.concatenate([upper, upper], axis=1)

    heads = range(SB_HEADS)
    hrows = [slice(h * SB_HEAD_DIM, (h + 1) * SB_HEAD_DIM) for h in heads]

    def tiles(blocks, carries):
        zs, sums, masks = [], [], []
        for ks, lo, masked in blocks:
            width = tq - lo
            zs.append([jnp.dot(k_ref[pl.ds(ks, bk), hrows[h]], qt_ref[hrows[h], lo:],
                               preferred_element_type=F32) for h in heads])
            if masked is True:
                masks.append(lax.broadcasted_iota(I32, (bk, width), 0)
                             < lax.broadcasted_iota(I32, (bk, width), 1))
            elif masked is False:
                masks.append(None)
            else:
                masks.append(jnp.broadcast_to(masked, (bk, width)))
        for zb, mask in zip(zs, masks):
            sps = [_softplus2(z) for z in zb]
            if mask is not None:
                sps = [jnp.where(mask, sp, 0.0) for sp in sps]
            sums.append([jnp.dot(upper2, jnp.concatenate(_split_bf16(sp), axis=0),
                                 preferred_element_type=F32) for sp in sps])
        csums = []
        for (ks, lo, masked), sb in zip(blocks, sums):
            cb = [sb[h] + carries[h][:, lo:] for h in heads]
            carries = [cb[h][0:1, :] if lo == 0
                       else jnp.concatenate([carries[h][:, :lo], cb[h][0:1, :]], axis=1)
                       for h in heads]
            csums.append(cb)
        probs = []
        for zb, cb, mask in zip(zs, csums, masks):
            pb = [jnp.exp2(zb[h] - cb[h]) for h in heads]
            if mask is not None:
                pb = [jnp.where(mask, a, 0.0) for a in pb]
            probs.append([a.astype(BF16) for a in pb])
        for (ks, lo, masked), pb in zip(blocks, probs):
            for h in heads:
                acc_refs[h][:, lo:] += jnp.dot(vt_ref[hrows[h], pl.ds(ks, bk)], pb[h],
                                               preferred_element_type=F32)
        return carries

    for acc_ref in acc_refs:
        acc_ref[...] = jnp.zeros_like(acc_ref)
    carries = [jnp.zeros((1, tq), F32) for _ in range(SB_HEADS)]
    blocks = [(pl.multiple_of(q0 + d * bk, bk), d * bk, True) for d in reversed(range(tq // bk))]
    for b in range(1, SB_STATIC_BELOW + 1):
        below = q0 // bk - b
        blocks.append((pl.multiple_of(jnp.maximum(below, 0) * bk, bk), 0, below >= 0))
    carries = tiles(blocks, carries)

    def cond(state):
        j = state[0]
        lowest = functools.reduce(jnp.minimum, state[1:])
        return jnp.logical_and(j >= 0, jnp.min(lowest) < SB_UNDERFLOW)

    def body(state):
        j = state[0]
        blocks = [(pl.multiple_of((j - b) * bk, bk), 0, False) for b in range(SB_LOOP_BLOCKS)]
        return (j - SB_LOOP_BLOCKS,) + tuple(tiles(blocks, list(state[1:])))

    assert (tq // bk) % SB_LOOP_BLOCKS == 0
    lax.while_loop(cond, body, (q0 // bk - 1 - SB_STATIC_BELOW,) + tuple(carries))
    for h in range(SB_HEADS):
        o_ref[:, h * SB_HEAD_DIM:(h + 1) * SB_HEAD_DIM] = acc_refs[h][...].T.astype(o_ref.dtype)


def _stick_breaking(qvt, uvk, tq, bk):
    s = uvk.shape[0]
    return pl.pallas_call(
        functools.partial(_sb_kernel, tq=tq, bk=bk),
        grid=(s // tq,),
        in_specs=[pl.BlockSpec((SB_WIDTH, tq), lambda i: (0, i)),
                  _resident((s, SB_WIDTH), lambda i: (0, 2 * GM_WIDTH // SB_WIDTH)),
                  _resident((SB_WIDTH, s), lambda i: (1, 0))],
        out_specs=pl.BlockSpec((tq, SB_WIDTH), lambda i: (i, 0)),
        out_shape=jax.ShapeDtypeStruct((s, SB_WIDTH), BF16),
        scratch_shapes=[pltpu.VMEM((SB_HEAD_DIM, tq), F32) for _ in range(SB_HEADS)],
        compiler_params=_cparams(("arbitrary",)),
        name="stick_breaking",
    )(qvt, uvk, qvt)


def _mixout_kernel(a_ref, b_ref, wa_ref, wb_ref, x_ref, g_ref, beta_ref, o_ref):
    mixed = (jnp.dot(a_ref[...], wa_ref[...], preferred_element_type=F32)
             + jnp.dot(b_ref[...], wb_ref[...], preferred_element_type=F32))
    y = DN_ALPHA * x_ref[...] + mixed
    o_ref[...] = _layer_norm(y, g_ref[...], beta_ref[...])


def _mixout(mix_a, mix_b, w, x, g, b, tm):
    s = x.shape[0]
    return pl.pallas_call(
        _mixout_kernel,
        grid=(s // tm,),
        in_specs=[pl.BlockSpec((tm, GM_WIDTH), lambda i: (i, 0)),
                  pl.BlockSpec((tm, SB_WIDTH), lambda i: (i, 0)),
                  _resident((GM_WIDTH, D_MODEL), lambda i: (0, 0)),
                  _resident((SB_WIDTH, D_MODEL), lambda i: (1, 0)),
                  pl.BlockSpec((tm, D_MODEL), lambda i: (i, 0)),
                  pl.BlockSpec((1, D_MODEL), lambda i: (0, 0)),
                  pl.BlockSpec((1, D_MODEL), lambda i: (0, 0))],
        out_specs=pl.BlockSpec((tm, D_MODEL), lambda i: (i, 0)),
        out_shape=jax.ShapeDtypeStruct((s, D_MODEL), F32),
        compiler_params=_cparams(("arbitrary",)),
        name="mixout_ln1",
    )(mix_a, mix_b, w, w, x, g, b)


def _route(logits):
    lane = lax.broadcasted_iota(I32, logits.shape, 1)
    neg = jnp.float32(-jnp.inf)
    big = jnp.int32(1 << 20)

    def first_argmax(vals):
        m = jnp.max(vals, axis=-1, keepdims=True)
        idx = jnp.min(jnp.where(vals == m, lane, big), axis=-1, keepdims=True)
        return m, idx

    is_group = jnp.logical_and(lane >= N_EXPERTS, lane < N_EXPERTS + N_GROUPS)
    gl = jnp.where(is_group, logits, neg)
    gmax, glane = first_argmax(gl)
    g_val = 1.0 / jnp.sum(jnp.exp(gl - gmax), axis=-1, keepdims=True)
    gidx = glane - N_EXPERTS
    in_group = jnp.logical_and(lane >= gidx * EXPERTS_PER_GROUP,
                               lane < (gidx + 1) * EXPERTS_PER_GROUP)
    el = jnp.where(in_group, logits, neg)
    m1, i1 = first_argmax(el)
    m2, i2 = first_argmax(jnp.where(lane == i1, neg, el))
    e2 = jnp.exp(m2 - m1)
    gate1 = g_val / (1.0 + e2)
    gate2 = g_val * e2 / (1.0 + e2)
    return jnp.where(lane == 0, i1.astype(F32),
                     jnp.where(lane == 1, i2.astype(F32),
                               jnp.where(lane == 2, gate1,
                                         jnp.where(lane == 3, gate2, 0.0))))


def _memattn_kernel(h_ref, wq_ref, km_ref, vm_ref, wo_ref, g_ref, beta_ref, wrh_ref, wrl_ref, br_ref,
                    o_ref, r_ref):
    h = h_ref[...]
    q = jnp.dot(h.astype(BF16), wq_ref[...], preferred_element_type=F32).astype(BF16)
    outs = []
    for hd in range(MEM_HEADS):
        cols = slice(hd * MEM_HEAD_DIM, (hd + 1) * MEM_HEAD_DIM)
        kh = km_ref[:, cols]
        vh = vm_ref[:, cols]
        s = lax.dot_general(q[:, cols], kh, (((1,), (1,)), ((), ())),
                            preferred_element_type=F32) * (MEM_HEAD_DIM ** -0.5)
        s = s - jnp.max(s, axis=-1, keepdims=True)
        e = jnp.exp(s)
        p = e / jnp.sum(e, axis=-1, keepdims=True)
        outs.append(jnp.dot(p.astype(BF16), vh, preferred_element_type=F32).astype(BF16))
    o = jnp.concatenate(outs, axis=-1)
    attn = jnp.dot(o, wo_ref[...], preferred_element_type=F32)
    h2 = _layer_norm(DN_ALPHA * h + attn, g_ref[...], beta_ref[...])
    o_ref[...] = h2
    h_hi, h_lo = _split_bf16(h2)
    logits = (jnp.dot(h_hi, wrh_ref[...], preferred_element_type=F32)
              + jnp.dot(h_lo, wrh_ref[...], preferred_element_type=F32)
              + jnp.dot(h_hi, wrl_ref[...], preferred_element_type=F32)) + br_ref[...]
    r_ref[...] = _route(logits)


def _memattn(h1, wq, k_mem, v_mem, wo, g, b, wr_hi, wr_lo, b_r, tm):
    s = h1.shape[0]
    return pl.pallas_call(
        _memattn_kernel,
        grid=(s // tm,),
        in_specs=[pl.BlockSpec((tm, D_MODEL), lambda i: (i, 0)),
                  _resident((D_MODEL, D_MODEL), lambda i: (0, 0)),
                  _resident((MEM_LEN, D_MODEL), lambda i: (0, 0)),
                  _resident((MEM_LEN, D_MODEL), lambda i: (0, 0)),
                  _resident((D_MODEL, D_MODEL), lambda i: (0, 0)),
                  pl.BlockSpec((1, D_MODEL), lambda i: (0, 0)),
                  pl.BlockSpec((1, D_MODEL), lambda i: (0, 0)),
                  _resident((D_MODEL, LANES), lambda i: (0, 0)),
                  _resident((D_MODEL, LANES), lambda i: (0, 0)),
                  pl.BlockSpec((1, LANES), lambda i: (0, 0))],
        out_specs=[pl.BlockSpec((tm, D_MODEL), lambda i: (i, 0)),
                   pl.BlockSpec((tm, LANES), lambda i: (i, 0))],
        out_shape=[jax.ShapeDtypeStruct((s, D_MODEL), F32),
                   jax.ShapeDtypeStruct((s, LANES), F32)],
        compiler_params=_cparams(("arbitrary",)),
        name="memattn_ln2_router",
    )(h1, wq, k_mem, v_mem, wo, g, b, wr_hi, wr_lo, b_r)


def _dispatch_kernel(e_ref, tok_ref, pos_ref, meta_ref):
    ca = DISPATCH_CHUNK
    n_chunks = N_ASSIGN // ca
    e_iota = lax.broadcasted_iota(I32, (N_EXPERTS, ca), 0)

    def onehot(c):
        e_c = e_ref[:, pl.ds(pl.multiple_of(c * ca, ca), ca)]
        return (e_iota == e_c).astype(F32)

    counts = lax.fori_loop(
        0, n_chunks, lambda c, acc: acc + jnp.sum(onehot(c), axis=1, keepdims=True),
        jnp.zeros((N_EXPERTS, 1), F32))
    nblk = jnp.right_shift(counts.astype(I32) + (MOE_ROWS - 1), MOE_ROWS_LOG2)
    strict_lower = (lax.broadcasted_iota(I32, (N_EXPERTS, N_EXPERTS), 1)
                    < lax.broadcasted_iota(I32, (N_EXPERTS, N_EXPERTS), 0)).astype(BF16)
    nblk_lanes = jnp.broadcast_to(nblk.astype(F32), (N_EXPERTS, LANES)).astype(BF16)
    blk_start = jnp.dot(strict_lower, nblk_lanes, preferred_element_type=F32)
    pad_start = blk_start[:, 0:1] * MOE_ROWS
    lane = lax.broadcasted_iota(I32, (N_EXPERTS, LANES), 1)
    meta_ref[...] = jnp.where(lane == 0, blk_start.astype(I32),
                              jnp.where(lane == 1, jnp.broadcast_to(nblk, (N_EXPERTS, LANES)), 0))

    earlier = (lax.broadcasted_iota(I32, (ca, ca), 0)
               < lax.broadcasted_iota(I32, (ca, ca), 1)).astype(BF16)
    blk_iota = lax.broadcasted_iota(I32, (N_ROW_BLOCKS, ca), 0)
    row_iota = lax.broadcasted_iota(I32, (MOE_ROWS, ca), 0)
    a_iota = lax.broadcasted_iota(I32, (1, ca), 1)

    def body(c, state):
        seen, acc_hi, acc_lo = state
        oh = onehot(c)
        before = jnp.dot(oh.astype(BF16), earlier, preferred_element_type=F32) + seen
        dest = jnp.sum(oh * (before + pad_start), axis=0, keepdims=True).astype(I32)
        pos_ref[:, pl.ds(pl.multiple_of(c * ca, ca), ca)] = dest
        in_blk = (blk_iota == jnp.right_shift(dest, MOE_ROWS_LOG2)).astype(F32)
        in_row = (row_iota == jnp.bitwise_and(dest, MOE_ROWS - 1)).astype(BF16)
        tok = jnp.right_shift(c * ca + a_iota, 1)
        tok_hi = (jnp.right_shift(tok, 7) + 1).astype(F32)
        tok_lo = jnp.bitwise_and(tok, 127).astype(F32)
        nt = (((1,), (1,)), ((), ()))
        acc_hi = acc_hi + lax.dot_general((in_blk * tok_hi).astype(BF16), in_row, nt,
                                          preferred_element_type=F32)
        acc_lo = acc_lo + lax.dot_general((in_blk * tok_lo).astype(BF16), in_row, nt,
                                          preferred_element_type=F32)
        return seen + jnp.sum(oh, axis=1, keepdims=True), acc_hi, acc_lo

    zeros = jnp.zeros((N_ROW_BLOCKS, MOE_ROWS), F32)
    _, acc_hi, acc_lo = lax.fori_loop(0, n_chunks, body,
                                      (jnp.zeros((N_EXPERTS, 1), F32), zeros, zeros))
    pad_tok = jnp.bitwise_and(
        lax.broadcasted_iota(I32, (N_ROW_BLOCKS, MOE_ROWS), 0) * MOE_ROWS
        + lax.broadcasted_iota(I32, (N_ROW_BLOCKS, MOE_ROWS), 1), SEQ - 1)
    tok_ref[...] = jnp.where(acc_hi > 0.5, ((acc_hi - 1.0) * 128.0 + acc_lo).astype(I32), pad_tok)


def _dispatch(flat_e):
    return pl.pallas_call(
        _dispatch_kernel,
        out_shape=[jax.ShapeDtypeStruct((N_ROW_BLOCKS, MOE_ROWS), I32),
                   jax.ShapeDtypeStruct((1, N_ASSIGN), I32),
                   jax.ShapeDtypeStruct((N_EXPERTS, LANES), I32)],
        compiler_params=pltpu.CompilerParams(vmem_limit_bytes=VMEM_LIMIT),
        name="moe_dispatch",
    )(flat_e)


def _expert_kernel(bs_ref, nb_ref, tot_ref, tok_hbm, h_hbm, w1_hbm, w3_hbm, w2_hbm, y_hbm,
                   idx, xbuf, ybuf, isem, gsem, ysem, wsem, w1f, w3f, w2f, w1b, w3b, w2b):
    e = pl.program_id(0)
    n_exp = pl.num_programs(0)
    total = tot_ref[0]
    first = bs_ref[e]
    nblk = nb_ref[e]

    def idx_copy(blk):
        row = jnp.minimum(blk, N_ROW_BLOCKS - 1)
        slot = blk % GATHER_SLOTS
        return pltpu.make_async_copy(tok_hbm.at[pl.ds(row, 1)], idx.at[pl.ds(slot, 1)],
                                     isem.at[slot])

    def issue_rows(blk, rows):
        slot = blk % GATHER_SLOTS
        for r in rows:
            tok = idx[slot, r]
            pltpu.make_async_copy(h_hbm.at[pl.ds(tok, 1)], xbuf.at[slot, pl.ds(r, 1)],
                                  gsem.at[slot]).start()

    def wait_gather(blk):
        slot = blk % GATHER_SLOTS
        pltpu.make_async_copy(h_hbm.at[pl.ds(0, MOE_ROWS)], xbuf.at[slot], gsem.at[slot]).wait()

    def y_copy(blk, slot):
        return pltpu.make_async_copy(ybuf.at[slot], y_hbm.at[pl.ds(blk * MOE_ROWS, MOE_ROWS)],
                                     ysem.at[slot])

    def weight_copies(ex):
        slot = ex % WEIGHT_SLOTS
        return [pltpu.make_async_copy(src.at[ex], dst.at[slot], wsem.at[slot])
                for src, dst in ((w1_hbm, w1f), (w3_hbm, w3f), (w2_hbm, w2f))]

    @pl.when(e == 0)
    def _():
        for ex in range(WEIGHT_SLOTS - 1):
            for cp in weight_copies(ex):
                cp.start(priority=1)

        for blk in range(GATHER_SLOTS):
            idx_copy(blk).start()
        for blk in range(GATHER_SLOTS - 1):
            idx_copy(blk).wait()

        def prime(r, c):
            for blk in range(GATHER_SLOTS - 1):
                pltpu.make_async_copy(h_hbm.at[pl.ds(idx[blk, r], 1)], xbuf.at[blk, pl.ds(r, 1)],
                                      gsem.at[blk]).start()
            return c
        lax.fori_loop(0, MOE_ROWS, prime, 0, unroll=8)

    @pl.when(e + WEIGHT_SLOTS - 1 < n_exp)
    def _():
        for cp in weight_copies(e + WEIGHT_SLOTS - 1):
            cp.start(priority=1)

    for cp in weight_copies(e):
        cp.wait()

    @pl.when(nblk > 0)
    def _():
        wslot = e % WEIGHT_SLOTS
        w1b[...] = w1f[wslot].astype(BF16)
        w3b[...] = w3f[wslot].astype(BF16)
        w2b[...] = w2f[wslot].astype(BF16)

    n_piece = 4
    rows_per_piece = MOE_ROWS // n_piece
    cols_per_piece = D_MODEL // n_piece

    def block(b, c):
        blk = first + b
        slot = blk % 2
        wait_gather(blk)
        idx_copy(blk + GATHER_SLOTS - 1).wait()
        idx_copy(blk + GATHER_SLOTS).start()

        @pl.when(blk >= 2)
        def _():
            y_copy(blk - 2, slot).wait()

        xb = xbuf[blk % GATHER_SLOTS].astype(BF16)
        a = jnp.dot(xb, w1b[...], preferred_element_type=F32)
        g = jnp.dot(xb, w3b[...], preferred_element_type=F32)
        hid = (a * jax.nn.sigmoid(a) * g).astype(BF16)
        for p in range(n_piece):
            issue_rows(blk + GATHER_SLOTS - 1,
                       range(p * rows_per_piece, (p + 1) * rows_per_piece))
            cols = slice(p * cols_per_piece, (p + 1) * cols_per_piece)
            ybuf[slot, :, cols] = jnp.dot(hid, w2b[:, cols], preferred_element_type=F32)
        y_copy(blk, slot).start()
        return c

    lax.fori_loop(0, nblk, block, 0)

    @pl.when(e == n_exp - 1)
    def _():
        for ahead in range(GATHER_SLOTS - 1):
            wait_gather(total + ahead)
        idx_copy(total + GATHER_SLOTS - 1).wait()
        for back in (1, 2):
            @pl.when(total >= back)
            def _():
                y_copy(total - back, (total - back) % 2).wait()

        ybuf[0] = jnp.zeros((MOE_ROWS, D_MODEL), F32)

        def zero_block(blk, c):
            y_copy(blk, 0).start()
            y_copy(blk, 0).wait()
            return c

        lax.fori_loop(total, N_ROW_BLOCKS, zero_block, 0)


def _experts(blk_start, nblk, total, row_tok, h2, w1, w3, w2):
    grid_spec = pltpu.PrefetchScalarGridSpec(
        num_scalar_prefetch=3,
        grid=(N_EXPERTS,),
        in_specs=[pl.BlockSpec(memory_space=pl.ANY)] * 5,
        out_specs=pl.BlockSpec(memory_space=pl.ANY),
        scratch_shapes=[pltpu.SMEM((GATHER_SLOTS, MOE_ROWS), I32),
                        pltpu.VMEM((GATHER_SLOTS, MOE_ROWS, D_MODEL), F32),
                        pltpu.VMEM((2, MOE_ROWS, D_MODEL), F32),
                        pltpu.SemaphoreType.DMA((GATHER_SLOTS,)),
                        pltpu.SemaphoreType.DMA((GATHER_SLOTS,)),
                        pltpu.SemaphoreType.DMA((2,)),
                        pltpu.SemaphoreType.DMA((WEIGHT_SLOTS,)),
                        pltpu.VMEM((WEIGHT_SLOTS, D_MODEL, D_EXPERT), F32),
                        pltpu.VMEM((WEIGHT_SLOTS, D_MODEL, D_EXPERT), F32),
                        pltpu.VMEM((WEIGHT_SLOTS, D_EXPERT, D_MODEL), F32),
                        pltpu.VMEM((D_MODEL, D_EXPERT), BF16),
                        pltpu.VMEM((D_MODEL, D_EXPERT), BF16),
                        pltpu.VMEM((D_EXPERT, D_MODEL), BF16)],
    )
    return pl.pallas_call(
        _expert_kernel,
        grid_spec=grid_spec,
        out_shape=jax.ShapeDtypeStruct((N_ROWS, D_MODEL), F32),
        compiler_params=_cparams(("arbitrary",)),
        name="experts",
    )(blk_start, nblk, total, row_tok, h2, w1, w3, w2)


def _combine_kernel(y_hbm, pos_ref, pos_next_ref, h_ref, r_ref, g_ref, beta_ref, o_ref, ybuf, sem,
                    *, tc):
    i = pl.program_id(0)
    n = pl.num_programs(0)
    slot = i % 2

    def start_gather(table, slot_):
        def issue(r, c):
            for kk in range(TOP_K):
                p = table[0, 0, r * TOP_K + kk]
                pltpu.make_async_copy(y_hbm.at[pl.ds(p, 1)], ybuf.at[slot_, kk, pl.ds(r, 1)],
                                      sem.at[slot_]).start(priority=kk)
            return c
        lax.fori_loop(0, tc, issue, 0, unroll=8)

    @pl.when(i == 0)
    def _():
        start_gather(pos_ref, 0)

    @pl.when(i + 1 < n)
    def _():
        start_gather(pos_next_ref, 1 - slot)

    for kk in range(TOP_K):
        pltpu.make_async_copy(y_hbm.at[pl.ds(0, tc)], ybuf.at[slot, kk], sem.at[slot]).wait()
    gates = r_ref[...]
    moe = (ybuf[slot, 0] * gates[:, TOP_K:TOP_K + 1]
           + ybuf[slot, 1] * gates[:, TOP_K + 1:TOP_K + 2])
    o_ref[...] = _layer_norm(DN_ALPHA * h_ref[...] + moe, g_ref[...], beta_ref[...])


def _combine(pos, y_rows, h2, rout, g, b, tc):
    s = h2.shape[0]
    n = s // tc
    pos = pos.reshape(n, 1, tc * TOP_K)
    smem_block = functools.partial(pl.BlockSpec, (1, 1, tc * TOP_K), memory_space=pltpu.SMEM)
    return pl.pallas_call(
        functools.partial(_combine_kernel, tc=tc),
        grid=(n,),
        in_specs=[pl.BlockSpec(memory_space=pl.ANY),
                  smem_block(lambda i: (i, 0, 0)),
                  smem_block(lambda i: (jnp.minimum(i + 1, n - 1), 0, 0)),
                  pl.BlockSpec((tc, D_MODEL), lambda i: (i, 0)),
                  pl.BlockSpec((tc, LANES), lambda i: (i, 0)),
                  pl.BlockSpec((1, D_MODEL), lambda i: (0, 0)),
                  pl.BlockSpec((1, D_MODEL), lambda i: (0, 0))],
        out_specs=pl.BlockSpec((tc, D_MODEL), lambda i: (i, 0)),
        out_shape=jax.ShapeDtypeStruct((s, D_MODEL), F32),
        scratch_shapes=[pltpu.VMEM((2, TOP_K, tc, D_MODEL), F32),
                        pltpu.SemaphoreType.DMA((2,))],
        compiler_params=_cparams(("arbitrary",)),
        name="combine_ln3",
    )(y_rows, pos, pos, h2, rout, g, b)


def kernel(x, mem, w_in, gm_ln_g, gm_ln_b, gm_w_s, gm_b_s, w_mix_out, ln1_g, ln1_b,
           mem_w_q, mem_w_k, mem_w_v, mem_w_o, ln2_g, ln2_b,
           w_group, b_group, w_router, b_router, w1, w3, w2, ln3_g, ln3_b):
    assert x.shape == (1, SEQ, D_MODEL) and w_in.shape[0] == 1
    xs = x[0]
    l = 0
    w_mix = w_mix_out[l].astype(BF16)
    w_q = mem_w_q[l].astype(BF16)
    w_o = mem_w_o[l].astype(BF16)
    w_r = jnp.concatenate(
        [w_router[l].transpose(1, 0, 2).reshape(D_MODEL, N_EXPERTS), w_group[l],
         jnp.zeros((D_MODEL, LANES - N_EXPERTS - N_GROUPS), F32)], axis=1)
    wr_hi, wr_lo = _split_bf16(w_r)
    b_r = jnp.concatenate([b_router[l].reshape(-1), b_group[l],
                           jnp.zeros((LANES - N_EXPERTS - N_GROUPS,), F32)]).reshape(1, LANES)
    row = lambda v: v.reshape(1, D_MODEL)

    uvk, qvt = _in_proj(xs, w_in[l], tm=1024, q_scale=SB_HEAD_DIM ** -0.5 * math.log2(math.e))
    mix_a = _gmlp(uvk, gm_w_s[l], gm_ln_g[l], gm_ln_b[l], gm_b_s[l].T, tg=512)
    mix_b = _stick_breaking(qvt, uvk, tq=256, bk=128)
    h1 = _mixout(mix_a, mix_b, w_mix, xs, row(ln1_g[l]), row(ln1_b[l]), tm=512)
    k_mem = _proj(mem[0], mem_w_k[l], tm=MEM_LEN, tn=1024)
    v_mem = _proj(mem[0], mem_w_v[l], tm=MEM_LEN, tn=1024)
    h2, rout = _memattn(h1, w_q, k_mem, v_mem, w_o, row(ln2_g[l]), row(ln2_b[l]), wr_hi, wr_lo, b_r, tm=512)
    flat_e = rout[:, 0:TOP_K].astype(I32).reshape(1, N_ASSIGN)
    row_tok, pos, meta = _dispatch(flat_e)
    blk_start, nblk = meta[:, 0], meta[:, 1]
    total = (blk_start[N_EXPERTS - 1] + nblk[N_EXPERTS - 1]).reshape(1)
    y_rows = _experts(blk_start, nblk, total, row_tok, h2, w1[l], w3[l], w2[l])
    out = _combine(pos.reshape(N_ASSIGN), y_rows, h2, rout, row(ln3_g[l]), row(ln3_b[l]), tc=256)
    return out.reshape(1, SEQ, D_MODEL)
```

```python
import functools
import math

import jax
import jax.numpy as jnp
from jax import lax
from jax.experimental import pallas as pl
from jax.experimental.pallas import tpu as pltpu

D_MODEL = 2048
SEQ = 8192
CHUNK = 64
MEM_LEN = 256
GM_BLOCK = 128
GM_GROUPS = 8
GM_WIDTH = 1024
SB_HEADS = 8
SB_HEAD_DIM = 128
SB_WIDTH = 1024
MEM_HEADS = 4
MEM_HEAD_DIM = 512
N_GROUPS = 8
EXPERTS_PER_GROUP = 8
N_EXPERTS = 64
TOP_K = 2
D_EXPERT = 512
DN_ALPHA = 2.0 ** 0.25
LN_EPS = 1e-5

LANES = 128
MOE_ROWS = 128
MOE_ROWS_LOG2 = 7
N_ASSIGN = SEQ * TOP_K
N_ROW_BLOCKS = N_ASSIGN // MOE_ROWS + N_EXPERTS
N_ROWS = N_ROW_BLOCKS * MOE_ROWS
DISPATCH_CHUNK = 512
GATHER_SLOTS = 3
WEIGHT_SLOTS = 3
COMBINE_CHUNK = 32
SB_UNDERFLOW = 160.0
SB_LOOP_BLOCKS = 1
SB_STATIC_BELOW = 2
VMEM_LIMIT = 56 * 1024 * 1024

BF16 = jnp.bfloat16
F32 = jnp.float32
I32 = jnp.int32


def _cparams(sem):
    return pltpu.CompilerParams(dimension_semantics=sem, vmem_limit_bytes=VMEM_LIMIT)


def _resident(shape, index_map):
    return pl.BlockSpec(shape, index_map, pipeline_mode=pl.Buffered(1))


def _layer_norm(y, g, b):
    mu = jnp.mean(y, axis=-1, keepdims=True)
    yc = y - mu
    var = jnp.mean(yc * yc, axis=-1, keepdims=True)
    return yc * lax.rsqrt(var + LN_EPS) * g + b


def _split_bf16(v):
    hi = v.astype(BF16)
    lo = (v - hi.astype(F32)).astype(BF16)
    return hi, lo


def _proj_kernel(x_ref, w_ref, o_ref, xb_ref):
    @pl.when(pl.program_id(1) == 0)
    def _():
        xb_ref[...] = x_ref[...].astype(BF16)

    o_ref[...] = jnp.dot(xb_ref[...], w_ref[...].astype(BF16),
                         preferred_element_type=F32).astype(o_ref.dtype)


def _proj(x, w, tm, tn):
    m, k = x.shape
    n = w.shape[1]
    return pl.pallas_call(
        _proj_kernel,
        grid=(m // tm, n // tn),
        in_specs=[pl.BlockSpec((tm, k), lambda i, j: (i, 0)),
                  pl.BlockSpec((k, tn), lambda i, j: (0, j))],
        out_specs=pl.BlockSpec((tm, tn), lambda i, j: (i, j)),
        out_shape=jax.ShapeDtypeStruct((m, n), BF16),
        scratch_shapes=[pltpu.VMEM((tm, k), BF16)],
        compiler_params=_cparams(("arbitrary", "arbitrary")),
        name="proj",
    )(x, w)


IN_BLOCKS = 5
IN_Q, IN_K, IN_V = 2, 3, 4


def _in_proj_kernel(x_ref, w_ref, o_ref, ot_ref, xb_ref, *, q_scale):
    j = pl.program_id(1)

    @pl.when(j == 0)
    def _():
        xb_ref[...] = x_ref[...].astype(BF16)

    wb = w_ref[...].astype(BF16)
    is_t = jnp.logical_or(j == IN_Q, j == IN_V)

    @pl.when(jnp.logical_not(is_t))
    def _():
        o_ref[...] = jnp.dot(xb_ref[...], wb, preferred_element_type=F32).astype(o_ref.dtype)

    @pl.when(is_t)
    def _():
        acc = lax.dot_general(wb, xb_ref[...], (((0,), (1,)), ((), ())), preferred_element_type=F32)
        ot_ref[...] = (acc * jnp.where(j == IN_Q, q_scale, 1.0).astype(F32)).astype(ot_ref.dtype)


def _in_proj(x, w, tm, q_scale):
    m, k = x.shape
    tn = SB_WIDTH
    omap = lambda i, j: (i, j - (j >= IN_Q).astype(I32) - (j >= IN_V).astype(I32))
    otmap = lambda i, j: ((j >= IN_V).astype(I32), i)
    return pl.pallas_call(
        functools.partial(_in_proj_kernel, q_scale=q_scale),
        grid=(m // tm, IN_BLOCKS),
        in_specs=[pl.BlockSpec((tm, k), lambda i, j: (i, 0)),
                  pl.BlockSpec((k, tn), lambda i, j: (0, j))],
        out_specs=[pl.BlockSpec((tm, tn), omap),
                   pl.BlockSpec((tn, tm), otmap)],
        out_shape=[jax.ShapeDtypeStruct((m, 3 * tn), BF16),
                   jax.ShapeDtypeStruct((2 * tn, m), BF16)],
        scratch_shapes=[pltpu.VMEM((tm, k), BF16)],
        compiler_params=_cparams(("arbitrary", "arbitrary")),
        name="in_proj",
    )(x, w)


def _gmlp_kernel(u_ref, v_ref, ws_ref, g_ref, b_ref, bs_ref, o_ref, *, tg):
    t_chunk = lax.broadcasted_iota(I32, (GM_BLOCK, GM_BLOCK), 0) // CHUNK
    s_chunk = lax.broadcasted_iota(I32, (GM_BLOCK, GM_BLOCK), 1) // CHUNK
    causal = t_chunk >= s_chunk
    for g in range(GM_GROUPS):
        wm = jnp.where(causal, ws_ref[g], 0.0).astype(BF16)
        ln_g = g_ref[g:g + 1, :]
        ln_b = b_ref[g:g + 1, :]
        bias = bs_ref[:, g:g + 1]
        cols = slice(g * LANES, (g + 1) * LANES)
        for n in range(tg // GM_BLOCK):
            rows = slice(n * GM_BLOCK, (n + 1) * GM_BLOCK)
            v = jax.nn.gelu(v_ref[rows, cols].astype(F32))
            vg = _layer_norm(v, ln_g, ln_b).astype(BF16)
            mixed = jnp.dot(wm, vg, preferred_element_type=F32) + bias
            u = jax.nn.gelu(u_ref[rows, cols].astype(F32))
            o_ref[rows, cols] = (u * mixed).astype(o_ref.dtype)


def _gmlp(uvk, gm_w_s, gm_ln_g, gm_ln_b, gm_b_s_t, tg):
    s = uvk.shape[0]
    return pl.pallas_call(
        functools.partial(_gmlp_kernel, tg=tg),
        grid=(s // tg,),
        in_specs=[pl.BlockSpec((tg, GM_WIDTH), lambda i: (i, 0)),
                  pl.BlockSpec((tg, GM_WIDTH), lambda i: (i, 1)),
                  pl.BlockSpec((GM_GROUPS, GM_BLOCK, GM_BLOCK), lambda i: (0, 0, 0)),
                  pl.BlockSpec((GM_GROUPS, LANES), lambda i: (0, 0)),
                  pl.BlockSpec((GM_GROUPS, LANES), lambda i: (0, 0)),
                  pl.BlockSpec((GM_BLOCK, GM_GROUPS), lambda i: (0, 0))],
        out_specs=pl.BlockSpec((tg, GM_WIDTH), lambda i: (i, 0)),
        out_shape=jax.ShapeDtypeStruct((s, GM_WIDTH), BF16),
        compiler_params=_cparams(("arbitrary",)),
        name="gmlp",
    )(uvk, uvk, gm_w_s, gm_ln_g, gm_ln_b, gm_b_s_t)


def _softplus2(u):
    return jnp.maximum(u, 0.0) + jnp.log2(1.0 + jnp.exp2(-jnp.abs(u)))


def _sb_kernel(qt_ref, k_ref, vt_ref, o_ref, *acc_refs, tq, bk):
    q0 = pl.program_id(0) * tq
    upper = (lax.broadcasted_iota(I32, (bk, bk), 1)
             >= lax.broadcasted_iota(I32, (bk, bk), 0)).astype(BF16)
    upper2 = jnp.concatenate([upper, upper], axis=1)

    heads = range(SB_HEADS)
    hrows = [slice(h * SB_HEAD_DIM, (h + 1) * SB_HEAD_DIM) for h in heads]

    def tiles(blocks, carries):
        zs, sums, masks = [], [], []
        for ks, lo, masked in blocks:
            width = tq - lo
            zs.append([jnp.dot(k_ref[pl.ds(ks, bk), hrows[h]], qt_ref[hrows[h], lo:],
                               preferred_element_type=F32) for h in heads])
            if masked is True:
                masks.append(lax.broadcasted_iota(I32, (bk, width), 0)
                             < lax.broadcasted_iota(I32, (bk, width), 1))
            elif masked is False:
                masks.append(None)
            else:
                masks.append(jnp.broadcast_to(masked, (bk, width)))
        for zb, mask in zip(zs, masks):
            sps = [_softplus2(z) for z in zb]
            if mask is not None:
                sps = [jnp.where(mask, sp, 0.0) for sp in sps]
            sums.append([jnp.dot(upper2, jnp.concatenate(_split_bf16(sp), axis=0),
                                 preferred_element_type=F32) for sp in sps])
        csums = []
        for (ks, lo, masked), sb in zip(blocks, sums):
            cb = [sb[h] + carries[h][:, lo:] for h in heads]
            carries = [cb[h][0:1, :] if lo == 0
                       else jnp.concatenate([carries[h][:, :lo], cb[h][0:1, :]], axis=1)
                       for h in heads]
            csums.append(cb)
        probs = []
        for zb, cb, mask in zip(zs, csums, masks):
            pb = [jnp.exp2(zb[h] - cb[h]) for h in heads]
            if mask is not None:
                pb = [jnp.where(mask, a, 0.0) for a in pb]
            probs.append([a.astype(BF16) for a in pb])
        for (ks, lo, masked), pb in zip(blocks, probs):
            for h in heads:
                acc_refs[h][:, lo:] += jnp.dot(vt_ref[hrows[h], pl.ds(ks, bk)], pb[h],
                                               preferred_element_type=F32)
        return carries

    for acc_ref in acc_refs:
        acc_ref[...] = jnp.zeros_like(acc_ref)
    carries = [jnp.zeros((1, tq), F32) for _ in range(SB_HEADS)]
    blocks = [(pl.multiple_of(q0 + d * bk, bk), d * bk, True) for d in reversed(range(tq // bk))]
    for b in range(1, SB_STATIC_BELOW + 1):
        below = q0 // bk - b
        blocks.append((pl.multiple_of(jnp.maximum(below, 0) * bk, bk), 0, below >= 0))
    carries = tiles(blocks, carries)

    def cond(state):
        j = state[0]
        lowest = functools.reduce(jnp.minimum, state[1:])
        return jnp.logical_and(j >= 0, jnp.min(lowest) < SB_UNDERFLOW)

    def body(state):
        j = state[0]
        blocks = [(pl.multiple_of((j - b) * bk, bk), 0, False) for b in range(SB_LOOP_BLOCKS)]
        return (j - SB_LOOP_BLOCKS,) + tuple(tiles(blocks, list(state[1:])))

    assert (tq // bk) % SB_LOOP_BLOCKS == 0
    lax.while_loop(cond, body, (q0 // bk - 1 - SB_STATIC_BELOW,) + tuple(carries))
    for h in range(SB_HEADS):
        o_ref[:, h * SB_HEAD_DIM:(h + 1) * SB_HEAD_DIM] = acc_refs[h][...].T.astype(o_ref.dtype)


def _stick_breaking(qvt, uvk, tq, bk):
    s = uvk.shape[0]
    return pl.pallas_call(
        functools.partial(_sb_kernel, tq=tq, bk=bk),
        grid=(s // tq,),
        in_specs=[pl.BlockSpec((SB_WIDTH, tq), lambda i: (0, i)),
                  _resident((s, SB_WIDTH), lambda i: (0, 2 * GM_WIDTH // SB_WIDTH)),
                  _resident((SB_WIDTH, s), lambda i: (1, 0))],
        out_specs=pl.BlockSpec((tq, SB_WIDTH), lambda i: (i, 0)),
        out_shape=jax.ShapeDtypeStruct((s, SB_WIDTH), BF16),
        scratch_shapes=[pltpu.VMEM((SB_HEAD_DIM, tq), F32) for _ in range(SB_HEADS)],
        compiler_params=_cparams(("arbitrary",)),
        name="stick_breaking",
    )(qvt, uvk, qvt)


def _mixout_kernel(a_ref, b_ref, wa_ref, wb_ref, x_ref, g_ref, beta_ref, o_ref):
    mixed = (jnp.dot(a_ref[...], wa_ref[...], preferred_element_type=F32)
             + jnp.dot(b_ref[...], wb_ref[...], preferred_element_type=F32))
    y = DN_ALPHA * x_ref[...] + mixed
    o_ref[...] = _layer_norm(y, g_ref[...], beta_ref[...])


def _mixout(mix_a, mix_b, w, x, g, b, tm):
    s = x.shape[0]
    return pl.pallas_call(
        _mixout_kernel,
        grid=(s // tm,),
        in_specs=[pl.BlockSpec((tm, GM_WIDTH), lambda i: (i, 0)),
                  pl.BlockSpec((tm, SB_WIDTH), lambda i: (i, 0)),
                  _resident((GM_WIDTH, D_MODEL), lambda i: (0, 0)),
                  _resident((SB_WIDTH, D_MODEL), lambda i: (1, 0)),
                  pl.BlockSpec((tm, D_MODEL), lambda i: (i, 0)),
                  pl.BlockSpec((1, D_MODEL), lambda i: (0, 0)),
                  pl.BlockSpec((1, D_MODEL), lambda i: (0, 0))],
        out_specs=pl.BlockSpec((tm, D_MODEL), lambda i: (i, 0)),
        out_shape=jax.ShapeDtypeStruct((s, D_MODEL), F32),
        compiler_params=_cparams(("arbitrary",)),
        name="mixout_ln1",
    )(mix_a, mix_b, w, w, x, g, b)


def _route(logits):
    lane = lax.broadcasted_iota(I32, logits.shape, 1)
    neg = jnp.float32(-jnp.inf)
    big = jnp.int32(1 << 20)

    def first_argmax(vals):
        m = jnp.max(vals, axis=-1, keepdims=True)
        idx = jnp.min(jnp.where(vals == m, lane, big), axis=-1, keepdims=True)
        return m, idx

    is_group = jnp.logical_and(lane >= N_EXPERTS, lane < N_EXPERTS + N_GROUPS)
    gl = jnp.where(is_group, logits, neg)
    gmax, glane = first_argmax(gl)
    g_val = 1.0 / jnp.sum(jnp.exp(gl - gmax), axis=-1, keepdims=True)
    gidx = glane - N_EXPERTS
    in_group = jnp.logical_and(lane >= gidx * EXPERTS_PER_GROUP,
                               lane < (gidx + 1) * EXPERTS_PER_GROUP)
    el = jnp.where(in_group, logits, neg)
    m1, i1 = first_argmax(el)
    m2, i2 = first_argmax(jnp.where(lane == i1, neg, el))
    e2 = jnp.exp(m2 - m1)
    gate1 = g_val / (1.0 + e2)
    gate2 = g_val * e2 / (1.0 + e2)
    return jnp.where(lane == 0, i1.astype(F32),
                     jnp.where(lane == 1, i2.astype(F32),
                               jnp.where(lane == 2, gate1,
                                         jnp.where(lane == 3, gate2, 0.0))))


def _memattn_kernel(h_ref, wq_ref, km_ref, vm_ref, wo_ref, g_ref, beta_ref, wrh_ref, wrl_ref, br_ref,
                    o_ref, r_ref):
    h = h_ref[...]
    q = jnp.dot(h.astype(BF16), wq_ref[...], preferred_element_type=F32).astype(BF16)
    outs = []
    for hd in range(MEM_HEADS):
        cols = slice(hd * MEM_HEAD_DIM, (hd + 1) * MEM_HEAD_DIM)
        kh = km_ref[:, cols]
        vh = vm_ref[:, cols]
        s = lax.dot_general(q[:, cols], kh, (((1,), (1,)), ((), ())),
                            preferred_element_type=F32) * (MEM_HEAD_DIM ** -0.5)
        s = s - jnp.max(s, axis=-1, keepdims=True)
        e = jnp.exp(s)
        p = e / jnp.sum(e, axis=-1, keepdims=True)
        outs.append(jnp.dot(p.astype(BF16), vh, preferred_element_type=F32).astype(BF16))
    o = jnp.concatenate(outs, axis=-1)
    attn = jnp.dot(o, wo_ref[...], preferred_element_type=F32)
    h2 = _layer_norm(DN_ALPHA * h + attn, g_ref[...], beta_ref[...])
    o_ref[...] = h2
    h_hi, h_lo = _split_bf16(h2)
    logits = (jnp.dot(h_hi, wrh_ref[...], preferred_element_type=F32)
              + jnp.dot(h_lo, wrh_ref[...], preferred_element_type=F32)
              + jnp.dot(h_hi, wrl_ref[...], preferred_element_type=F32)) + br_ref[...]
    r_ref[...] = _route(logits)


def _memattn(h1, wq, k_mem, v_mem, wo, g, b, wr_hi, wr_lo, b_r, tm):
    s = h1.shape[0]
    return pl.pallas_call(
        _memattn_kernel,
        grid=(s // tm,),
        in_specs=[pl.BlockSpec((tm, D_MODEL), lambda i: (i, 0)),
                  _resident((D_MODEL, D_MODEL), lambda i: (0, 0)),
                  _resident((MEM_LEN, D_MODEL), lambda i: (0, 0)),
                  _resident((MEM_LEN, D_MODEL), lambda i: (0, 0)),
                  _resident((D_MODEL, D_MODEL), lambda i: (0, 0)),
                  pl.BlockSpec((1, D_MODEL), lambda i: (0, 0)),
                  pl.BlockSpec((1, D_MODEL), lambda i: (0, 0)),
                  _resident((D_MODEL, LANES), lambda i: (0, 0)),
                  _resident((D_MODEL, LANES), lambda i: (0, 0)),
                  pl.BlockSpec((1, LANES), lambda i: (0, 0))],
        out_specs=[pl.BlockSpec((tm, D_MODEL), lambda i: (i, 0)),
                   pl.BlockSpec((tm, LANES), lambda i: (i, 0))],
        out_shape=[jax.ShapeDtypeStruct((s, D_MODEL), F32),
                   jax.ShapeDtypeStruct((s, LANES), F32)],
        compiler_params=_cparams(("arbitrary",)),
        name="memattn_ln2_router",
    )(h1, wq, k_mem, v_mem, wo, g, b, wr_hi, wr_lo, b_r)


def _dispatch_kernel(e_ref, tok_ref, pos_ref, meta_ref):
    ca = DISPATCH_CHUNK
    n_chunks = N_ASSIGN // ca
    e_iota = lax.broadcasted_iota(I32, (N_EXPERTS, ca), 0)

    def onehot(c):
        e_c = e_ref[:, pl.ds(pl.multiple_of(c * ca, ca), ca)]
        return (e_iota == e_c).astype(F32)

    counts = lax.fori_loop(
        0, n_chunks, lambda c, acc: acc + jnp.sum(onehot(c), axis=1, keepdims=True),
        jnp.zeros((N_EXPERTS, 1), F32))
    nblk = jnp.right_shift(counts.astype(I32) + (MOE_ROWS - 1), MOE_ROWS_LOG2)
    strict_lower = (lax.broadcasted_iota(I32, (N_EXPERTS, N_EXPERTS), 1)
                    < lax.broadcasted_iota(I32, (N_EXPERTS, N_EXPERTS), 0)).astype(BF16)
    nblk_lanes = jnp.broadcast_to(nblk.astype(F32), (N_EXPERTS, LANES)).astype(BF16)
    blk_start = jnp.dot(strict_lower, nblk_lanes, preferred_element_type=F32)
    pad_start = blk_start[:, 0:1] * MOE_ROWS
    lane = lax.broadcasted_iota(I32, (N_EXPERTS, LANES), 1)
    meta_ref[...] = jnp.where(lane == 0, blk_start.astype(I32),
                              jnp.where(lane == 1, jnp.broadcast_to(nblk, (N_EXPERTS, LANES)), 0))

    earlier = (lax.broadcasted_iota(I32, (ca, ca), 0)
               < lax.broadcasted_iota(I32, (ca, ca), 1)).astype(BF16)
    blk_iota = lax.broadcasted_iota(I32, (N_ROW_BLOCKS, ca), 0)
    row_iota = lax.broadcasted_iota(I32, (MOE_ROWS, ca), 0)
    a_iota = lax.broadcasted_iota(I32, (1, ca), 1)

    def body(c, state):
        seen, acc_hi, acc_lo = state
        oh = onehot(c)
        before = jnp.dot(oh.astype(BF16), earlier, preferred_element_type=F32) + seen
        dest = jnp.sum(oh * (before + pad_start), axis=0, keepdims=True).astype(I32)
        pos_ref[:, pl.ds(pl.multiple_of(c * ca, ca), ca)] = dest
        in_blk = (blk_iota == jnp.right_shift(dest, MOE_ROWS_LOG2)).astype(F32)
        in_row = (row_iota == jnp.bitwise_and(dest, MOE_ROWS - 1)).astype(BF16)
        tok = jnp.right_shift(c * ca + a_iota, 1)
        tok_hi = (jnp.right_shift(tok, 7) + 1).astype(F32)
        tok_lo = jnp.bitwise_and(tok, 127).astype(F32)
        nt = (((1,), (1,)), ((), ()))
        acc_hi = acc_hi + lax.dot_general((in_blk * tok_hi).astype(BF16), in_row, nt,
                                          preferred_element_type=F32)
        acc_lo = acc_lo + lax.dot_general((in_blk * tok_lo).astype(BF16), in_row, nt,
                                          preferred_element_type=F32)
        return seen + jnp.sum(oh, axis=1, keepdims=True), acc_hi, acc_lo

    zeros = jnp.zeros((N_ROW_BLOCKS, MOE_ROWS), F32)
    _, acc_hi, acc_lo = lax.fori_loop(0, n_chunks, body,
                                      (jnp.zeros((N_EXPERTS, 1), F32), zeros, zeros))
    pad_tok = jnp.bitwise_and(
        lax.broadcasted_iota(I32, (N_ROW_BLOCKS, MOE_ROWS), 0) * MOE_ROWS
        + lax.broadcasted_iota(I32, (N_ROW_BLOCKS, MOE_ROWS), 1), SEQ - 1)
    tok_ref[...] = jnp.where(acc_hi > 0.5, ((acc_hi - 1.0) * 128.0 + acc_lo).astype(I32), pad_tok)


def _dispatch(flat_e):
    return pl.pallas_call(
        _dispatch_kernel,
        out_shape=[jax.ShapeDtypeStruct((N_ROW_BLOCKS, MOE_ROWS), I32),
                   jax.ShapeDtypeStruct((1, N_ASSIGN), I32),
                   jax.ShapeDtypeStruct((N_EXPERTS, LANES), I32)],
        compiler_params=pltpu.CompilerParams(vmem_limit_bytes=VMEM_LIMIT),
        name="moe_dispatch",
    )(flat_e)


def _expert_kernel(bs_ref, nb_ref, tot_ref, tok_hbm, h_hbm, w1_hbm, w3_hbm, w2_hbm, y_hbm,
                   idx, xbuf, ybuf, isem, gsem, ysem, wsem, w1f, w3f, w2f, w1b, w3b, w2b):
    e = pl.program_id(0)
    n_exp = pl.num_programs(0)
    total = tot_ref[0]
    first = bs_ref[e]
    nblk = nb_ref[e]

    def idx_copy(blk):
        row = jnp.minimum(blk, N_ROW_BLOCKS - 1)
        slot = blk % GATHER_SLOTS
        return pltpu.make_async_copy(tok_hbm.at[pl.ds(row, 1)], idx.at[pl.ds(slot, 1)],
                                     isem.at[slot])

    def issue_rows(blk, rows):
        slot = blk % GATHER_SLOTS
        for r in rows:
            tok = idx[slot, r]
            pltpu.make_async_copy(h_hbm.at[pl.ds(tok, 1)], xbuf.at[slot, pl.ds(r, 1)],
                                  gsem.at[slot]).start()

    def wait_gather(blk):
        slot = blk % GATHER_SLOTS
        pltpu.make_async_copy(h_hbm.at[pl.ds(0, MOE_ROWS)], xbuf.at[slot], gsem.at[slot]).wait()

    def y_copy(blk, slot):
        return pltpu.make_async_copy(ybuf.at[slot], y_hbm.at[pl.ds(blk * MOE_ROWS, MOE_ROWS)],
                                     ysem.at[slot])

    def weight_copies(ex):
        slot = ex % WEIGHT_SLOTS
        return [pltpu.make_async_copy(src.at[ex], dst.at[slot], wsem.at[slot])
                for src, dst in ((w1_hbm, w1f), (w3_hbm, w3f), (w2_hbm, w2f))]

    @pl.when(e == 0)
    def _():
        for ex in range(WEIGHT_SLOTS - 1):
            for cp in weight_copies(ex):
                cp.start(priority=1)

        for blk in range(GATHER_SLOTS):
            idx_copy(blk).start()
        for blk in range(GATHER_SLOTS - 1):
            idx_copy(blk).wait()

        def prime(r, c):
            for blk in range(GATHER_SLOTS - 1):
                pltpu.make_async_copy(h_hbm.at[pl.ds(idx[blk, r], 1)], xbuf.at[blk, pl.ds(r, 1)],
                                      gsem.at[blk]).start()
            return c
        lax.fori_loop(0, MOE_ROWS, prime, 0, unroll=8)

    @pl.when(e + WEIGHT_SLOTS - 1 < n_exp)
    def _():
        for cp in weight_copies(e + WEIGHT_SLOTS - 1):
            cp.start(priority=1)

    for cp in weight_copies(e):
        cp.wait()

    @pl.when(nblk > 0)
    def _():
        wslot = e % WEIGHT_SLOTS
        w1b[...] = w1f[wslot].astype(BF16)
        w3b[...] = w3f[wslot].astype(BF16)
        w2b[...] = w2f[wslot].astype(BF16)

    n_piece = 4
    rows_per_piece = MOE_ROWS // n_piece
    cols_per_piece = D_MODEL // n_piece

    def block(b, c):
        blk = first + b
        slot = blk % 2
        wait_gather(blk)
        idx_copy(blk + GATHER_SLOTS - 1).wait()
        idx_copy(blk + GATHER_SLOTS).start()

        @pl.when(blk >= 2)
        def _():
            y_copy(blk - 2, slot).wait()

        xb = xbuf[blk % GATHER_SLOTS].astype(BF16)
        a = jnp.dot(xb, w1b[...], preferred_element_type=F32)
        g = jnp.dot(xb, w3b[...], preferred_element_type=F32)
        hid = (a * jax.nn.sigmoid(a) * g).astype(BF16)
        for p in range(n_piece):
            issue_rows(blk + GATHER_SLOTS - 1,
                       range(p * rows_per_piece, (p + 1) * rows_per_piece))
            cols = slice(p * cols_per_piece, (p + 1) * cols_per_piece)
            ybuf[slot, :, cols] = jnp.dot(hid, w2b[:, cols], preferred_element_type=F32)
        y_copy(blk, slot).start()
        return c

    lax.fori_loop(0, nblk, block, 0)

    @pl.when(e == n_exp - 1)
    def _():
        for ahead in range(GATHER_SLOTS - 1):
            wait_gather(total + ahead)
        idx_copy(total + GATHER_SLOTS - 1).wait()
        for back in (1, 2):
            @pl.when(total >= back)
            def _():
                y_copy(total - back, (total - back) % 2).wait()

        ybuf[0] = jnp.zeros((MOE_ROWS, D_MODEL), F32)

        def zero_block(blk, c):
            y_copy(blk, 0).start()
            y_copy(blk, 0).wait()
            return c

        lax.fori_loop(total, N_ROW_BLOCKS, zero_block, 0)


def _experts(blk_start, nblk, total, row_tok, h2, w1, w3, w2):
    grid_spec = pltpu.PrefetchScalarGridSpec(
        num_scalar_prefetch=3,
        grid=(N_EXPERTS,),
        in_specs=[pl.BlockSpec(memory_space=pl.ANY)] * 5,
        out_specs=pl.BlockSpec(memory_space=pl.ANY),
        scratch_shapes=[pltpu.SMEM((GATHER_SLOTS, MOE_ROWS), I32),
                        pltpu.VMEM((GATHER_SLOTS, MOE_ROWS, D_MODEL), F32),
                        pltpu.VMEM((2, MOE_ROWS, D_MODEL), F32),
                        pltpu.SemaphoreType.DMA((GATHER_SLOTS,)),
                        pltpu.SemaphoreType.DMA((GATHER_SLOTS,)),
                        pltpu.SemaphoreType.DMA((2,)),
                        pltpu.SemaphoreType.DMA((WEIGHT_SLOTS,)),
                        pltpu.VMEM((WEIGHT_SLOTS, D_MODEL, D_EXPERT), F32),
                        pltpu.VMEM((WEIGHT_SLOTS, D_MODEL, D_EXPERT), F32),
                        pltpu.VMEM((WEIGHT_SLOTS, D_EXPERT, D_MODEL), F32),
                        pltpu.VMEM((D_MODEL, D_EXPERT), BF16),
                        pltpu.VMEM((D_MODEL, D_EXPERT), BF16),
                        pltpu.VMEM((D_EXPERT, D_MODEL), BF16)],
    )
    return pl.pallas_call(
        _expert_kernel,
        grid_spec=grid_spec,
        out_shape=jax.ShapeDtypeStruct((N_ROWS, D_MODEL), F32),
        compiler_params=_cparams(("arbitrary",)),
        name="experts",
    )(blk_start, nblk, total, row_tok, h2, w1, w3, w2)


def _combine_kernel(y_hbm, pos_ref, pos_next_ref, h_ref, r_ref, g_ref, beta_ref, o_ref, ybuf, sem,
                    *, tc):
    i = pl.program_id(0)
    n = pl.num_programs(0)
    slot = i % 2

    def issue_rows(table, slot_, rows):
        for r in rows:
            for kk in range(TOP_K):
                p = table[0, 0, r * TOP_K + kk]
                pltpu.make_async_copy(y_hbm.at[pl.ds(p, 1)], ybuf.at[slot_, kk, pl.ds(r, 1)],
                                      sem.at[slot_]).start(priority=kk)

    def wait_rows(slot_):
        for kk in range(TOP_K):
            pltpu.make_async_copy(y_hbm.at[pl.ds(0, tc)], ybuf.at[slot_, kk], sem.at[slot_]).wait()

    @pl.when(i == 0)
    def _():
        issue_rows(pos_ref, 0, range(tc))

    def step(slot_):
        wait_rows(slot_)
        for c in range(tc // COMBINE_CHUNK):
            rows = slice(c * COMBINE_CHUNK, (c + 1) * COMBINE_CHUNK)
            issue_rows(pos_next_ref, 1 - slot_, range(c * COMBINE_CHUNK, (c + 1) * COMBINE_CHUNK))
            gates = r_ref[rows, :]
            moe = (ybuf[slot_, 0, rows, :] * gates[:, TOP_K:TOP_K + 1]
                   + ybuf[slot_, 1, rows, :] * gates[:, TOP_K + 1:TOP_K + 2])
            o_ref[rows, :] = _layer_norm(DN_ALPHA * h_ref[rows, :] + moe, g_ref[...], beta_ref[...])

    for parity in range(2):
        @pl.when(slot == parity)
        def _():
            step(parity)

    @pl.when(i == n - 1)
    def _():
        for parity in range(2):
            @pl.when(slot == parity)
            def _():
                wait_rows(1 - parity)


def _combine(pos, y_rows, h2, rout, g, b, tc):
    s = h2.shape[0]
    n = s // tc
    pos = pos.reshape(n, 1, tc * TOP_K)
    smem_block = functools.partial(pl.BlockSpec, (1, 1, tc * TOP_K), memory_space=pltpu.SMEM)
    return pl.pallas_call(
        functools.partial(_combine_kernel, tc=tc),
        grid=(n,),
        in_specs=[pl.BlockSpec(memory_space=pl.ANY),
                  smem_block(lambda i: (i, 0, 0)),
                  smem_block(lambda i: (jnp.minimum(i + 1, n - 1), 0, 0)),
                  pl.BlockSpec((tc, D_MODEL), lambda i: (i, 0)),
                  pl.BlockSpec((tc, LANES), lambda i: (i, 0)),
                  pl.BlockSpec((1, D_MODEL), lambda i: (0, 0)),
                  pl.BlockSpec((1, D_MODEL), lambda i: (0, 0))],
        out_specs=pl.BlockSpec((tc, D_MODEL), lambda i: (i, 0)),
        out_shape=jax.ShapeDtypeStruct((s, D_MODEL), F32),
        scratch_shapes=[pltpu.VMEM((2, TOP_K, tc, D_MODEL), F32),
                        pltpu.SemaphoreType.DMA((2,))],
        compiler_params=_cparams(("arbitrary",)),
        name="combine_ln3",
    )(y_rows, pos, pos, h2, rout, g, b)


def kernel(x, mem, w_in, gm_ln_g, gm_ln_b, gm_w_s, gm_b_s, w_mix_out, ln1_g, ln1_b,
           mem_w_q, mem_w_k, mem_w_v, mem_w_o, ln2_g, ln2_b,
           w_group, b_group, w_router, b_router, w1, w3, w2, ln3_g, ln3_b):
    assert x.shape == (1, SEQ, D_MODEL) and w_in.shape[0] == 1
    xs = x[0]
    l = 0
    w_mix = w_mix_out[l].astype(BF16)
    w_q = mem_w_q[l].astype(BF16)
    w_o = mem_w_o[l].astype(BF16)
    w_r = jnp.concatenate(
        [w_router[l].transpose(1, 0, 2).reshape(D_MODEL, N_EXPERTS), w_group[l],
         jnp.zeros((D_MODEL, LANES - N_EXPERTS - N_GROUPS), F32)], axis=1)
    wr_hi, wr_lo = _split_bf16(w_r)
    b_r = jnp.concatenate([b_router[l].reshape(-1), b_group[l],
                           jnp.zeros((LANES - N_EXPERTS - N_GROUPS,), F32)]).reshape(1, LANES)
    row = lambda v: v.reshape(1, D_MODEL)

    uvk, qvt = _in_proj(xs, w_in[l], tm=1024, q_scale=SB_HEAD_DIM ** -0.5 * math.log2(math.e))
    mix_a = _gmlp(uvk, gm_w_s[l], gm_ln_g[l], gm_ln_b[l], gm_b_s[l].T, tg=512)
    mix_b = _stick_breaking(qvt, uvk, tq=256, bk=128)
    h1 = _mixout(mix_a, mix_b, w_mix, xs, row(ln1_g[l]), row(ln1_b[l]), tm=512)
    k_mem = _proj(mem[0], mem_w_k[l], tm=MEM_LEN, tn=1024)
    v_mem = _proj(mem[0], mem_w_v[l], tm=MEM_LEN, tn=1024)
    h2, rout = _memattn(h1, w_q, k_mem, v_mem, w_o, row(ln2_g[l]), row(ln2_b[l]), wr_hi, wr_lo, b_r, tm=512)
    flat_e = rout[:, 0:TOP_K].astype(I32).reshape(1, N_ASSIGN)
    row_tok, pos, meta = _dispatch(flat_e)
    blk_start, nblk = meta[:, 0], meta[:, 1]
    total = (blk_start[N_EXPERTS - 1] + nblk[N_EXPERTS - 1]).reshape(1)
    y_rows = _experts(blk_start, nblk, total, row_tok, h2, w1[l], w3[l], w2[l])
    out = _combine(pos.reshape(N_ASSIGN), y_rows, h2, rout, row(ln3_g[l]), row(ln3_b[l]), tc=256)
    return out.reshape(1, SEQ, D_MODEL)
```

```python
import functools
import math

import jax
import jax.numpy as jnp
from jax import lax
from jax.experimental import pallas as pl
from jax.experimental.pallas import tpu as pltpu

D_MODEL = 2048
SEQ = 8192
CHUNK = 64
MEM_LEN = 256
GM_BLOCK = 128
GM_GROUPS = 8
GM_WIDTH = 1024
SB_HEADS = 8
SB_HEAD_DIM = 128
SB_WIDTH = 1024
MEM_HEADS = 4
MEM_HEAD_DIM = 512
N_GROUPS = 8
EXPERTS_PER_GROUP = 8
N_EXPERTS = 64
TOP_K = 2
D_EXPERT = 512
DN_ALPHA = 2.0 ** 0.25
LN_EPS = 1e-5

LANES = 128
MOE_ROWS = 128
MOE_ROWS_LOG2 = 7
N_ASSIGN = SEQ * TOP_K
N_ROW_BLOCKS = N_ASSIGN // MOE_ROWS + N_EXPERTS
N_ROWS = N_ROW_BLOCKS * MOE_ROWS
DISPATCH_CHUNK = 512
GATHER_SLOTS = 3
WEIGHT_SLOTS = 3
COMBINE_CHUNK = 32
SB_UNDERFLOW = 160.0
SB_LOOP_BLOCKS = 1
SB_STATIC_BELOW = 2
VMEM_LIMIT = 56 * 1024 * 1024

BF16 = jnp.bfloat16
F32 = jnp.float32
I32 = jnp.int32


def _cparams(sem):
    return pltpu.CompilerParams(dimension_semantics=sem, vmem_limit_bytes=VMEM_LIMIT)


def _resident(shape, index_map):
    return pl.BlockSpec(shape, index_map, pipeline_mode=pl.Buffered(1))


def _layer_norm(y, g, b):
    mu = jnp.mean(y, axis=-1, keepdims=True)
    yc = y - mu
    var = jnp.mean(yc * yc, axis=-1, keepdims=True)
    return yc * lax.rsqrt(var + LN_EPS) * g + b


def _split_bf16(v):
    hi = v.astype(BF16)
    lo = (v - hi.astype(F32)).astype(BF16)
    return hi, lo


def _proj_kernel(x_ref, w_ref, o_ref, xb_ref):
    @pl.when(pl.program_id(1) == 0)
    def _():
        xb_ref[...] = x_ref[...].astype(BF16)

    o_ref[...] = jnp.dot(xb_ref[...], w_ref[...].astype(BF16),
                         preferred_element_type=F32).astype(o_ref.dtype)


def _proj(x, w, tm, tn):
    m, k = x.shape
    n = w.shape[1]
    return pl.pallas_call(
        _proj_kernel,
        grid=(m // tm, n // tn),
        in_specs=[pl.BlockSpec((tm, k), lambda i, j: (i, 0)),
                  pl.BlockSpec((k, tn), lambda i, j: (0, j))],
        out_specs=pl.BlockSpec((tm, tn), lambda i, j: (i, j)),
        out_shape=jax.ShapeDtypeStruct((m, n), BF16),
        scratch_shapes=[pltpu.VMEM((tm, k), BF16)],
        compiler_params=_cparams(("arbitrary", "arbitrary")),
        name="proj",
    )(x, w)


IN_BLOCKS = 5
IN_Q, IN_K, IN_V = 2, 3, 4


def _in_proj_kernel(x_ref, w_ref, o_ref, ot_ref, xb_ref, *, q_scale):
    j = pl.program_id(1)

    @pl.when(j == 0)
    def _():
        xb_ref[...] = x_ref[...].astype(BF16)

    wb = w_ref[...].astype(BF16)
    is_t = jnp.logical_or(j == IN_Q, j == IN_V)

    @pl.when(jnp.logical_not(is_t))
    def _():
        o_ref[...] = jnp.dot(xb_ref[...], wb, preferred_element_type=F32).astype(o_ref.dtype)

    @pl.when(is_t)
    def _():
        acc = lax.dot_general(wb, xb_ref[...], (((0,), (1,)), ((), ())), preferred_element_type=F32)
        ot_ref[...] = (acc * jnp.where(j == IN_Q, q_scale, 1.0).astype(F32)).astype(ot_ref.dtype)


def _in_proj(x, w, tm, q_scale):
    m, k = x.shape
    tn = SB_WIDTH
    omap = lambda i, j: (i, j - (j >= IN_Q).astype(I32) - (j >= IN_V).astype(I32))
    otmap = lambda i, j: ((j >= IN_V).astype(I32), i)
    return pl.pallas_call(
        functools.partial(_in_proj_kernel, q_scale=q_scale),
        grid=(m // tm, IN_BLOCKS),
        in_specs=[pl.BlockSpec((tm, k), lambda i, j: (i, 0)),
                  pl.BlockSpec((k, tn), lambda i, j: (0, j))],
        out_specs=[pl.BlockSpec((tm, tn), omap),
                   pl.BlockSpec((tn, tm), otmap)],
        out_shape=[jax.ShapeDtypeStruct((m, 3 * tn), BF16),
                   jax.ShapeDtypeStruct((2 * tn, m), BF16)],
        scratch_shapes=[pltpu.VMEM((tm, k), BF16)],
        compiler_params=_cparams(("arbitrary", "arbitrary")),
        name="in_proj",
    )(x, w)


def _gelu_tanh(x):
    c1 = math.sqrt(2.0 / math.pi)
    return x * (0.5 + 0.5 * jnp.tanh(x * (c1 + (0.044715 * c1) * (x * x))))


def _gmlp_kernel(u_ref, v_ref, ws_ref, g_ref, b_ref, bs_ref, o_ref, *, tg):
    t_chunk = lax.broadcasted_iota(I32, (GM_BLOCK, GM_BLOCK), 0) // CHUNK
    s_chunk = lax.broadcasted_iota(I32, (GM_BLOCK, GM_BLOCK), 1) // CHUNK
    causal = t_chunk >= s_chunk
    for g in range(GM_GROUPS):
        wm = jnp.where(causal, ws_ref[g], 0.0).astype(BF16)
        ln_g = g_ref[g:g + 1, :]
        ln_b = b_ref[g:g + 1, :]
        bias = bs_ref[:, g:g + 1]
        cols = slice(g * LANES, (g + 1) * LANES)
        for n in range(tg // GM_BLOCK):
            rows = slice(n * GM_BLOCK, (n + 1) * GM_BLOCK)
            v = _gelu_tanh(v_ref[rows, cols].astype(F32))
            vg = _layer_norm(v, ln_g, ln_b).astype(BF16)
            mixed = jnp.dot(wm, vg, preferred_element_type=F32) + bias
            u = _gelu_tanh(u_ref[rows, cols].astype(F32))
            o_ref[rows, cols] = (u * mixed).astype(o_ref.dtype)


def _gmlp(uvk, gm_w_s, gm_ln_g, gm_ln_b, gm_b_s_t, tg):
    s = uvk.shape[0]
    return pl.pallas_call(
        functools.partial(_gmlp_kernel, tg=tg),
        grid=(s // tg,),
        in_specs=[pl.BlockSpec((tg, GM_WIDTH), lambda i: (i, 0)),
                  pl.BlockSpec((tg, GM_WIDTH), lambda i: (i, 1)),
                  pl.BlockSpec((GM_GROUPS, GM_BLOCK, GM_BLOCK), lambda i: (0, 0, 0)),
                  pl.BlockSpec((GM_GROUPS, LANES), lambda i: (0, 0)),
                  pl.BlockSpec((GM_GROUPS, LANES), lambda i: (0, 0)),
                  pl.BlockSpec((GM_BLOCK, GM_GROUPS), lambda i: (0, 0))],
        out_specs=pl.BlockSpec((tg, GM_WIDTH), lambda i: (i, 0)),
        out_shape=jax.ShapeDtypeStruct((s, GM_WIDTH), BF16),
        compiler_params=_cparams(("arbitrary",)),
        name="gmlp",
    )(uvk, uvk, gm_w_s, gm_ln_g, gm_ln_b, gm_b_s_t)


def _softplus2(u):
    return jnp.maximum(u, 0.0) + jnp.log2(1.0 + jnp.exp2(-jnp.abs(u)))


def _sb_kernel(qt_ref, k_ref, vt_ref, o_ref, *acc_refs, tq, bk):
    q0 = pl.program_id(0) * tq
    upper = (lax.broadcasted_iota(I32, (bk, bk), 1)
             >= lax.broadcasted_iota(I32, (bk, bk), 0)).astype(BF16)
    upper2 = jnp.concatenate([upper, upper], axis=1)

    heads = range(SB_HEADS)
    hrows = [slice(h * SB_HEAD_DIM, (h + 1) * SB_HEAD_DIM) for h in heads]

    def tiles(blocks, carries):
        zs, sums, masks = [], [], []
        for ks, lo, masked in blocks:
            width = tq - lo
            zs.append([jnp.dot(k_ref[pl.ds(ks, bk), hrows[h]], qt_ref[hrows[h], lo:],
                               preferred_element_type=F32) for h in heads])
            if masked is True:
                masks.append(lax.broadcasted_iota(I32, (bk, width), 0)
                             < lax.broadcasted_iota(I32, (bk, width), 1))
            elif masked is False:
                masks.append(None)
            else:
                masks.append(jnp.broadcast_to(masked, (bk, width)))
        for zb, mask in zip(zs, masks):
            sps = [_softplus2(z) for z in zb]
            if mask is not None:
                sps = [jnp.where(mask, sp, 0.0) for sp in sps]
            sums.append([jnp.dot(upper2, jnp.concatenate(_split_bf16(sp), axis=0),
                                 preferred_element_type=F32) for sp in sps])
        csums = []
        for (ks, lo, masked), sb in zip(blocks, sums):
            cb = [sb[h] + carries[h][:, lo:] for h in heads]
            carries = [cb[h][0:1, :] if lo == 0
                       else jnp.concatenate([carries[h][:, :lo], cb[h][0:1, :]], axis=1)
                       for h in heads]
            csums.append(cb)
        probs = []
        for zb, cb, mask in zip(zs, csums, masks):
            pb = [jnp.exp2(zb[h] - cb[h]) for h in heads]
            if mask is not None:
                pb = [jnp.where(mask, a, 0.0) for a in pb]
            probs.append([a.astype(BF16) for a in pb])
        for (ks, lo, masked), pb in zip(blocks, probs):
            for h in heads:
                acc_refs[h][:, lo:] += jnp.dot(vt_ref[hrows[h], pl.ds(ks, bk)], pb[h],
                                               preferred_element_type=F32)
        return carries

    for acc_ref in acc_refs:
        acc_ref[...] = jnp.zeros_like(acc_ref)
    carries = [jnp.zeros((1, tq), F32) for _ in range(SB_HEADS)]
    blocks = [(pl.multiple_of(q0 + d * bk, bk), d * bk, True) for d in reversed(range(tq // bk))]
    for b in range(1, SB_STATIC_BELOW + 1):
        below = q0 // bk - b
        blocks.append((pl.multiple_of(jnp.maximum(below, 0) * bk, bk), 0, below >= 0))
    carries = tiles(blocks, carries)

    def cond(state):
        j = state[0]
        lowest = functools.reduce(jnp.minimum, state[1:])
        return jnp.logical_and(j >= 0, jnp.min(lowest) < SB_UNDERFLOW)

    def body(state):
        j = state[0]
        blocks = [(pl.multiple_of((j - b) * bk, bk), 0, False) for b in range(SB_LOOP_BLOCKS)]
        return (j - SB_LOOP_BLOCKS,) + tuple(tiles(blocks, list(state[1:])))

    assert (tq // bk) % SB_LOOP_BLOCKS == 0
    lax.while_loop(cond, body, (q0 // bk - 1 - SB_STATIC_BELOW,) + tuple(carries))
    for h in range(SB_HEADS):
        o_ref[:, h * SB_HEAD_DIM:(h + 1) * SB_HEAD_DIM] = acc_refs[h][...].T.astype(o_ref.dtype)


def _stick_breaking(qvt, uvk, tq, bk):
    s = uvk.shape[0]
    return pl.pallas_call(
        functools.partial(_sb_kernel, tq=tq, bk=bk),
        grid=(s // tq,),
        in_specs=[pl.BlockSpec((SB_WIDTH, tq), lambda i: (0, i)),
                  _resident((s, SB_WIDTH), lambda i: (0, 2 * GM_WIDTH // SB_WIDTH)),
                  _resident((SB_WIDTH, s), lambda i: (1, 0))],
        out_specs=pl.BlockSpec((tq, SB_WIDTH), lambda i: (i, 0)),
        out_shape=jax.ShapeDtypeStruct((s, SB_WIDTH), BF16),
        scratch_shapes=[pltpu.VMEM((SB_HEAD_DIM, tq), F32) for _ in range(SB_HEADS)],
        compiler_params=_cparams(("arbitrary",)),
        name="stick_breaking",
    )(qvt, uvk, qvt)


def _mixout_kernel(a_ref, b_ref, wa_ref, wb_ref, x_ref, g_ref, beta_ref, o_ref):
    mixed = (jnp.dot(a_ref[...], wa_ref[...], preferred_element_type=F32)
             + jnp.dot(b_ref[...], wb_ref[...], preferred_element_type=F32))
    y = DN_ALPHA * x_ref[...] + mixed
    o_ref[...] = _layer_norm(y, g_ref[...], beta_ref[...])


def _mixout(mix_a, mix_b, w, x, g, b, tm):
    s = x.shape[0]
    return pl.pallas_call(
        _mixout_kernel,
        grid=(s // tm,),
        in_specs=[pl.BlockSpec((tm, GM_WIDTH), lambda i: (i, 0)),
                  pl.BlockSpec((tm, SB_WIDTH), lambda i: (i, 0)),
                  _resident((GM_WIDTH, D_MODEL), lambda i: (0, 0)),
                  _resident((SB_WIDTH, D_MODEL), lambda i: (1, 0)),
                  pl.BlockSpec((tm, D_MODEL), lambda i: (i, 0)),
                  pl.BlockSpec((1, D_MODEL), lambda i: (0, 0)),
                  pl.BlockSpec((1, D_MODEL), lambda i: (0, 0))],
        out_specs=pl.BlockSpec((tm, D_MODEL), lambda i: (i, 0)),
        out_shape=jax.ShapeDtypeStruct((s, D_MODEL), F32),
        compiler_params=_cparams(("arbitrary",)),
        name="mixout_ln1",
    )(mix_a, mix_b, w, w, x, g, b)


def _route(logits):
    lane = lax.broadcasted_iota(I32, logits.shape, 1)
    neg = jnp.float32(-jnp.inf)
    big = jnp.int32(1 << 20)

    def first_argmax(vals):
        m = jnp.max(vals, axis=-1, keepdims=True)
        idx = jnp.min(jnp.where(vals == m, lane, big), axis=-1, keepdims=True)
        return m, idx

    is_group = jnp.logical_and(lane >= N_EXPERTS, lane < N_EXPERTS + N_GROUPS)
    gl = jnp.where(is_group, logits, neg)
    gmax, glane = first_argmax(gl)
    g_val = 1.0 / jnp.sum(jnp.exp(gl - gmax), axis=-1, keepdims=True)
    gidx = glane - N_EXPERTS
    in_group = jnp.logical_and(lane >= gidx * EXPERTS_PER_GROUP,
                               lane < (gidx + 1) * EXPERTS_PER_GROUP)
    el = jnp.where(in_group, logits, neg)
    m1, i1 = first_argmax(el)
    m2, i2 = first_argmax(jnp.where(lane == i1, neg, el))
    e2 = jnp.exp(m2 - m1)
    gate1 = g_val / (1.0 + e2)
    gate2 = g_val * e2 / (1.0 + e2)
    return jnp.where(lane == 0, i1.astype(F32),
                     jnp.where(lane == 1, i2.astype(F32),
                               jnp.where(lane == 2, gate1,
                                         jnp.where(lane == 3, gate2, 0.0))))


def _memattn_kernel(h_ref, wq_ref, km_ref, vm_ref, wo_ref, g_ref, beta_ref, wr_ref, br_ref,
                    o_ref, r_ref):
    h = h_ref[...]
    q = jnp.dot(h.astype(BF16), wq_ref[...], preferred_element_type=F32).astype(BF16)
    outs = []
    for hd in range(MEM_HEADS):
        cols = slice(hd * MEM_HEAD_DIM, (hd + 1) * MEM_HEAD_DIM)
        kh = km_ref[:, cols]
        vh = vm_ref[:, cols]
        s = lax.dot_general(q[:, cols], kh, (((1,), (1,)), ((), ())),
                            preferred_element_type=F32) * (MEM_HEAD_DIM ** -0.5)
        s = s - jnp.max(s, axis=-1, keepdims=True)
        e = jnp.exp(s)
        p = e * (1.0 / jnp.sum(e, axis=-1, keepdims=True))
        outs.append(jnp.dot(p.astype(BF16), vh, preferred_element_type=F32).astype(BF16))
    o = jnp.concatenate(outs, axis=-1)
    attn = jnp.dot(o, wo_ref[...], preferred_element_type=F32)
    h2 = _layer_norm(DN_ALPHA * h + attn, g_ref[...], beta_ref[...])
    o_ref[...] = h2
    h_hi, h_lo = _split_bf16(h2)
    both = jnp.dot(h_hi, wr_ref[...], preferred_element_type=F32)
    logits = (both[:, :LANES] + both[:, LANES:]
              + jnp.dot(h_lo, wr_ref[:, :LANES], preferred_element_type=F32)) + br_ref[...]
    r_ref[...] = _route(logits)


def _memattn(h1, wq, k_mem, v_mem, wo, g, b, wr_hi_lo, b_r, tm):
    s = h1.shape[0]
    return pl.pallas_call(
        _memattn_kernel,
        grid=(s // tm,),
        in_specs=[pl.BlockSpec((tm, D_MODEL), lambda i: (i, 0)),
                  _resident((D_MODEL, D_MODEL), lambda i: (0, 0)),
                  _resident((MEM_LEN, D_MODEL), lambda i: (0, 0)),
                  _resident((MEM_LEN, D_MODEL), lambda i: (0, 0)),
                  _resident((D_MODEL, D_MODEL), lambda i: (0, 0)),
                  pl.BlockSpec((1, D_MODEL), lambda i: (0, 0)),
                  pl.BlockSpec((1, D_MODEL), lambda i: (0, 0)),
                  _resident((D_MODEL, 2 * LANES), lambda i: (0, 0)),
                  pl.BlockSpec((1, LANES), lambda i: (0, 0))],
        out_specs=[pl.BlockSpec((tm, D_MODEL), lambda i: (i, 0)),
                   pl.BlockSpec((tm, LANES), lambda i: (i, 0))],
        out_shape=[jax.ShapeDtypeStruct((s, D_MODEL), F32),
                   jax.ShapeDtypeStruct((s, LANES), F32)],
        compiler_params=_cparams(("arbitrary",)),
        name="memattn_ln2_router",
    )(h1, wq, k_mem, v_mem, wo, g, b, wr_hi_lo, b_r)


def _dispatch_kernel(e_ref, tok_ref, pos_ref, meta_ref):
    ca = DISPATCH_CHUNK
    n_chunks = N_ASSIGN // ca
    e_iota = lax.broadcasted_iota(I32, (N_EXPERTS, ca), 0)

    def onehot(c):
        e_c = e_ref[:, pl.ds(pl.multiple_of(c * ca, ca), ca)]
        return (e_iota == e_c).astype(F32)

    counts = lax.fori_loop(
        0, n_chunks, lambda c, acc: acc + jnp.sum(onehot(c), axis=1, keepdims=True),
        jnp.zeros((N_EXPERTS, 1), F32))
    nblk = jnp.right_shift(counts.astype(I32) + (MOE_ROWS - 1), MOE_ROWS_LOG2)
    strict_lower = (lax.broadcasted_iota(I32, (N_EXPERTS, N_EXPERTS), 1)
                    < lax.broadcasted_iota(I32, (N_EXPERTS, N_EXPERTS), 0)).astype(BF16)
    nblk_lanes = jnp.broadcast_to(nblk.astype(F32), (N_EXPERTS, LANES)).astype(BF16)
    blk_start = jnp.dot(strict_lower, nblk_lanes, preferred_element_type=F32)
    pad_start = blk_start[:, 0:1] * MOE_ROWS
    lane = lax.broadcasted_iota(I32, (N_EXPERTS, LANES), 1)
    meta_ref[...] = jnp.where(lane == 0, blk_start.astype(I32),
                              jnp.where(lane == 1, jnp.broadcast_to(nblk, (N_EXPERTS, LANES)), 0))

    earlier = (lax.broadcasted_iota(I32, (ca, ca), 0)
               < lax.broadcasted_iota(I32, (ca, ca), 1)).astype(BF16)
    blk_iota = lax.broadcasted_iota(I32, (N_ROW_BLOCKS, ca), 0)
    row_iota = lax.broadcasted_iota(I32, (MOE_ROWS, ca), 0)
    a_iota = lax.broadcasted_iota(I32, (1, ca), 1)

    def body(c, state):
        seen, acc_hi, acc_lo = state
        oh = onehot(c)
        before = jnp.dot(oh.astype(BF16), earlier, preferred_element_type=F32) + seen
        dest = jnp.sum(oh * (before + pad_start), axis=0, keepdims=True).astype(I32)
        pos_ref[:, pl.ds(pl.multiple_of(c * ca, ca), ca)] = dest
        in_blk = (blk_iota == jnp.right_shift(dest, MOE_ROWS_LOG2)).astype(F32)
        in_row = (row_iota == jnp.bitwise_and(dest, MOE_ROWS - 1)).astype(BF16)
        tok = jnp.right_shift(c * ca + a_iota, 1)
        tok_hi = (jnp.right_shift(tok, 7) + 1).astype(F32)
        tok_lo = jnp.bitwise_and(tok, 127).astype(F32)
        nt = (((1,), (1,)), ((), ()))
        acc_hi = acc_hi + lax.dot_general((in_blk * tok_hi).astype(BF16), in_row, nt,
                                          preferred_element_type=F32)
        acc_lo = acc_lo + lax.dot_general((in_blk * tok_lo).astype(BF16), in_row, nt,
                                          preferred_element_type=F32)
        return seen + jnp.sum(oh, axis=1, keepdims=True), acc_hi, acc_lo

    zeros = jnp.zeros((N_ROW_BLOCKS, MOE_ROWS), F32)
    _, acc_hi, acc_lo = lax.fori_loop(0, n_chunks, body,
                                      (jnp.zeros((N_EXPERTS, 1), F32), zeros, zeros))
    pad_tok = jnp.bitwise_and(
        lax.broadcasted_iota(I32, (N_ROW_BLOCKS, MOE_ROWS), 0) * MOE_ROWS
        + lax.broadcasted_iota(I32, (N_ROW_BLOCKS, MOE_ROWS), 1), SEQ - 1)
    tok_ref[...] = jnp.where(acc_hi > 0.5, ((acc_hi - 1.0) * 128.0 + acc_lo).astype(I32), pad_tok)


def _dispatch(flat_e):
    return pl.pallas_call(
        _dispatch_kernel,
        out_shape=[jax.ShapeDtypeStruct((N_ROW_BLOCKS, MOE_ROWS), I32),
                   jax.ShapeDtypeStruct((1, N_ASSIGN), I32),
                   jax.ShapeDtypeStruct((N_EXPERTS, LANES), I32)],
        compiler_params=pltpu.CompilerParams(vmem_limit_bytes=VMEM_LIMIT),
        name="moe_dispatch",
    )(flat_e)


def _expert_kernel(bs_ref, nb_ref, tot_ref, tok_hbm, h_hbm, w1_hbm, w3_hbm, w2_hbm, y_hbm,
                   idx, xbuf, ybuf, isem, gsem, ysem, wsem, w1f, w3f, w2f, w1b, w3b, w2b):
    e = pl.program_id(0)
    n_exp = pl.num_programs(0)
    total = tot_ref[0]
    first = bs_ref[e]
    nblk = nb_ref[e]

    def idx_copy(blk):
        row = jnp.minimum(blk, N_ROW_BLOCKS - 1)
        slot = blk % GATHER_SLOTS
        return pltpu.make_async_copy(tok_hbm.at[pl.ds(row, 1)], idx.at[pl.ds(slot, 1)],
                                     isem.at[slot])

    def issue_rows(blk, rows):
        slot = blk % GATHER_SLOTS
        for r in rows:
            tok = idx[slot, r]
            pltpu.make_async_copy(h_hbm.at[pl.ds(tok, 1)], xbuf.at[slot, pl.ds(r, 1)],
                                  gsem.at[slot]).start()

    def wait_gather(blk):
        slot = blk % GATHER_SLOTS
        pltpu.make_async_copy(h_hbm.at[pl.ds(0, MOE_ROWS)], xbuf.at[slot], gsem.at[slot]).wait()

    def y_copy(blk, slot):
        return pltpu.make_async_copy(ybuf.at[slot], y_hbm.at[pl.ds(blk * MOE_ROWS, MOE_ROWS)],
                                     ysem.at[slot])

    def weight_copies(ex):
        slot = ex % WEIGHT_SLOTS
        return [pltpu.make_async_copy(src.at[ex], dst.at[slot], wsem.at[slot])
                for src, dst in ((w1_hbm, w1f), (w3_hbm, w3f), (w2_hbm, w2f))]

    @pl.when(e == 0)
    def _():
        for ex in range(WEIGHT_SLOTS - 1):
            for cp in weight_copies(ex):
                cp.start(priority=1)

        for blk in range(GATHER_SLOTS):
            idx_copy(blk).start()
        for blk in range(GATHER_SLOTS - 1):
            idx_copy(blk).wait()

        def prime(r, c):
            for blk in range(GATHER_SLOTS - 1):
                pltpu.make_async_copy(h_hbm.at[pl.ds(idx[blk, r], 1)], xbuf.at[blk, pl.ds(r, 1)],
                                      gsem.at[blk]).start()
            return c
        lax.fori_loop(0, MOE_ROWS, prime, 0, unroll=8)

    @pl.when(e + WEIGHT_SLOTS - 1 < n_exp)
    def _():
        for cp in weight_copies(e + WEIGHT_SLOTS - 1):
            cp.start(priority=1)

    for cp in weight_copies(e):
        cp.wait()

    @pl.when(nblk > 0)
    def _():
        wslot = e % WEIGHT_SLOTS
        w1b[...] = w1f[wslot].astype(BF16)
        w3b[...] = w3f[wslot].astype(BF16)
        w2b[...] = w2f[wslot].astype(BF16)

    n_piece = 4
    rows_per_piece = MOE_ROWS // n_piece
    cols_per_piece = D_MODEL // n_piece

    def block(b, c):
        blk = first + b
        slot = blk % 2
        wait_gather(blk)
        idx_copy(blk + GATHER_SLOTS - 1).wait()
        idx_copy(blk + GATHER_SLOTS).start()

        @pl.when(blk >= 2)
        def _():
            y_copy(blk - 2, slot).wait()

        xb = xbuf[blk % GATHER_SLOTS].astype(BF16)
        a = jnp.dot(xb, w1b[...], preferred_element_type=F32)
        g = jnp.dot(xb, w3b[...], preferred_element_type=F32)
        hid = (a * jax.nn.sigmoid(a) * g).astype(BF16)
        for p in range(n_piece):
            issue_rows(blk + GATHER_SLOTS - 1,
                       range(p * rows_per_piece, (p + 1) * rows_per_piece))
            cols = slice(p * cols_per_piece, (p + 1) * cols_per_piece)
            ybuf[slot, :, cols] = jnp.dot(hid, w2b[:, cols], preferred_element_type=F32)
        y_copy(blk, slot).start()
        return c

    lax.fori_loop(0, nblk, block, 0)

    @pl.when(e == n_exp - 1)
    def _():
        for ahead in range(GATHER_SLOTS - 1):
            wait_gather(total + ahead)
        idx_copy(total + GATHER_SLOTS - 1).wait()
        for back in (1, 2):
            @pl.when(total >= back)
            def _():
                y_copy(total - back, (total - back) % 2).wait()

        ybuf[0] = jnp.zeros((MOE_ROWS, D_MODEL), F32)

        def zero_block(blk, c):
            y_copy(blk, 0).start()
            y_copy(blk, 0).wait()
            return c

        lax.fori_loop(total, N_ROW_BLOCKS, zero_block, 0)


def _experts(blk_start, nblk, total, row_tok, h2, w1, w3, w2):
    grid_spec = pltpu.PrefetchScalarGridSpec(
        num_scalar_prefetch=3,
        grid=(N_EXPERTS,),
        in_specs=[pl.BlockSpec(memory_space=pl.ANY)] * 5,
        out_specs=pl.BlockSpec(memory_space=pl.ANY),
        scratch_shapes=[pltpu.SMEM((GATHER_SLOTS, MOE_ROWS), I32),
                        pltpu.VMEM((GATHER_SLOTS, MOE_ROWS, D_MODEL), F32),
                        pltpu.VMEM((2, MOE_ROWS, D_MODEL), F32),
                        pltpu.SemaphoreType.DMA((GATHER_SLOTS,)),
                        pltpu.SemaphoreType.DMA((GATHER_SLOTS,)),
                        pltpu.SemaphoreType.DMA((2,)),
                        pltpu.SemaphoreType.DMA((WEIGHT_SLOTS,)),
                        pltpu.VMEM((WEIGHT_SLOTS, D_MODEL, D_EXPERT), F32),
                        pltpu.VMEM((WEIGHT_SLOTS, D_MODEL, D_EXPERT), F32),
                        pltpu.VMEM((WEIGHT_SLOTS, D_EXPERT, D_MODEL), F32),
                        pltpu.VMEM((D_MODEL, D_EXPERT), BF16),
                        pltpu.VMEM((D_MODEL, D_EXPERT), BF16),
                        pltpu.VMEM((D_EXPERT, D_MODEL), BF16)],
    )
    return pl.pallas_call(
        _expert_kernel,
        grid_spec=grid_spec,
        out_shape=jax.ShapeDtypeStruct((N_ROWS, D_MODEL), F32),
        compiler_params=_cparams(("arbitrary",)),
        name="experts",
    )(blk_start, nblk, total, row_tok, h2, w1, w3, w2)


def _combine_kernel(y_hbm, pos_ref, pos_next_ref, h_ref, r_ref, g_ref, beta_ref, o_ref, ybuf, sem,
                    *, tc):
    i = pl.program_id(0)
    n = pl.num_programs(0)
    slot = i % 2

    def issue_rows(table, slot_, rows):
        for r in rows:
            for kk in range(TOP_K):
                p = table[0, 0, r * TOP_K + kk]
                pltpu.make_async_copy(y_hbm.at[pl.ds(p, 1)], ybuf.at[slot_, kk, pl.ds(r, 1)],
                                      sem.at[slot_]).start(priority=kk)

    def wait_rows(slot_):
        for kk in range(TOP_K):
            pltpu.make_async_copy(y_hbm.at[pl.ds(0, tc)], ybuf.at[slot_, kk], sem.at[slot_]).wait()

    @pl.when(i == 0)
    def _():
        issue_rows(pos_ref, 0, range(tc))

    def step(slot_):
        wait_rows(slot_)
        for c in range(tc // COMBINE_CHUNK):
            rows = slice(c * COMBINE_CHUNK, (c + 1) * COMBINE_CHUNK)
            issue_rows(pos_next_ref, 1 - slot_, range(c * COMBINE_CHUNK, (c + 1) * COMBINE_CHUNK))
            gates = r_ref[rows, :]
            moe = (ybuf[slot_, 0, rows, :] * gates[:, TOP_K:TOP_K + 1]
                   + ybuf[slot_, 1, rows, :] * gates[:, TOP_K + 1:TOP_K + 2])
            o_ref[rows, :] = _layer_norm(DN_ALPHA * h_ref[rows, :] + moe, g_ref[...], beta_ref[...])

    for parity in range(2):
        @pl.when(slot == parity)
        def _():
            step(parity)

    @pl.when(i == n - 1)
    def _():
        for parity in range(2):
            @pl.when(slot == parity)
            def _():
                wait_rows(1 - parity)


def _combine(pos, y_rows, h2, rout, g, b, tc):
    s = h2.shape[0]
    n = s // tc
    pos = pos.reshape(n, 1, tc * TOP_K)
    smem_block = functools.partial(pl.BlockSpec, (1, 1, tc * TOP_K), memory_space=pltpu.SMEM)
    return pl.pallas_call(
        functools.partial(_combine_kernel, tc=tc),
        grid=(n,),
        in_specs=[pl.BlockSpec(memory_space=pl.ANY),
                  smem_block(lambda i: (i, 0, 0)),
                  smem_block(lambda i: (jnp.minimum(i + 1, n - 1), 0, 0)),
                  pl.BlockSpec((tc, D_MODEL), lambda i: (i, 0)),
                  pl.BlockSpec((tc, LANES), lambda i: (i, 0)),
                  pl.BlockSpec((1, D_MODEL), lambda i: (0, 0)),
                  pl.BlockSpec((1, D_MODEL), lambda i: (0, 0))],
        out_specs=pl.BlockSpec((tc, D_MODEL), lambda i: (i, 0)),
        out_shape=jax.ShapeDtypeStruct((s, D_MODEL), F32),
        scratch_shapes=[pltpu.VMEM((2, TOP_K, tc, D_MODEL), F32),
                        pltpu.SemaphoreType.DMA((2,))],
        compiler_params=_cparams(("arbitrary",)),
        name="combine_ln3",
    )(y_rows, pos, pos, h2, rout, g, b)


def kernel(x, mem, w_in, gm_ln_g, gm_ln_b, gm_w_s, gm_b_s, w_mix_out, ln1_g, ln1_b,
           mem_w_q, mem_w_k, mem_w_v, mem_w_o, ln2_g, ln2_b,
           w_group, b_group, w_router, b_router, w1, w3, w2, ln3_g, ln3_b):
    assert x.shape == (1, SEQ, D_MODEL) and w_in.shape[0] == 1
    xs = x[0]
    l = 0
    w_mix = w_mix_out[l].astype(BF16)
    w_q = mem_w_q[l].astype(BF16)
    w_o = mem_w_o[l].astype(BF16)
    w_r = jnp.concatenate(
        [w_router[l].transpose(1, 0, 2).reshape(D_MODEL, N_EXPERTS), w_group[l],
         jnp.zeros((D_MODEL, LANES - N_EXPERTS - N_GROUPS), F32)], axis=1)
    wr_hi_lo = jnp.concatenate(_split_bf16(w_r), axis=1)
    b_r = jnp.concatenate([b_router[l].reshape(-1), b_group[l],
                           jnp.zeros((LANES - N_EXPERTS - N_GROUPS,), F32)]).reshape(1, LANES)
    row = lambda v: v.reshape(1, D_MODEL)

    uvk, qvt = _in_proj(xs, w_in[l], tm=1024, q_scale=SB_HEAD_DIM ** -0.5 * math.log2(math.e))
    mix_a = _gmlp(uvk, gm_w_s[l], gm_ln_g[l], gm_ln_b[l], gm_b_s[l].T, tg=512)
    mix_b = _stick_breaking(qvt, uvk, tq=256, bk=128)
    h1 = _mixout(mix_a, mix_b, w_mix, xs, row(ln1_g[l]), row(ln1_b[l]), tm=512)
    k_mem = _proj(mem[0], mem_w_k[l], tm=MEM_LEN, tn=1024)
    v_mem = _proj(mem[0], mem_w_v[l], tm=MEM_LEN, tn=1024)
    h2, rout = _memattn(h1, w_q, k_mem, v_mem, w_o, row(ln2_g[l]), row(ln2_b[l]), wr_hi_lo, b_r, tm=512)
    flat_e = rout[:, 0:TOP_K].astype(I32).reshape(1, N_ASSIGN)
    row_tok, pos, meta = _dispatch(flat_e)
    blk_start, nblk = meta[:, 0], meta[:, 1]
    total = (blk_start[N_EXPERTS - 1] + nblk[N_EXPERTS - 1]).reshape(1)
    y_rows = _experts(blk_start, nblk, total, row_tok, h2, w1[l], w3[l], w2[l])
    out = _combine(pos.reshape(N_ASSIGN), y_rows, h2, rout, row(ln3_g[l]), row(ln3_b[l]), tc=256)
    return out.reshape(1, SEQ, D_MODEL)
```

```python
import functools
import math

import jax
import jax.numpy as jnp
from jax import lax
from jax.experimental import pallas as pl
from jax.experimental.pallas import tpu as pltpu

D_MODEL = 2048
SEQ = 8192
CHUNK = 64
MEM_LEN = 256
GM_BLOCK = 128
GM_GROUPS = 8
GM_WIDTH = 1024
SB_HEADS = 8
SB_HEAD_DIM = 128
SB_WIDTH = 1024
MEM_HEADS = 4
MEM_HEAD_DIM = 512
N_GROUPS = 8
EXPERTS_PER_GROUP = 8
N_EXPERTS = 64
TOP_K = 2
D_EXPERT = 512
DN_ALPHA = 2.0 ** 0.25
LN_EPS = 1e-5

LANES = 128
MOE_ROWS = 128
MOE_ROWS_LOG2 = 7
N_ASSIGN = SEQ * TOP_K
N_ROW_BLOCKS = N_ASSIGN // MOE_ROWS + N_EXPERTS
N_ROWS = N_ROW_BLOCKS * MOE_ROWS
DISPATCH_CHUNK = 512
GATHER_SLOTS = 3
WEIGHT_SLOTS = 3
COMBINE_CHUNK = 32
SB_UNDERFLOW = 160.0
SB_LOOP_BLOCKS = 1
SB_STATIC_BELOW = 2
VMEM_LIMIT = 56 * 1024 * 1024

BF16 = jnp.bfloat16
F32 = jnp.float32
I32 = jnp.int32


def _cparams(sem):
    return pltpu.CompilerParams(dimension_semantics=sem, vmem_limit_bytes=VMEM_LIMIT)


def _resident(shape, index_map):
    return pl.BlockSpec(shape, index_map, pipeline_mode=pl.Buffered(1))


def _layer_norm(y, g, b):
    mu = jnp.mean(y, axis=-1, keepdims=True)
    yc = y - mu
    var = jnp.mean(yc * yc, axis=-1, keepdims=True)
    return yc * lax.rsqrt(var + LN_EPS) * g + b


def _split_bf16(v):
    hi = v.astype(BF16)
    lo = (v - hi.astype(F32)).astype(BF16)
    return hi, lo


def _proj_kernel(x_ref, w_ref, o_ref, xb_ref):
    @pl.when(pl.program_id(1) == 0)
    def _():
        xb_ref[...] = x_ref[...].astype(BF16)

    o_ref[...] = jnp.dot(xb_ref[...], w_ref[...].astype(BF16),
                         preferred_element_type=F32).astype(o_ref.dtype)


def _proj(x, w, tm, tn):
    m, k = x.shape
    n = w.shape[1]
    return pl.pallas_call(
        _proj_kernel,
        grid=(m // tm, n // tn),
        in_specs=[pl.BlockSpec((tm, k), lambda i, j: (i, 0)),
                  pl.BlockSpec((k, tn), lambda i, j: (0, j))],
        out_specs=pl.BlockSpec((tm, tn), lambda i, j: (i, j)),
        out_shape=jax.ShapeDtypeStruct((m, n), BF16),
        scratch_shapes=[pltpu.VMEM((tm, k), BF16)],
        compiler_params=_cparams(("arbitrary", "arbitrary")),
        name="proj",
    )(x, w)


IN_BLOCKS = 5
IN_Q, IN_K, IN_V = 2, 3, 4


def _in_proj_kernel(x_ref, w_ref, o_ref, ot_ref, xb_ref, *, q_scale):
    j = pl.program_id(1)

    @pl.when(j == 0)
    def _():
        xb_ref[...] = x_ref[...].astype(BF16)

    wb = w_ref[...]
    is_t = jnp.logical_or(j == IN_Q, j == IN_V)

    @pl.when(jnp.logical_not(is_t))
    def _():
        o_ref[...] = jnp.dot(xb_ref[...], wb, preferred_element_type=F32).astype(o_ref.dtype)

    @pl.when(is_t)
    def _():
        acc = lax.dot_general(wb, xb_ref[...], (((0,), (1,)), ((), ())), preferred_element_type=F32)
        ot_ref[...] = (acc * jnp.where(j == IN_Q, q_scale, 1.0).astype(F32)).astype(ot_ref.dtype)


def _in_proj(x, w, tm, q_scale):
    m, k = x.shape
    tn = SB_WIDTH
    omap = lambda i, j: (i, j - (j >= IN_Q).astype(I32) - (j >= IN_V).astype(I32))
    otmap = lambda i, j: ((j >= IN_V).astype(I32), i)
    return pl.pallas_call(
        functools.partial(_in_proj_kernel, q_scale=q_scale),
        grid=(m // tm, IN_BLOCKS),
        in_specs=[pl.BlockSpec((tm, k), lambda i, j: (i, 0)),
                  pl.BlockSpec((k, tn), lambda i, j: (0, j))],
        out_specs=[pl.BlockSpec((tm, tn), omap),
                   pl.BlockSpec((tn, tm), otmap)],
        out_shape=[jax.ShapeDtypeStruct((m, 3 * tn), BF16),
                   jax.ShapeDtypeStruct((2 * tn, m), BF16)],
        scratch_shapes=[pltpu.VMEM((tm, k), BF16)],
        compiler_params=_cparams(("arbitrary", "arbitrary")),
        name="in_proj",
    )(x, w)


def _gelu_tanh(x):
    c1 = math.sqrt(2.0 / math.pi)
    return x * (0.5 + 0.5 * jnp.tanh(x * (c1 + (0.044715 * c1) * (x * x))))


def _gmlp_kernel(u_ref, v_ref, ws_ref, g_ref, b_ref, bs_ref, o_ref, *, tg):
    t_chunk = lax.broadcasted_iota(I32, (GM_BLOCK, GM_BLOCK), 0) // CHUNK
    s_chunk = lax.broadcasted_iota(I32, (GM_BLOCK, GM_BLOCK), 1) // CHUNK
    causal = t_chunk >= s_chunk
    for g in range(GM_GROUPS):
        wm = jnp.where(causal, ws_ref[g], 0.0).astype(BF16)
        ln_g = g_ref[g:g + 1, :]
        ln_b = b_ref[g:g + 1, :]
        bias = bs_ref[:, g:g + 1]
        cols = slice(g * LANES, (g + 1) * LANES)
        for n in range(tg // GM_BLOCK):
            rows = slice(n * GM_BLOCK, (n + 1) * GM_BLOCK)
            v = _gelu_tanh(v_ref[rows, cols].astype(F32))
            vg = _layer_norm(v, ln_g, ln_b).astype(BF16)
            mixed = jnp.dot(wm, vg, preferred_element_type=F32) + bias
            u = _gelu_tanh(u_ref[rows, cols].astype(F32))
            o_ref[rows, cols] = (u * mixed).astype(o_ref.dtype)


def _gmlp(uvk, gm_w_s, gm_ln_g, gm_ln_b, gm_b_s_t, tg):
    s = uvk.shape[0]
    return pl.pallas_call(
        functools.partial(_gmlp_kernel, tg=tg),
        grid=(s // tg,),
        in_specs=[pl.BlockSpec((tg, GM_WIDTH), lambda i: (i, 0)),
                  pl.BlockSpec((tg, GM_WIDTH), lambda i: (i, 1)),
                  pl.BlockSpec((GM_GROUPS, GM_BLOCK, GM_BLOCK), lambda i: (0, 0, 0)),
                  pl.BlockSpec((GM_GROUPS, LANES), lambda i: (0, 0)),
                  pl.BlockSpec((GM_GROUPS, LANES), lambda i: (0, 0)),
                  pl.BlockSpec((GM_BLOCK, GM_GROUPS), lambda i: (0, 0))],
        out_specs=pl.BlockSpec((tg, GM_WIDTH), lambda i: (i, 0)),
        out_shape=jax.ShapeDtypeStruct((s, GM_WIDTH), BF16),
        compiler_params=_cparams(("arbitrary",)),
        name="gmlp",
    )(uvk, uvk, gm_w_s, gm_ln_g, gm_ln_b, gm_b_s_t)


def _softplus2(u):
    return jnp.maximum(u, 0.0) + jnp.log2(1.0 + jnp.exp2(-jnp.abs(u)))


def _sb_kernel(qt_ref, k_ref, vt_ref, o_ref, *acc_refs, tq, bk):
    q0 = pl.program_id(0) * tq
    upper = (lax.broadcasted_iota(I32, (bk, bk), 1)
             >= lax.broadcasted_iota(I32, (bk, bk), 0)).astype(BF16)
    upper2 = jnp.concatenate([upper, upper], axis=1)

    heads = range(SB_HEADS)
    hrows = [slice(h * SB_HEAD_DIM, (h + 1) * SB_HEAD_DIM) for h in heads]

    def tiles(blocks, carries):
        zs, sums, masks = [], [], []
        for ks, lo, masked in blocks:
            width = tq - lo
            zs.append([jnp.dot(k_ref[pl.ds(ks, bk), hrows[h]], qt_ref[hrows[h], lo:],
                               preferred_element_type=F32) for h in heads])
            if masked is True:
                masks.append(lax.broadcasted_iota(I32, (bk, width), 0)
                             < lax.broadcasted_iota(I32, (bk, width), 1))
            elif masked is False:
                masks.append(None)
            else:
                masks.append(jnp.broadcast_to(masked, (bk, width)))
        for zb, mask in zip(zs, masks):
            sps = [_softplus2(z) for z in zb]
            if mask is not None:
                sps = [jnp.where(mask, sp, 0.0) for sp in sps]
            sums.append([jnp.dot(upper2, jnp.concatenate(_split_bf16(sp), axis=0),
                                 preferred_element_type=F32) for sp in sps])
        csums = []
        for (ks, lo, masked), sb in zip(blocks, sums):
            cb = [sb[h] + carries[h][:, lo:] for h in heads]
            carries = [cb[h][0:1, :] if lo == 0
                       else jnp.concatenate([carries[h][:, :lo], cb[h][0:1, :]], axis=1)
                       for h in heads]
            csums.append(cb)
        probs = []
        for zb, cb, mask in zip(zs, csums, masks):
            pb = [jnp.exp2(zb[h] - cb[h]) for h in heads]
            if mask is not None:
                pb = [jnp.where(mask, a, 0.0) for a in pb]
            probs.append([a.astype(BF16) for a in pb])
        for (ks, lo, masked), pb in zip(blocks, probs):
            for h in heads:
                acc_refs[h][:, lo:] += jnp.dot(vt_ref[hrows[h], pl.ds(ks, bk)], pb[h],
                                               preferred_element_type=F32)
        return carries

    for acc_ref in acc_refs:
        acc_ref[...] = jnp.zeros_like(acc_ref)
    carries = [jnp.zeros((1, tq), F32) for _ in range(SB_HEADS)]
    blocks = [(pl.multiple_of(q0 + d * bk, bk), d * bk, True) for d in reversed(range(tq // bk))]
    for b in range(1, SB_STATIC_BELOW + 1):
        below = q0 // bk - b
        blocks.append((pl.multiple_of(jnp.maximum(below, 0) * bk, bk), 0, below >= 0))
    carries = tiles(blocks, carries)

    def cond(state):
        j = state[0]
        lowest = functools.reduce(jnp.minimum, state[1:])
        return jnp.logical_and(j >= 0, jnp.min(lowest) < SB_UNDERFLOW)

    def body(state):
        j = state[0]
        blocks = [(pl.multiple_of((j - b) * bk, bk), 0, False) for b in range(SB_LOOP_BLOCKS)]
        return (j - SB_LOOP_BLOCKS,) + tuple(tiles(blocks, list(state[1:])))

    assert (tq // bk) % SB_LOOP_BLOCKS == 0
    lax.while_loop(cond, body, (q0 // bk - 1 - SB_STATIC_BELOW,) + tuple(carries))
    for h in range(SB_HEADS):
        o_ref[hrows[h], :] = acc_refs[h][...].astype(o_ref.dtype)


def _stick_breaking(qvt, uvk, tq, bk):
    s = uvk.shape[0]
    return pl.pallas_call(
        functools.partial(_sb_kernel, tq=tq, bk=bk),
        grid=(s // tq,),
        in_specs=[pl.BlockSpec((SB_WIDTH, tq), lambda i: (0, i)),
                  _resident((s, SB_WIDTH), lambda i: (0, 2 * GM_WIDTH // SB_WIDTH)),
                  _resident((SB_WIDTH, s), lambda i: (1, 0))],
        out_specs=pl.BlockSpec((SB_WIDTH, tq), lambda i: (0, i)),
        out_shape=jax.ShapeDtypeStruct((SB_WIDTH, s), BF16),
        scratch_shapes=[pltpu.VMEM((SB_HEAD_DIM, tq), F32) for _ in range(SB_HEADS)],
        compiler_params=_cparams(("arbitrary",)),
        name="stick_breaking",
    )(qvt, uvk, qvt)


def _mixout_kernel(a_ref, bt_ref, wa_ref, wb_ref, x_ref, g_ref, beta_ref, o_ref):
    mixed = (jnp.dot(a_ref[...], wa_ref[...], preferred_element_type=F32)
             + lax.dot_general(bt_ref[...], wb_ref[...], (((0,), (0,)), ((), ())),
                               preferred_element_type=F32))
    y = DN_ALPHA * x_ref[...] + mixed
    o_ref[...] = _layer_norm(y, g_ref[...], beta_ref[...])


def _mixout(mix_a, mix_b, w, x, g, b, tm):
    s = x.shape[0]
    return pl.pallas_call(
        _mixout_kernel,
        grid=(s // tm,),
        in_specs=[pl.BlockSpec((tm, GM_WIDTH), lambda i: (i, 0)),
                  pl.BlockSpec((SB_WIDTH, tm), lambda i: (0, i)),
                  _resident((GM_WIDTH, D_MODEL), lambda i: (0, 0)),
                  _resident((SB_WIDTH, D_MODEL), lambda i: (1, 0)),
                  pl.BlockSpec((tm, D_MODEL), lambda i: (i, 0)),
                  pl.BlockSpec((1, D_MODEL), lambda i: (0, 0)),
                  pl.BlockSpec((1, D_MODEL), lambda i: (0, 0))],
        out_specs=pl.BlockSpec((tm, D_MODEL), lambda i: (i, 0)),
        out_shape=jax.ShapeDtypeStruct((s, D_MODEL), F32),
        compiler_params=_cparams(("arbitrary",)),
        name="mixout_ln1",
    )(mix_a, mix_b, w, w, x, g, b)


def _route(logits):
    lane = lax.broadcasted_iota(I32, logits.shape, 1)
    neg = jnp.float32(-jnp.inf)
    big = jnp.int32(1 << 20)

    def first_argmax(vals):
        m = jnp.max(vals, axis=-1, keepdims=True)
        idx = jnp.min(jnp.where(vals == m, lane, big), axis=-1, keepdims=True)
        return m, idx

    is_group = jnp.logical_and(lane >= N_EXPERTS, lane < N_EXPERTS + N_GROUPS)
    gl = jnp.where(is_group, logits, neg)
    gmax, glane = first_argmax(gl)
    g_val = 1.0 / jnp.sum(jnp.exp(gl - gmax), axis=-1, keepdims=True)
    gidx = glane - N_EXPERTS
    in_group = jnp.logical_and(lane >= gidx * EXPERTS_PER_GROUP,
                               lane < (gidx + 1) * EXPERTS_PER_GROUP)
    el = jnp.where(in_group, logits, neg)
    m1, i1 = first_argmax(el)
    m2, i2 = first_argmax(jnp.where(lane == i1, neg, el))
    e2 = jnp.exp(m2 - m1)
    gate1 = g_val / (1.0 + e2)
    gate2 = g_val * e2 / (1.0 + e2)
    return jnp.where(lane == 0, i1.astype(F32),
                     jnp.where(lane == 1, i2.astype(F32),
                               jnp.where(lane == 2, gate1,
                                         jnp.where(lane == 3, gate2, 0.0))))


def _memattn_kernel(h_ref, wq_ref, km_ref, vm_ref, wo_ref, g_ref, beta_ref, wr_ref, br_ref,
                    o_ref, r_ref):
    h = h_ref[...]
    q = jnp.dot(h.astype(BF16), wq_ref[...], preferred_element_type=F32).astype(BF16)
    outs = []
    for hd in range(MEM_HEADS):
        cols = slice(hd * MEM_HEAD_DIM, (hd + 1) * MEM_HEAD_DIM)
        kh = km_ref[:, cols]
        vh = vm_ref[:, cols]
        s = lax.dot_general(q[:, cols], kh, (((1,), (1,)), ((), ())),
                            preferred_element_type=F32) * (MEM_HEAD_DIM ** -0.5)
        s = s - jnp.max(s, axis=-1, keepdims=True)
        e = jnp.exp(s)
        p = e * (1.0 / jnp.sum(e, axis=-1, keepdims=True))
        outs.append(jnp.dot(p.astype(BF16), vh, preferred_element_type=F32).astype(BF16))
    o = jnp.concatenate(outs, axis=-1)
    attn = jnp.dot(o, wo_ref[...], preferred_element_type=F32)
    h2 = _layer_norm(DN_ALPHA * h + attn, g_ref[...], beta_ref[...])
    o_ref[...] = h2
    h_hi, h_lo = _split_bf16(h2)
    both = jnp.dot(h_hi, wr_ref[...], preferred_element_type=F32)
    logits = (both[:, :LANES] + both[:, LANES:]
              + jnp.dot(h_lo, wr_ref[:, :LANES], preferred_element_type=F32)) + br_ref[...]
    r_ref[...] = _route(logits)


def _memattn(h1, wq, k_mem, v_mem, wo, g, b, wr_hi_lo, b_r, tm):
    s = h1.shape[0]
    return pl.pallas_call(
        _memattn_kernel,
        grid=(s // tm,),
        in_specs=[pl.BlockSpec((tm, D_MODEL), lambda i: (i, 0)),
                  _resident((D_MODEL, D_MODEL), lambda i: (0, 0)),
                  _resident((MEM_LEN, D_MODEL), lambda i: (0, 0)),
                  _resident((MEM_LEN, D_MODEL), lambda i: (0, 0)),
                  _resident((D_MODEL, D_MODEL), lambda i: (0, 0)),
                  pl.BlockSpec((1, D_MODEL), lambda i: (0, 0)),
                  pl.BlockSpec((1, D_MODEL), lambda i: (0, 0)),
                  _resident((D_MODEL, 2 * LANES), lambda i: (0, 0)),
                  pl.BlockSpec((1, LANES), lambda i: (0, 0))],
        out_specs=[pl.BlockSpec((tm, D_MODEL), lambda i: (i, 0)),
                   pl.BlockSpec((tm, LANES), lambda i: (i, 0))],
        out_shape=[jax.ShapeDtypeStruct((s, D_MODEL), F32),
                   jax.ShapeDtypeStruct((s, LANES), F32)],
        compiler_params=_cparams(("arbitrary",)),
        name="memattn_ln2_router",
    )(h1, wq, k_mem, v_mem, wo, g, b, wr_hi_lo, b_r)


def _dispatch_kernel(e_ref, tok_ref, pos_ref, meta_ref):
    ca = DISPATCH_CHUNK
    n_chunks = N_ASSIGN // ca
    e_iota = lax.broadcasted_iota(I32, (N_EXPERTS, ca), 0)

    def onehot(c):
        e_c = e_ref[:, pl.ds(pl.multiple_of(c * ca, ca), ca)]
        return (e_iota == e_c).astype(F32)

    counts = lax.fori_loop(
        0, n_chunks, lambda c, acc: acc + jnp.sum(onehot(c), axis=1, keepdims=True),
        jnp.zeros((N_EXPERTS, 1), F32))
    nblk = jnp.right_shift(counts.astype(I32) + (MOE_ROWS - 1), MOE_ROWS_LOG2)
    strict_lower = (lax.broadcasted_iota(I32, (N_EXPERTS, N_EXPERTS), 1)
                    < lax.broadcasted_iota(I32, (N_EXPERTS, N_EXPERTS), 0)).astype(BF16)
    nblk_lanes = jnp.broadcast_to(nblk.astype(F32), (N_EXPERTS, LANES)).astype(BF16)
    blk_start = jnp.dot(strict_lower, nblk_lanes, preferred_element_type=F32)
    pad_start = blk_start[:, 0:1] * MOE_ROWS
    lane = lax.broadcasted_iota(I32, (N_EXPERTS, LANES), 1)
    meta_ref[...] = jnp.where(lane == 0, blk_start.astype(I32),
                              jnp.where(lane == 1, jnp.broadcast_to(nblk, (N_EXPERTS, LANES)), 0))

    earlier = (lax.broadcasted_iota(I32, (ca, ca), 0)
               < lax.broadcasted_iota(I32, (ca, ca), 1)).astype(BF16)
    blk_iota = lax.broadcasted_iota(I32, (N_ROW_BLOCKS, ca), 0)
    row_iota = lax.broadcasted_iota(I32, (MOE_ROWS, ca), 0)
    a_iota = lax.broadcasted_iota(I32, (1, ca), 1)

    def body(c, state):
        seen, acc_hi, acc_lo = state
        oh = onehot(c)
        before = jnp.dot(oh.astype(BF16), earlier, preferred_element_type=F32) + seen
        dest = jnp.sum(oh * (before + pad_start), axis=0, keepdims=True).astype(I32)
        pos_ref[:, pl.ds(pl.multiple_of(c * ca, ca), ca)] = dest
        in_blk = (blk_iota == jnp.right_shift(dest, MOE_ROWS_LOG2)).astype(F32)
        in_row = (row_iota == jnp.bitwise_and(dest, MOE_ROWS - 1)).astype(BF16)
        tok = jnp.right_shift(c * ca + a_iota, 1)
        tok_hi = (jnp.right_shift(tok, 7) + 1).astype(F32)
        tok_lo = jnp.bitwise_and(tok, 127).astype(F32)
        nt = (((1,), (1,)), ((), ()))
        acc_hi = acc_hi + lax.dot_general((in_blk * tok_hi).astype(BF16), in_row, nt,
                                          preferred_element_type=F32)
        acc_lo = acc_lo + lax.dot_general((in_blk * tok_lo).astype(BF16), in_row, nt,
                                          preferred_element_type=F32)
        return seen + jnp.sum(oh, axis=1, keepdims=True), acc_hi, acc_lo

    zeros = jnp.zeros((N_ROW_BLOCKS, MOE_ROWS), F32)
    _, acc_hi, acc_lo = lax.fori_loop(0, n_chunks, body,
                                      (jnp.zeros((N_EXPERTS, 1), F32), zeros, zeros))
    pad_tok = jnp.bitwise_and(
        lax.broadcasted_iota(I32, (N_ROW_BLOCKS, MOE_ROWS), 0) * MOE_ROWS
        + lax.broadcasted_iota(I32, (N_ROW_BLOCKS, MOE_ROWS), 1), SEQ - 1)
    tok_ref[...] = jnp.where(acc_hi > 0.5, ((acc_hi - 1.0) * 128.0 + acc_lo).astype(I32), pad_tok)


def _dispatch(flat_e):
    return pl.pallas_call(
        _dispatch_kernel,
        out_shape=[jax.ShapeDtypeStruct((N_ROW_BLOCKS, MOE_ROWS), I32),
                   jax.ShapeDtypeStruct((1, N_ASSIGN), I32),
                   jax.ShapeDtypeStruct((N_EXPERTS, LANES), I32)],
        compiler_params=pltpu.CompilerParams(vmem_limit_bytes=VMEM_LIMIT),
        name="moe_dispatch",
    )(flat_e)


def _expert_kernel(bs_ref, nb_ref, tot_ref, tok_hbm, h_hbm, w1_hbm, w3_hbm, w2_hbm, y_hbm,
                   idx, xbuf, ybuf, isem, gsem, ysem, wsem, w1f, w3f, w2f, w1b, w3b, w2b):
    e = pl.program_id(0)
    n_exp = pl.num_programs(0)
    total = tot_ref[0]
    first = bs_ref[e]
    nblk = nb_ref[e]

    def idx_copy(blk):
        row = jnp.minimum(blk, N_ROW_BLOCKS - 1)
        slot = blk % GATHER_SLOTS
        return pltpu.make_async_copy(tok_hbm.at[pl.ds(row, 1)], idx.at[pl.ds(slot, 1)],
                                     isem.at[slot])

    def issue_rows(blk, rows):
        slot = blk % GATHER_SLOTS
        for r in rows:
            tok = idx[slot, r]
            pltpu.make_async_copy(h_hbm.at[pl.ds(tok, 1)], xbuf.at[slot, pl.ds(r, 1)],
                                  gsem.at[slot]).start()

    def wait_gather(blk):
        slot = blk % GATHER_SLOTS
        pltpu.make_async_copy(h_hbm.at[pl.ds(0, MOE_ROWS)], xbuf.at[slot], gsem.at[slot]).wait()

    def y_copy(blk, slot):
        return pltpu.make_async_copy(ybuf.at[slot], y_hbm.at[pl.ds(blk * MOE_ROWS, MOE_ROWS)],
                                     ysem.at[slot])

    def weight_copies(ex):
        slot = ex % WEIGHT_SLOTS
        return [pltpu.make_async_copy(src.at[ex], dst.at[slot], wsem.at[slot])
                for src, dst in ((w1_hbm, w1f), (w3_hbm, w3f), (w2_hbm, w2f))]

    @pl.when(e == 0)
    def _():
        for ex in range(WEIGHT_SLOTS - 1):
            for cp in weight_copies(ex):
                cp.start(priority=1)

        for blk in range(GATHER_SLOTS):
            idx_copy(blk).start()
        for blk in range(GATHER_SLOTS - 1):
            idx_copy(blk).wait()

        def prime(r, c):
            for blk in range(GATHER_SLOTS - 1):
                pltpu.make_async_copy(h_hbm.at[pl.ds(idx[blk, r], 1)], xbuf.at[blk, pl.ds(r, 1)],
                                      gsem.at[blk]).start()
            return c
        lax.fori_loop(0, MOE_ROWS, prime, 0, unroll=8)

    @pl.when(e + WEIGHT_SLOTS - 1 < n_exp)
    def _():
        for cp in weight_copies(e + WEIGHT_SLOTS - 1):
            cp.start(priority=1)

    for cp in weight_copies(e):
        cp.wait()

    @pl.when(nblk > 0)
    def _():
        wslot = e % WEIGHT_SLOTS
        w1b[...] = w1f[wslot].astype(BF16)
        w3b[...] = w3f[wslot].astype(BF16)
        w2b[...] = w2f[wslot].astype(BF16)

    n_piece = 4
    rows_per_piece = MOE_ROWS // n_piece
    cols_per_piece = D_MODEL // n_piece

    def block(b, c):
        blk = first + b
        slot = blk % 2
        wait_gather(blk)
        idx_copy(blk + GATHER_SLOTS - 1).wait()
        idx_copy(blk + GATHER_SLOTS).start()

        @pl.when(blk >= 2)
        def _():
            y_copy(blk - 2, slot).wait()

        xb = xbuf[blk % GATHER_SLOTS].astype(BF16)
        a = jnp.dot(xb, w1b[...], preferred_element_type=F32)
        g = jnp.dot(xb, w3b[...], preferred_element_type=F32)
        hid = (a * jax.nn.sigmoid(a) * g).astype(BF16)
        for p in range(n_piece):
            issue_rows(blk + GATHER_SLOTS - 1,
                       range(p * rows_per_piece, (p + 1) * rows_per_piece))
            cols = slice(p * cols_per_piece, (p + 1) * cols_per_piece)
            ybuf[slot, :, cols] = jnp.dot(hid, w2b[:, cols], preferred_element_type=F32)
        y_copy(blk, slot).start()
        return c

    lax.fori_loop(0, nblk, block, 0)

    @pl.when(e == n_exp - 1)
    def _():
        for ahead in range(GATHER_SLOTS - 1):
            wait_gather(total + ahead)
        idx_copy(total + GATHER_SLOTS - 1).wait()
        for back in (1, 2):
            @pl.when(total >= back)
            def _():
                y_copy(total - back, (total - back) % 2).wait()

        ybuf[0] = jnp.zeros((MOE_ROWS, D_MODEL), F32)

        def zero_block(blk, c):
            y_copy(blk, 0).start()
            y_copy(blk, 0).wait()
            return c

        lax.fori_loop(total, N_ROW_BLOCKS, zero_block, 0)


def _experts(blk_start, nblk, total, row_tok, h2, w1, w3, w2):
    grid_spec = pltpu.PrefetchScalarGridSpec(
        num_scalar_prefetch=3,
        grid=(N_EXPERTS,),
        in_specs=[pl.BlockSpec(memory_space=pl.ANY)] * 5,
        out_specs=pl.BlockSpec(memory_space=pl.ANY),
        scratch_shapes=[pltpu.SMEM((GATHER_SLOTS, MOE_ROWS), I32),
                        pltpu.VMEM((GATHER_SLOTS, MOE_ROWS, D_MODEL), F32),
                        pltpu.VMEM((2, MOE_ROWS, D_MODEL), F32),
                        pltpu.SemaphoreType.DMA((GATHER_SLOTS,)),
                        pltpu.SemaphoreType.DMA((GATHER_SLOTS,)),
                        pltpu.SemaphoreType.DMA((2,)),
                        pltpu.SemaphoreType.DMA((WEIGHT_SLOTS,)),
                        pltpu.VMEM((WEIGHT_SLOTS, D_MODEL, D_EXPERT), F32),
                        pltpu.VMEM((WEIGHT_SLOTS, D_MODEL, D_EXPERT), F32),
                        pltpu.VMEM((WEIGHT_SLOTS, D_EXPERT, D_MODEL), F32),
                        pltpu.VMEM((D_MODEL, D_EXPERT), BF16),
                        pltpu.VMEM((D_MODEL, D_EXPERT), BF16),
                        pltpu.VMEM((D_EXPERT, D_MODEL), BF16)],
    )
    return pl.pallas_call(
        _expert_kernel,
        grid_spec=grid_spec,
        out_shape=jax.ShapeDtypeStruct((N_ROWS, D_MODEL), F32),
        compiler_params=_cparams(("arbitrary",)),
        name="experts",
    )(blk_start, nblk, total, row_tok, h2, w1, w3, w2)


def _combine_kernel(y_hbm, pos_ref, pos_next_ref, h_ref, r_ref, g_ref, beta_ref, o_ref, ybuf, sem,
                    *, tc):
    i = pl.program_id(0)
    n = pl.num_programs(0)
    slot = i % 2

    def issue_rows(table, slot_, rows):
        for r in rows:
            for kk in range(TOP_K):
                p = table[0, 0, r * TOP_K + kk]
                pltpu.make_async_copy(y_hbm.at[pl.ds(p, 1)], ybuf.at[slot_, kk, pl.ds(r, 1)],
                                      sem.at[slot_]).start(priority=kk)

    def wait_rows(slot_):
        for kk in range(TOP_K):
            pltpu.make_async_copy(y_hbm.at[pl.ds(0, tc)], ybuf.at[slot_, kk], sem.at[slot_]).wait()

    @pl.when(i == 0)
    def _():
        issue_rows(pos_ref, 0, range(tc))

    def step(slot_):
        wait_rows(slot_)
        for c in range(tc // COMBINE_CHUNK):
            rows = slice(c * COMBINE_CHUNK, (c + 1) * COMBINE_CHUNK)
            issue_rows(pos_next_ref, 1 - slot_, range(c * COMBINE_CHUNK, (c + 1) * COMBINE_CHUNK))
            gates = r_ref[rows, :]
            moe = (ybuf[slot_, 0, rows, :] * gates[:, TOP_K:TOP_K + 1]
                   + ybuf[slot_, 1, rows, :] * gates[:, TOP_K + 1:TOP_K + 2])
            o_ref[rows, :] = _layer_norm(DN_ALPHA * h_ref[rows, :] + moe, g_ref[...], beta_ref[...])

    for parity in range(2):
        @pl.when(slot == parity)
        def _():
            step(parity)

    @pl.when(i == n - 1)
    def _():
        for parity in range(2):
            @pl.when(slot == parity)
            def _():
                wait_rows(1 - parity)


def _combine(pos, y_rows, h2, rout, g, b, tc):
    s = h2.shape[0]
    n = s // tc
    pos = pos.reshape(n, 1, tc * TOP_K)
    smem_block = functools.partial(pl.BlockSpec, (1, 1, tc * TOP_K), memory_space=pltpu.SMEM)
    return pl.pallas_call(
        functools.partial(_combine_kernel, tc=tc),
        grid=(n,),
        in_specs=[pl.BlockSpec(memory_space=pl.ANY),
                  smem_block(lambda i: (i, 0, 0)),
                  smem_block(lambda i: (jnp.minimum(i + 1, n - 1), 0, 0)),
                  pl.BlockSpec((tc, D_MODEL), lambda i: (i, 0)),
                  pl.BlockSpec((tc, LANES), lambda i: (i, 0)),
                  pl.BlockSpec((1, D_MODEL), lambda i: (0, 0)),
                  pl.BlockSpec((1, D_MODEL), lambda i: (0, 0))],
        out_specs=pl.BlockSpec((tc, D_MODEL), lambda i: (i, 0)),
        out_shape=jax.ShapeDtypeStruct((s, D_MODEL), F32),
        scratch_shapes=[pltpu.VMEM((2, TOP_K, tc, D_MODEL), F32),
                        pltpu.SemaphoreType.DMA((2,))],
        compiler_params=_cparams(("arbitrary",)),
        name="combine_ln3",
    )(y_rows, pos, pos, h2, rout, g, b)


def kernel(x, mem, w_in, gm_ln_g, gm_ln_b, gm_w_s, gm_b_s, w_mix_out, ln1_g, ln1_b,
           mem_w_q, mem_w_k, mem_w_v, mem_w_o, ln2_g, ln2_b,
           w_group, b_group, w_router, b_router, w1, w3, w2, ln3_g, ln3_b):
    assert x.shape == (1, SEQ, D_MODEL) and w_in.shape[0] == 1
    xs = x[0]
    l = 0
    w_mix = w_mix_out[l].astype(BF16)
    w_q = mem_w_q[l].astype(BF16)
    w_o = mem_w_o[l].astype(BF16)
    w_r = jnp.concatenate(
        [w_router[l].transpose(1, 0, 2).reshape(D_MODEL, N_EXPERTS), w_group[l],
         jnp.zeros((D_MODEL, LANES - N_EXPERTS - N_GROUPS), F32)], axis=1)
    wr_hi_lo = jnp.concatenate(_split_bf16(w_r), axis=1)
    b_r = jnp.concatenate([b_router[l].reshape(-1), b_group[l],
                           jnp.zeros((LANES - N_EXPERTS - N_GROUPS,), F32)]).reshape(1, LANES)
    row = lambda v: v.reshape(1, D_MODEL)

    uvk, qvt = _in_proj(xs, w_in[l].astype(BF16), tm=1024,
                        q_scale=SB_HEAD_DIM ** -0.5 * math.log2(math.e))
    mix_a = _gmlp(uvk, gm_w_s[l], gm_ln_g[l], gm_ln_b[l], gm_b_s[l].T, tg=512)
    mix_b = _stick_breaking(qvt, uvk, tq=256, bk=128)
    h1 = _mixout(mix_a, mix_b, w_mix, xs, row(ln1_g[l]), row(ln1_b[l]), tm=512)
    k_mem = _proj(mem[0], mem_w_k[l], tm=MEM_LEN, tn=1024)
    v_mem = _proj(mem[0], mem_w_v[l], tm=MEM_LEN, tn=1024)
    h2, rout = _memattn(h1, w_q, k_mem, v_mem, w_o, row(ln2_g[l]), row(ln2_b[l]), wr_hi_lo, b_r, tm=512)
    flat_e = rout[:, 0:TOP_K].astype(I32).reshape(1, N_ASSIGN)
    row_tok, pos, meta = _dispatch(flat_e)
    blk_start, nblk = meta[:, 0], meta[:, 1]
    total = (blk_start[N_EXPERTS - 1] + nblk[N_EXPERTS - 1]).reshape(1)
    y_rows = _experts(blk_start, nblk, total, row_tok, h2, w1[l], w3[l], w2[l])
    out = _combine(pos.reshape(N_ASSIGN), y_rows, h2, rout, row(ln3_g[l]), row(ln3_b[l]), tc=256)
    return out.reshape(1, SEQ, D_MODEL)
```

```python
import functools
import math

import jax
import jax.numpy as jnp
from jax import lax
from jax.experimental import pallas as pl
from jax.experimental.pallas import tpu as pltpu

D_MODEL = 2048
SEQ = 8192
CHUNK = 64
MEM_LEN = 256
GM_BLOCK = 128
GM_GROUPS = 8
GM_WIDTH = 1024
SB_HEADS = 8
SB_HEAD_DIM = 128
SB_WIDTH = 1024
MEM_HEADS = 4
MEM_HEAD_DIM = 512
N_GROUPS = 8
EXPERTS_PER_GROUP = 8
N_EXPERTS = 64
TOP_K = 2
D_EXPERT = 512
DN_ALPHA = 2.0 ** 0.25
LN_EPS = 1e-5

LANES = 128
MOE_ROWS = 128
MOE_ROWS_LOG2 = 7
N_ASSIGN = SEQ * TOP_K
N_ROW_BLOCKS = N_ASSIGN // MOE_ROWS + N_EXPERTS
N_ROWS = N_ROW_BLOCKS * MOE_ROWS
DISPATCH_CHUNK = 512
GATHER_SLOTS = 3
WEIGHT_SLOTS = 3
COMBINE_CHUNK = 32
SB_UNDERFLOW = 160.0
SB_LOOP_BLOCKS = 1
SB_STATIC_BELOW = 2
VMEM_LIMIT = 56 * 1024 * 1024

BF16 = jnp.bfloat16
F32 = jnp.float32
I32 = jnp.int32


def _cparams(sem):
    return pltpu.CompilerParams(dimension_semantics=sem, vmem_limit_bytes=VMEM_LIMIT)


def _resident(shape, index_map):
    return pl.BlockSpec(shape, index_map, pipeline_mode=pl.Buffered(1))


def _layer_norm(y, g, b):
    mu = jnp.mean(y, axis=-1, keepdims=True)
    yc = y - mu
    var = jnp.mean(yc * yc, axis=-1, keepdims=True)
    return yc * lax.rsqrt(var + LN_EPS) * g + b


def _split_bf16(v):
    hi = v.astype(BF16)
    lo = (v - hi.astype(F32)).astype(BF16)
    return hi, lo


def _proj_kernel(x_ref, w_ref, o_ref, xb_ref):
    @pl.when(pl.program_id(1) == 0)
    def _():
        xb_ref[...] = x_ref[...].astype(BF16)

    o_ref[...] = jnp.dot(xb_ref[...], w_ref[...].astype(BF16),
                         preferred_element_type=F32).astype(o_ref.dtype)


def _proj(x, w, tm, tn):
    m, k = x.shape
    n = w.shape[1]
    return pl.pallas_call(
        _proj_kernel,
        grid=(m // tm, n // tn),
        in_specs=[pl.BlockSpec((tm, k), lambda i, j: (i, 0)),
                  pl.BlockSpec((k, tn), lambda i, j: (0, j))],
        out_specs=pl.BlockSpec((tm, tn), lambda i, j: (i, j)),
        out_shape=jax.ShapeDtypeStruct((m, n), BF16),
        scratch_shapes=[pltpu.VMEM((tm, k), BF16)],
        compiler_params=_cparams(("arbitrary", "arbitrary")),
        name="proj",
    )(x, w)


IN_BLOCKS = 5
IN_Q, IN_K, IN_V = 2, 3, 4


def _in_proj_kernel(x_ref, w_ref, o_ref, ot_ref, xb_ref, *, q_scale):
    j = pl.program_id(1)

    @pl.when(j == 0)
    def _():
        xb_ref[...] = x_ref[...].astype(BF16)

    wb = w_ref[...].astype(BF16)
    is_t = jnp.logical_or(j == IN_Q, j == IN_V)

    @pl.when(jnp.logical_not(is_t))
    def _():
        o_ref[...] = jnp.dot(xb_ref[...], wb, preferred_element_type=F32).astype(o_ref.dtype)

    @pl.when(is_t)
    def _():
        acc = lax.dot_general(wb, xb_ref[...], (((0,), (1,)), ((), ())), preferred_element_type=F32)
        ot_ref[...] = (acc * jnp.where(j == IN_Q, q_scale, 1.0).astype(F32)).astype(ot_ref.dtype)


def _in_proj(x, w, tm, q_scale):
    m, k = x.shape
    tn = SB_WIDTH
    omap = lambda i, j: (i, j - (j >= IN_Q).astype(I32) - (j >= IN_V).astype(I32))
    otmap = lambda i, j: ((j >= IN_V).astype(I32), i)
    return pl.pallas_call(
        functools.partial(_in_proj_kernel, q_scale=q_scale),
        grid=(m // tm, IN_BLOCKS),
        in_specs=[pl.BlockSpec((tm, k), lambda i, j: (i, 0)),
                  pl.BlockSpec((k, tn), lambda i, j: (0, j))],
        out_specs=[pl.BlockSpec((tm, tn), omap),
                   pl.BlockSpec((tn, tm), otmap)],
        out_shape=[jax.ShapeDtypeStruct((m, 3 * tn), BF16),
                   jax.ShapeDtypeStruct((2 * tn, m), BF16)],
        scratch_shapes=[pltpu.VMEM((tm, k), BF16)],
        compiler_params=_cparams(("arbitrary", "arbitrary")),
        name="in_proj",
    )(x, w)


def _gelu_tanh(x):
    c1 = math.sqrt(2.0 / math.pi)
    return x * (0.5 + 0.5 * jnp.tanh(x * (c1 + (0.044715 * c1) * (x * x))))


def _gmlp_kernel(u_ref, v_ref, ws_ref, g_ref, b_ref, bs_ref, o_ref, *, tg):
    t_chunk = lax.broadcasted_iota(I32, (GM_BLOCK, GM_BLOCK), 0) // CHUNK
    s_chunk = lax.broadcasted_iota(I32, (GM_BLOCK, GM_BLOCK), 1) // CHUNK
    causal = t_chunk >= s_chunk
    for g in range(GM_GROUPS):
        wm = jnp.where(causal, ws_ref[g], 0.0).astype(BF16)
        ln_g = g_ref[g:g + 1, :]
        ln_b = b_ref[g:g + 1, :]
        bias = bs_ref[:, g:g + 1]
        cols = slice(g * LANES, (g + 1) * LANES)
        for n in range(tg // GM_BLOCK):
            rows = slice(n * GM_BLOCK, (n + 1) * GM_BLOCK)
            v = _gelu_tanh(v_ref[rows, cols].astype(F32))
            vg = _layer_norm(v, ln_g, ln_b).astype(BF16)
            mixed = jnp.dot(wm, vg, preferred_element_type=F32) + bias
            u = _gelu_tanh(u_ref[rows, cols].astype(F32))
            o_ref[rows, cols] = (u * mixed).astype(o_ref.dtype)


def _gmlp(uvk, gm_w_s, gm_ln_g, gm_ln_b, gm_b_s_t, tg):
    s = uvk.shape[0]
    return pl.pallas_call(
        functools.partial(_gmlp_kernel, tg=tg),
        grid=(s // tg,),
        in_specs=[pl.BlockSpec((tg, GM_WIDTH), lambda i: (i, 0)),
                  pl.BlockSpec((tg, GM_WIDTH), lambda i: (i, 1)),
                  pl.BlockSpec((GM_GROUPS, GM_BLOCK, GM_BLOCK), lambda i: (0, 0, 0)),
                  pl.BlockSpec((GM_GROUPS, LANES), lambda i: (0, 0)),
                  pl.BlockSpec((GM_GROUPS, LANES), lambda i: (0, 0)),
                  pl.BlockSpec((GM_BLOCK, GM_GROUPS), lambda i: (0, 0))],
        out_specs=pl.BlockSpec((tg, GM_WIDTH), lambda i: (i, 0)),
        out_shape=jax.ShapeDtypeStruct((s, GM_WIDTH), BF16),
        compiler_params=_cparams(("arbitrary",)),
        name="gmlp",
    )(uvk, uvk, gm_w_s, gm_ln_g, gm_ln_b, gm_b_s_t)


def _softplus2(u):
    return jnp.maximum(u, 0.0) + jnp.log2(1.0 + jnp.exp2(-jnp.abs(u)))


def _sb_kernel(qt_ref, k_ref, vt_ref, o_ref, *acc_refs, tq, bk):
    q0 = pl.program_id(0) * tq
    upper = (lax.broadcasted_iota(I32, (bk, bk), 1)
             >= lax.broadcasted_iota(I32, (bk, bk), 0)).astype(BF16)
    upper2 = jnp.concatenate([upper, upper], axis=1)

    heads = range(SB_HEADS)
    hrows = [slice(h * SB_HEAD_DIM, (h + 1) * SB_HEAD_DIM) for h in heads]

    def tiles(blocks, carries):
        zs, sums, masks = [], [], []
        for ks, lo, masked in blocks:
            width = tq - lo
            zs.append([jnp.dot(k_ref[pl.ds(ks, bk), hrows[h]], qt_ref[hrows[h], lo:],
                               preferred_element_type=F32) for h in heads])
            if masked is True:
                masks.append(lax.broadcasted_iota(I32, (bk, width), 0)
                             < lax.broadcasted_iota(I32, (bk, width), 1))
            elif masked is False:
                masks.append(None)
            else:
                masks.append(jnp.broadcast_to(masked, (bk, width)))
        for zb, mask in zip(zs, masks):
            sps = [_softplus2(z) for z in zb]
            if mask is not None:
                sps = [jnp.where(mask, sp, 0.0) for sp in sps]
            sums.append([jnp.dot(upper2, jnp.concatenate(_split_bf16(sp), axis=0),
                                 preferred_element_type=F32) for sp in sps])
        csums = []
        for (ks, lo, masked), sb in zip(blocks, sums):
            cb = [sb[h] + carries[h][:, lo:] for h in heads]
            carries = [cb[h][0:1, :] if lo == 0
                       else jnp.concatenate([carries[h][:, :lo], cb[h][0:1, :]], axis=1)
                       for h in heads]
            csums.append(cb)
        probs = []
        for zb, cb, mask in zip(zs, csums, masks):
            pb = [jnp.exp2(zb[h] - cb[h]) for h in heads]
            if mask is not None:
                pb = [jnp.where(mask, a, 0.0) for a in pb]
            probs.append([a.astype(BF16) for a in pb])
        for (ks, lo, masked), pb in zip(blocks, probs):
            for h in heads:
                acc_refs[h][:, lo:] += jnp.dot(vt_ref[hrows[h], pl.ds(ks, bk)], pb[h],
                                               preferred_element_type=F32)
        return carries

    for acc_ref in acc_refs:
        acc_ref[...] = jnp.zeros_like(acc_ref)
    carries = [jnp.zeros((1, tq), F32) for _ in range(SB_HEADS)]
    blocks = [(pl.multiple_of(q0 + d * bk, bk), d * bk, True) for d in reversed(range(tq // bk))]
    for b in range(1, SB_STATIC_BELOW + 1):
        below = q0 // bk - b
        blocks.append((pl.multiple_of(jnp.maximum(below, 0) * bk, bk), 0, below >= 0))
    carries = tiles(blocks, carries)

    def cond(state):
        j = state[0]
        lowest = functools.reduce(jnp.minimum, state[1:])
        return jnp.logical_and(j >= 0, jnp.min(lowest) < SB_UNDERFLOW)

    def body(state):
        j = state[0]
        blocks = [(pl.multiple_of((j - b) * bk, bk), 0, False) for b in range(SB_LOOP_BLOCKS)]
        return (j - SB_LOOP_BLOCKS,) + tuple(tiles(blocks, list(state[1:])))

    assert (tq // bk) % SB_LOOP_BLOCKS == 0
    lax.while_loop(cond, body, (q0 // bk - 1 - SB_STATIC_BELOW,) + tuple(carries))
    for h in range(SB_HEADS):
        o_ref[hrows[h], :] = acc_refs[h][...].astype(o_ref.dtype)


def _stick_breaking(qvt, uvk, tq, bk):
    s = uvk.shape[0]
    return pl.pallas_call(
        functools.partial(_sb_kernel, tq=tq, bk=bk),
        grid=(s // tq,),
        in_specs=[pl.BlockSpec((SB_WIDTH, tq), lambda i: (0, i)),
                  _resident((s, SB_WIDTH), lambda i: (0, 2 * GM_WIDTH // SB_WIDTH)),
                  _resident((SB_WIDTH, s), lambda i: (1, 0))],
        out_specs=pl.BlockSpec((SB_WIDTH, tq), lambda i: (0, i)),
        out_shape=jax.ShapeDtypeStruct((SB_WIDTH, s), BF16),
        scratch_shapes=[pltpu.VMEM((SB_HEAD_DIM, tq), F32) for _ in range(SB_HEADS)],
        compiler_params=_cparams(("arbitrary",)),
        name="stick_breaking",
    )(qvt, uvk, qvt)


def _mixout_kernel(a_ref, bt_ref, wa_ref, wb_ref, x_ref, g_ref, beta_ref, o_ref):
    mixed = (jnp.dot(a_ref[...], wa_ref[...], preferred_element_type=F32)
             + lax.dot_general(bt_ref[...], wb_ref[...], (((0,), (0,)), ((), ())),
                               preferred_element_type=F32))
    y = DN_ALPHA * x_ref[...] + mixed
    o_ref[...] = _layer_norm(y, g_ref[...], beta_ref[...])


def _mixout(mix_a, mix_b, w, x, g, b, tm):
    s = x.shape[0]
    return pl.pallas_call(
        _mixout_kernel,
        grid=(s // tm,),
        in_specs=[pl.BlockSpec((tm, GM_WIDTH), lambda i: (i, 0)),
                  pl.BlockSpec((SB_WIDTH, tm), lambda i: (0, i)),
                  _resident((GM_WIDTH, D_MODEL), lambda i: (0, 0)),
                  _resident((SB_WIDTH, D_MODEL), lambda i: (1, 0)),
                  pl.BlockSpec((tm, D_MODEL), lambda i: (i, 0)),
                  pl.BlockSpec((1, D_MODEL), lambda i: (0, 0)),
                  pl.BlockSpec((1, D_MODEL), lambda i: (0, 0))],
        out_specs=pl.BlockSpec((tm, D_MODEL), lambda i: (i, 0)),
        out_shape=jax.ShapeDtypeStruct((s, D_MODEL), F32),
        compiler_params=_cparams(("arbitrary",)),
        name="mixout_ln1",
    )(mix_a, mix_b, w, w, x, g, b)


def _route(logits):
    lane = lax.broadcasted_iota(I32, logits.shape, 1)
    neg = jnp.float32(-jnp.inf)
    big = jnp.int32(1 << 20)

    def first_argmax(vals):
        m = jnp.max(vals, axis=-1, keepdims=True)
        idx = jnp.min(jnp.where(vals == m, lane, big), axis=-1, keepdims=True)
        return m, idx

    is_group = jnp.logical_and(lane >= N_EXPERTS, lane < N_EXPERTS + N_GROUPS)
    gl = jnp.where(is_group, logits, neg)
    gmax, glane = first_argmax(gl)
    g_val = 1.0 / jnp.sum(jnp.exp(gl - gmax), axis=-1, keepdims=True)
    gidx = glane - N_EXPERTS
    in_group = jnp.logical_and(lane >= gidx * EXPERTS_PER_GROUP,
                               lane < (gidx + 1) * EXPERTS_PER_GROUP)
    el = jnp.where(in_group, logits, neg)
    m1, i1 = first_argmax(el)
    m2, i2 = first_argmax(jnp.where(lane == i1, neg, el))
    e2 = jnp.exp(m2 - m1)
    gate1 = g_val / (1.0 + e2)
    gate2 = g_val * e2 / (1.0 + e2)
    return jnp.where(lane == 0, i1.astype(F32),
                     jnp.where(lane == 1, i2.astype(F32),
                               jnp.where(lane == 2, gate1,
                                         jnp.where(lane == 3, gate2, 0.0))))


def _memattn_kernel(h_ref, wq_ref, km_ref, vm_ref, wo_ref, g_ref, beta_ref, wr_ref, br_ref,
                    o_ref, r_ref):
    h = h_ref[...]
    q = jnp.dot(h.astype(BF16), wq_ref[...], preferred_element_type=F32).astype(BF16)
    outs = []
    for hd in range(MEM_HEADS):
        cols = slice(hd * MEM_HEAD_DIM, (hd + 1) * MEM_HEAD_DIM)
        kh = km_ref[:, cols]
        vh = vm_ref[:, cols]
        s = lax.dot_general(q[:, cols], kh, (((1,), (1,)), ((), ())),
                            preferred_element_type=F32) * (MEM_HEAD_DIM ** -0.5)
        s = s - jnp.max(s, axis=-1, keepdims=True)
        e = jnp.exp(s)
        p = e * (1.0 / jnp.sum(e, axis=-1, keepdims=True))
        outs.append(jnp.dot(p.astype(BF16), vh, preferred_element_type=F32).astype(BF16))
    o = jnp.concatenate(outs, axis=-1)
    attn = jnp.dot(o, wo_ref[...], preferred_element_type=F32)
    h2 = _layer_norm(DN_ALPHA * h + attn, g_ref[...], beta_ref[...])
    o_ref[...] = h2
    h_hi, h_lo = _split_bf16(h2)
    both = jnp.dot(h_hi, wr_ref[...], preferred_element_type=F32)
    logits = (both[:, :LANES] + both[:, LANES:]
              + jnp.dot(h_lo, wr_ref[:, :LANES], preferred_element_type=F32)) + br_ref[...]
    r_ref[...] = _route(logits)


def _memattn(h1, wq, k_mem, v_mem, wo, g, b, wr_hi_lo, b_r, tm):
    s = h1.shape[0]
    return pl.pallas_call(
        _memattn_kernel,
        grid=(s // tm,),
        in_specs=[pl.BlockSpec((tm, D_MODEL), lambda i: (i, 0)),
                  _resident((D_MODEL, D_MODEL), lambda i: (0, 0)),
                  _resident((MEM_LEN, D_MODEL), lambda i: (0, 0)),
                  _resident((MEM_LEN, D_MODEL), lambda i: (0, 0)),
                  _resident((D_MODEL, D_MODEL), lambda i: (0, 0)),
                  pl.BlockSpec((1, D_MODEL), lambda i: (0, 0)),
                  pl.BlockSpec((1, D_MODEL), lambda i: (0, 0)),
                  _resident((D_MODEL, 2 * LANES), lambda i: (0, 0)),
                  pl.BlockSpec((1, LANES), lambda i: (0, 0))],
        out_specs=[pl.BlockSpec((tm, D_MODEL), lambda i: (i, 0)),
                   pl.BlockSpec((tm, LANES), lambda i: (i, 0))],
        out_shape=[jax.ShapeDtypeStruct((s, D_MODEL), F32),
                   jax.ShapeDtypeStruct((s, LANES), F32)],
        compiler_params=_cparams(("arbitrary",)),
        name="memattn_ln2_router",
    )(h1, wq, k_mem, v_mem, wo, g, b, wr_hi_lo, b_r)


def _dispatch_kernel(e_ref, tok_ref, pos_ref, meta_ref):
    ca = DISPATCH_CHUNK
    n_chunks = N_ASSIGN // ca
    e_iota = lax.broadcasted_iota(I32, (N_EXPERTS, ca), 0)

    def onehot(c):
        e_c = e_ref[:, pl.ds(pl.multiple_of(c * ca, ca), ca)]
        return (e_iota == e_c).astype(F32)

    counts = lax.fori_loop(
        0, n_chunks, lambda c, acc: acc + jnp.sum(onehot(c), axis=1, keepdims=True),
        jnp.zeros((N_EXPERTS, 1), F32))
    nblk = jnp.right_shift(counts.astype(I32) + (MOE_ROWS - 1), MOE_ROWS_LOG2)
    strict_lower = (lax.broadcasted_iota(I32, (N_EXPERTS, N_EXPERTS), 1)
                    < lax.broadcasted_iota(I32, (N_EXPERTS, N_EXPERTS), 0)).astype(BF16)
    nblk_lanes = jnp.broadcast_to(nblk.astype(F32), (N_EXPERTS, LANES)).astype(BF16)
    blk_start = jnp.dot(strict_lower, nblk_lanes, preferred_element_type=F32)
    pad_start = blk_start[:, 0:1] * MOE_ROWS
    lane = lax.broadcasted_iota(I32, (N_EXPERTS, LANES), 1)
    meta_ref[...] = jnp.where(lane == 0, blk_start.astype(I32),
                              jnp.where(lane == 1, jnp.broadcast_to(nblk, (N_EXPERTS, LANES)), 0))

    earlier = (lax.broadcasted_iota(I32, (ca, ca), 0)
               < lax.broadcasted_iota(I32, (ca, ca), 1)).astype(BF16)
    blk_iota = lax.broadcasted_iota(I32, (N_ROW_BLOCKS, ca), 0)
    row_iota = lax.broadcasted_iota(I32, (MOE_ROWS, ca), 0)
    a_iota = lax.broadcasted_iota(I32, (1, ca), 1)

    def body(c, state):
        seen, acc_hi, acc_lo = state
        oh = onehot(c)
        before = jnp.dot(oh.astype(BF16), earlier, preferred_element_type=F32) + seen
        dest = jnp.sum(oh * (before + pad_start), axis=0, keepdims=True).astype(I32)
        pos_ref[:, pl.ds(pl.multiple_of(c * ca, ca), ca)] = dest
        in_blk = (blk_iota == jnp.right_shift(dest, MOE_ROWS_LOG2)).astype(F32)
        in_row = (row_iota == jnp.bitwise_and(dest, MOE_ROWS - 1)).astype(BF16)
        tok = jnp.right_shift(c * ca + a_iota, 1)
        tok_hi = (jnp.right_shift(tok, 7) + 1).astype(F32)
        tok_lo = jnp.bitwise_and(tok, 127).astype(F32)
        nt = (((1,), (1,)), ((), ()))
        acc_hi = acc_hi + lax.dot_general((in_blk * tok_hi).astype(BF16), in_row, nt,
                                          preferred_element_type=F32)
        acc_lo = acc_lo + lax.dot_general((in_blk * tok_lo).astype(BF16), in_row, nt,
                                          preferred_element_type=F32)
        return seen + jnp.sum(oh, axis=1, keepdims=True), acc_hi, acc_lo

    zeros = jnp.zeros((N_ROW_BLOCKS, MOE_ROWS), F32)
    _, acc_hi, acc_lo = lax.fori_loop(0, n_chunks, body,
                                      (jnp.zeros((N_EXPERTS, 1), F32), zeros, zeros))
    pad_tok = jnp.bitwise_and(
        lax.broadcasted_iota(I32, (N_ROW_BLOCKS, MOE_ROWS), 0) * MOE_ROWS
        + lax.broadcasted_iota(I32, (N_ROW_BLOCKS, MOE_ROWS), 1), SEQ - 1)
    tok_ref[...] = jnp.where(acc_hi > 0.5, ((acc_hi - 1.0) * 128.0 + acc_lo).astype(I32), pad_tok)


def _dispatch(flat_e):
    return pl.pallas_call(
        _dispatch_kernel,
        out_shape=[jax.ShapeDtypeStruct((N_ROW_BLOCKS, MOE_ROWS), I32),
                   jax.ShapeDtypeStruct((1, N_ASSIGN), I32),
                   jax.ShapeDtypeStruct((N_EXPERTS, LANES), I32)],
        compiler_params=pltpu.CompilerParams(vmem_limit_bytes=VMEM_LIMIT),
        name="moe_dispatch",
    )(flat_e)


def _expert_kernel(bs_ref, nb_ref, tot_ref, tok_hbm, h_hbm, w1_hbm, w3_hbm, w2_hbm, y_hbm,
                   idx, xbuf, ybuf, isem, gsem, ysem, wsem, w1f, w3f, w2f, w1b, w3b, w2b):
    e = pl.program_id(0)
    n_exp = pl.num_programs(0)
    total = tot_ref[0]
    first = bs_ref[e]
    nblk = nb_ref[e]

    def idx_copy(blk):
        row = jnp.minimum(blk, N_ROW_BLOCKS - 1)
        slot = blk % GATHER_SLOTS
        return pltpu.make_async_copy(tok_hbm.at[pl.ds(row, 1)], idx.at[pl.ds(slot, 1)],
                                     isem.at[slot])

    def issue_rows(blk, rows):
        slot = blk % GATHER_SLOTS
        for r in rows:
            tok = idx[slot, r]
            pltpu.make_async_copy(h_hbm.at[pl.ds(tok, 1)], xbuf.at[slot, pl.ds(r, 1)],
                                  gsem.at[slot]).start()

    def wait_gather(blk):
        slot = blk % GATHER_SLOTS
        pltpu.make_async_copy(h_hbm.at[pl.ds(0, MOE_ROWS)], xbuf.at[slot], gsem.at[slot]).wait()

    def y_copy(blk, slot):
        return pltpu.make_async_copy(ybuf.at[slot], y_hbm.at[pl.ds(blk * MOE_ROWS, MOE_ROWS)],
                                     ysem.at[slot])

    def weight_copies(ex):
        slot = ex % WEIGHT_SLOTS
        return [pltpu.make_async_copy(src.at[ex], dst.at[slot], wsem.at[slot])
                for src, dst in ((w1_hbm, w1f), (w3_hbm, w3f), (w2_hbm, w2f))]

    @pl.when(e == 0)
    def _():
        for ex in range(WEIGHT_SLOTS - 1):
            for cp in weight_copies(ex):
                cp.start(priority=1)

        for blk in range(GATHER_SLOTS):
            idx_copy(blk).start()
        for blk in range(GATHER_SLOTS - 1):
            idx_copy(blk).wait()

        def prime(r, c):
            for blk in range(GATHER_SLOTS - 1):
                pltpu.make_async_copy(h_hbm.at[pl.ds(idx[blk, r], 1)], xbuf.at[blk, pl.ds(r, 1)],
                                      gsem.at[blk]).start()
            return c
        lax.fori_loop(0, MOE_ROWS, prime, 0, unroll=8)

    @pl.when(e + WEIGHT_SLOTS - 1 < n_exp)
    def _():
        for cp in weight_copies(e + WEIGHT_SLOTS - 1):
            cp.start(priority=1)

    for cp in weight_copies(e):
        cp.wait()

    @pl.when(nblk > 0)
    def _():
        wslot = e % WEIGHT_SLOTS
        w1b[...] = w1f[wslot].astype(BF16)
        w3b[...] = w3f[wslot].astype(BF16)
        w2b[...] = w2f[wslot].astype(BF16)

    n_piece = 4
    rows_per_piece = MOE_ROWS // n_piece
    cols_per_piece = D_MODEL // n_piece

    def block(b, c):
        blk = first + b
        slot = blk % 2
        wait_gather(blk)
        idx_copy(blk + GATHER_SLOTS - 1).wait()
        idx_copy(blk + GATHER_SLOTS).start()

        @pl.when(blk >= 2)
        def _():
            y_copy(blk - 2, slot).wait()

        xb = xbuf[blk % GATHER_SLOTS].astype(BF16)
        a = jnp.dot(xb, w1b[...], preferred_element_type=F32)
        g = jnp.dot(xb, w3b[...], preferred_element_type=F32)
        hid = (a * jax.nn.sigmoid(a) * g).astype(BF16)
        for p in range(n_piece):
            issue_rows(blk + GATHER_SLOTS - 1,
                       range(p * rows_per_piece, (p + 1) * rows_per_piece))
            cols = slice(p * cols_per_piece, (p + 1) * cols_per_piece)
            ybuf[slot, :, cols] = jnp.dot(hid, w2b[:, cols], preferred_element_type=F32)
        y_copy(blk, slot).start()
        return c

    lax.fori_loop(0, nblk, block, 0)

    @pl.when(e == n_exp - 1)
    def _():
        for ahead in range(GATHER_SLOTS - 1):
            wait_gather(total + ahead)
        idx_copy(total + GATHER_SLOTS - 1).wait()
        for back in (1, 2):
            @pl.when(total >= back)
            def _():
                y_copy(total - back, (total - back) % 2).wait()

        ybuf[0] = jnp.zeros((MOE_ROWS, D_MODEL), F32)

        def start_zero(blk, c):
            y_copy(blk, 0).start()
            return c

        def wait_zero(blk, c):
            y_copy(blk, 0).wait()
            return c

        lax.fori_loop(total, N_ROW_BLOCKS, start_zero, 0)
        lax.fori_loop(total, N_ROW_BLOCKS, wait_zero, 0)


def _experts(blk_start, nblk, total, row_tok, h2, w1, w3, w2):
    grid_spec = pltpu.PrefetchScalarGridSpec(
        num_scalar_prefetch=3,
        grid=(N_EXPERTS,),
        in_specs=[pl.BlockSpec(memory_space=pl.ANY)] * 5,
        out_specs=pl.BlockSpec(memory_space=pl.ANY),
        scratch_shapes=[pltpu.SMEM((GATHER_SLOTS, MOE_ROWS), I32),
                        pltpu.VMEM((GATHER_SLOTS, MOE_ROWS, D_MODEL), F32),
                        pltpu.VMEM((2, MOE_ROWS, D_MODEL), F32),
                        pltpu.SemaphoreType.DMA((GATHER_SLOTS,)),
                        pltpu.SemaphoreType.DMA((GATHER_SLOTS,)),
                        pltpu.SemaphoreType.DMA((2,)),
                        pltpu.SemaphoreType.DMA((WEIGHT_SLOTS,)),
                        pltpu.VMEM((WEIGHT_SLOTS, D_MODEL, D_EXPERT), F32),
                        pltpu.VMEM((WEIGHT_SLOTS, D_MODEL, D_EXPERT), F32),
                        pltpu.VMEM((WEIGHT_SLOTS, D_EXPERT, D_MODEL), F32),
                        pltpu.VMEM((D_MODEL, D_EXPERT), BF16),
                        pltpu.VMEM((D_MODEL, D_EXPERT), BF16),
                        pltpu.VMEM((D_EXPERT, D_MODEL), BF16)],
    )
    return pl.pallas_call(
        _expert_kernel,
        grid_spec=grid_spec,
        out_shape=jax.ShapeDtypeStruct((N_ROWS, D_MODEL), F32),
        compiler_params=_cparams(("arbitrary",)),
        name="experts",
    )(blk_start, nblk, total, row_tok, h2, w1, w3, w2)


def _combine_kernel(y_hbm, pos_ref, pos_next_ref, h_ref, r_ref, g_ref, beta_ref, o_ref, ybuf, sem,
                    *, tc):
    i = pl.program_id(0)
    n = pl.num_programs(0)
    slot = i % 2

    def issue_rows(table, slot_, rows):
        for r in rows:
            for kk in range(TOP_K):
                p = table[0, 0, r * TOP_K + kk]
                pltpu.make_async_copy(y_hbm.at[pl.ds(p, 1)], ybuf.at[slot_, kk, pl.ds(r, 1)],
                                      sem.at[slot_]).start(priority=kk)

    def wait_rows(slot_):
        for kk in range(TOP_K):
            pltpu.make_async_copy(y_hbm.at[pl.ds(0, tc)], ybuf.at[slot_, kk], sem.at[slot_]).wait()

    @pl.when(i == 0)
    def _():
        issue_rows(pos_ref, 0, range(tc))

    def step(slot_):
        wait_rows(slot_)
        for c in range(tc // COMBINE_CHUNK):
            rows = slice(c * COMBINE_CHUNK, (c + 1) * COMBINE_CHUNK)
            issue_rows(pos_next_ref, 1 - slot_, range(c * COMBINE_CHUNK, (c + 1) * COMBINE_CHUNK))
            gates = r_ref[rows, :]
            moe = (ybuf[slot_, 0, rows, :] * gates[:, TOP_K:TOP_K + 1]
                   + ybuf[slot_, 1, rows, :] * gates[:, TOP_K + 1:TOP_K + 2])
            o_ref[rows, :] = _layer_norm(DN_ALPHA * h_ref[rows, :] + moe, g_ref[...], beta_ref[...])

    for parity in range(2):
        @pl.when(slot == parity)
        def _():
            step(parity)

    @pl.when(i == n - 1)
    def _():
        for parity in range(2):
            @pl.when(slot == parity)
            def _():
                wait_rows(1 - parity)


def _combine(pos, y_rows, h2, rout, g, b, tc):
    s = h2.shape[0]
    n = s // tc
    pos = pos.reshape(n, 1, tc * TOP_K)
    smem_block = functools.partial(pl.BlockSpec, (1, 1, tc * TOP_K), memory_space=pltpu.SMEM)
    return pl.pallas_call(
        functools.partial(_combine_kernel, tc=tc),
        grid=(n,),
        in_specs=[pl.BlockSpec(memory_space=pl.ANY),
                  smem_block(lambda i: (i, 0, 0)),
                  smem_block(lambda i: (jnp.minimum(i + 1, n - 1), 0, 0)),
                  pl.BlockSpec((tc, D_MODEL), lambda i: (i, 0)),
                  pl.BlockSpec((tc, LANES), lambda i: (i, 0)),
                  pl.BlockSpec((1, D_MODEL), lambda i: (0, 0)),
                  pl.BlockSpec((1, D_MODEL), lambda i: (0, 0))],
        out_specs=pl.BlockSpec((tc, D_MODEL), lambda i: (i, 0)),
        out_shape=jax.ShapeDtypeStruct((s, D_MODEL), F32),
        scratch_shapes=[pltpu.VMEM((2, TOP_K, tc, D_MODEL), F32),
                        pltpu.SemaphoreType.DMA((2,))],
        compiler_params=_cparams(("arbitrary",)),
        name="combine_ln3",
    )(y_rows, pos, pos, h2, rout, g, b)


def kernel(x, mem, w_in, gm_ln_g, gm_ln_b, gm_w_s, gm_b_s, w_mix_out, ln1_g, ln1_b,
           mem_w_q, mem_w_k, mem_w_v, mem_w_o, ln2_g, ln2_b,
           w_group, b_group, w_router, b_router, w1, w3, w2, ln3_g, ln3_b):
    assert x.shape == (1, SEQ, D_MODEL) and w_in.shape[0] == 1
    xs = x[0]
    l = 0
    w_mix = w_mix_out[l].astype(BF16)
    w_q = mem_w_q[l].astype(BF16)
    w_o = mem_w_o[l].astype(BF16)
    w_r = jnp.concatenate(
        [w_router[l].transpose(1, 0, 2).reshape(D_MODEL, N_EXPERTS), w_group[l],
         jnp.zeros((D_MODEL, LANES - N_EXPERTS - N_GROUPS), F32)], axis=1)
    wr_hi_lo = jnp.concatenate(_split_bf16(w_r), axis=1)
    b_r = jnp.concatenate([b_router[l].reshape(-1), b_group[l],
                           jnp.zeros((LANES - N_EXPERTS - N_GROUPS,), F32)]).reshape(1, LANES)
    row = lambda v: v.reshape(1, D_MODEL)

    uvk, qvt = _in_proj(xs, w_in[l], tm=1024, q_scale=SB_HEAD_DIM ** -0.5 * math.log2(math.e))
    mix_a = _gmlp(uvk, gm_w_s[l], gm_ln_g[l], gm_ln_b[l], gm_b_s[l].T, tg=512)
    mix_b = _stick_breaking(qvt, uvk, tq=256, bk=128)
    h1 = _mixout(mix_a, mix_b, w_mix, xs, row(ln1_g[l]), row(ln1_b[l]), tm=512)
    k_mem = _proj(mem[0], mem_w_k[l], tm=MEM_LEN, tn=1024)
    v_mem = _proj(mem[0], mem_w_v[l], tm=MEM_LEN, tn=1024)
    h2, rout = _memattn(h1, w_q, k_mem, v_mem, w_o, row(ln2_g[l]), row(ln2_b[l]), wr_hi_lo, b_r, tm=512)
    flat_e = rout[:, 0:TOP_K].astype(I32).reshape(1, N_ASSIGN)
    row_tok, pos, meta = _dispatch(flat_e)
    blk_start, nblk = meta[:, 0], meta[:, 1]
    total = (blk_start[N_EXPERTS - 1] + nblk[N_EXPERTS - 1]).reshape(1)
    y_rows = _experts(blk_start, nblk, total, row_tok, h2, w1[l], w3[l], w2[l])
    out = _combine(pos.reshape(N_ASSIGN), y_rows, h2, rout, row(ln3_g[l]), row(ln3_b[l]), tc=256)
    return out.reshape(1, SEQ, D_MODEL)
```

```python
import functools
import math

import jax
import jax.numpy as jnp
from jax import lax
from jax.experimental import pallas as pl
from jax.experimental.pallas import tpu as pltpu

D_MODEL = 2048
SEQ = 8192
CHUNK = 64
MEM_LEN = 256
GM_BLOCK = 128
GM_GROUPS = 8
GM_WIDTH = 1024
SB_HEADS = 8
SB_HEAD_DIM = 128
SB_WIDTH = 1024
MEM_HEADS = 4
MEM_HEAD_DIM = 512
N_GROUPS = 8
EXPERTS_PER_GROUP = 8
N_EXPERTS = 64
TOP_K = 2
D_EXPERT = 512
DN_ALPHA = 2.0 ** 0.25
LN_EPS = 1e-5

LANES = 128
MOE_ROWS = 128
MOE_ROWS_LOG2 = 7
N_ASSIGN = SEQ * TOP_K
N_ROW_BLOCKS = N_ASSIGN // MOE_ROWS + N_EXPERTS
N_ROWS = N_ROW_BLOCKS * MOE_ROWS
DISPATCH_CHUNK = 512
GATHER_SLOTS = 3
WEIGHT_SLOTS = 3
COMBINE_CHUNK = 32
SB_UNDERFLOW = 160.0
SB_LOOP_BLOCKS = 1
SB_STATIC_BELOW = 2
VMEM_LIMIT = 56 * 1024 * 1024

BF16 = jnp.bfloat16
F32 = jnp.float32
I32 = jnp.int32


def _cparams(sem):
    return pltpu.CompilerParams(dimension_semantics=sem, vmem_limit_bytes=VMEM_LIMIT)


def _resident(shape, index_map):
    return pl.BlockSpec(shape, index_map, pipeline_mode=pl.Buffered(1))


def _layer_norm(y, g, b):
    mu = jnp.mean(y, axis=-1, keepdims=True)
    yc = y - mu
    var = jnp.mean(yc * yc, axis=-1, keepdims=True)
    return yc * lax.rsqrt(var + LN_EPS) * g + b


def _split_bf16(v):
    hi = v.astype(BF16)
    lo = (v - hi.astype(F32)).astype(BF16)
    return hi, lo


def _proj_kernel(x_ref, w_ref, o_ref, xb_ref):
    @pl.when(pl.program_id(1) == 0)
    def _():
        xb_ref[...] = x_ref[...].astype(BF16)

    o_ref[...] = jnp.dot(xb_ref[...], w_ref[...].astype(BF16),
                         preferred_element_type=F32).astype(o_ref.dtype)


def _proj(x, w, tm, tn):
    m, k = x.shape
    n = w.shape[1]
    return pl.pallas_call(
        _proj_kernel,
        grid=(m // tm, n // tn),
        in_specs=[pl.BlockSpec((tm, k), lambda i, j: (i, 0)),
                  pl.BlockSpec((k, tn), lambda i, j: (0, j))],
        out_specs=pl.BlockSpec((tm, tn), lambda i, j: (i, j)),
        out_shape=jax.ShapeDtypeStruct((m, n), BF16),
        scratch_shapes=[pltpu.VMEM((tm, k), BF16)],
        compiler_params=_cparams(("arbitrary", "arbitrary")),
        name="proj",
    )(x, w)


IN_BLOCKS = 5
IN_Q, IN_K, IN_V = 2, 3, 4


def _in_proj_kernel(x_ref, w_ref, o_ref, ot_ref, xb_ref, *, q_scale):
    j = pl.program_id(1)

    @pl.when(j == 0)
    def _():
        xb_ref[...] = x_ref[...].astype(BF16)

    wb = w_ref[...].astype(BF16)
    is_t = jnp.logical_or(j == IN_Q, j == IN_V)

    @pl.when(jnp.logical_not(is_t))
    def _():
        o_ref[...] = jnp.dot(xb_ref[...], wb, preferred_element_type=F32).astype(o_ref.dtype)

    @pl.when(is_t)
    def _():
        acc = lax.dot_general(wb, xb_ref[...], (((0,), (1,)), ((), ())), preferred_element_type=F32)
        ot_ref[...] = (acc * jnp.where(j == IN_Q, q_scale, 1.0).astype(F32)).astype(ot_ref.dtype)


def _in_proj(x, w, tm, q_scale):
    m, k = x.shape
    tn = SB_WIDTH
    omap = lambda i, j: (i, j - (j >= IN_Q).astype(I32) - (j >= IN_V).astype(I32))
    otmap = lambda i, j: ((j >= IN_V).astype(I32), i)
    return pl.pallas_call(
        functools.partial(_in_proj_kernel, q_scale=q_scale),
        grid=(m // tm, IN_BLOCKS),
        in_specs=[pl.BlockSpec((tm, k), lambda i, j: (i, 0)),
                  pl.BlockSpec((k, tn), lambda i, j: (0, j))],
        out_specs=[pl.BlockSpec((tm, tn), omap),
                   pl.BlockSpec((tn, tm), otmap)],
        out_shape=[jax.ShapeDtypeStruct((m, 3 * tn), BF16),
                   jax.ShapeDtypeStruct((2 * tn, m), BF16)],
        scratch_shapes=[pltpu.VMEM((tm, k), BF16)],
        compiler_params=_cparams(("arbitrary", "arbitrary")),
        name="in_proj",
    )(x, w)


def _gelu_tanh(x):
    c1 = math.sqrt(2.0 / math.pi)
    return x * (0.5 + 0.5 * jnp.tanh(x * (c1 + (0.044715 * c1) * (x * x))))


def _gmlp_kernel(u_ref, v_ref, ws_ref, g_ref, b_ref, bs_ref, o_ref, *, tg):
    t_chunk = lax.broadcasted_iota(I32, (GM_BLOCK, GM_BLOCK), 0) // CHUNK
    s_chunk = lax.broadcasted_iota(I32, (GM_BLOCK, GM_BLOCK), 1) // CHUNK
    causal = t_chunk >= s_chunk
    for g in range(GM_GROUPS):
        wm = jnp.where(causal, ws_ref[g], 0.0).astype(BF16)
        ln_g = g_ref[g:g + 1, :]
        ln_b = b_ref[g:g + 1, :]
        bias = bs_ref[:, g:g + 1]
        cols = slice(g * LANES, (g + 1) * LANES)
        for n in range(tg // GM_BLOCK):
            rows = slice(n * GM_BLOCK, (n + 1) * GM_BLOCK)
            v = _gelu_tanh(v_ref[rows, cols].astype(F32))
            vg = _layer_norm(v, ln_g, ln_b).astype(BF16)
            mixed = jnp.dot(wm, vg, preferred_element_type=F32) + bias
            u = _gelu_tanh(u_ref[rows, cols].astype(F32))
            o_ref[rows, cols] = (u * mixed).astype(o_ref.dtype)


def _gmlp(uvk, gm_w_s, gm_ln_g, gm_ln_b, gm_b_s_t, tg):
    s = uvk.shape[0]
    return pl.pallas_call(
        functools.partial(_gmlp_kernel, tg=tg),
        grid=(s // tg,),
        in_specs=[pl.BlockSpec((tg, GM_WIDTH), lambda i: (i, 0)),
                  pl.BlockSpec((tg, GM_WIDTH), lambda i: (i, 1)),
                  pl.BlockSpec((GM_GROUPS, GM_BLOCK, GM_BLOCK), lambda i: (0, 0, 0)),
                  pl.BlockSpec((GM_GROUPS, LANES), lambda i: (0, 0)),
                  pl.BlockSpec((GM_GROUPS, LANES), lambda i: (0, 0)),
                  pl.BlockSpec((GM_BLOCK, GM_GROUPS), lambda i: (0, 0))],
        out_specs=pl.BlockSpec((tg, GM_WIDTH), lambda i: (i, 0)),
        out_shape=jax.ShapeDtypeStruct((s, GM_WIDTH), BF16),
        compiler_params=_cparams(("arbitrary",)),
        name="gmlp",
    )(uvk, uvk, gm_w_s, gm_ln_g, gm_ln_b, gm_b_s_t)


def _softplus2(u):
    return jnp.maximum(u, 0.0) + jnp.log2(1.0 + jnp.exp2(-jnp.abs(u)))


def _sb_kernel(qt_ref, k_ref, vt_ref, o_ref, *acc_refs, tq, bk):
    q0 = pl.program_id(0) * tq
    upper = (lax.broadcasted_iota(I32, (bk, bk), 1)
             >= lax.broadcasted_iota(I32, (bk, bk), 0)).astype(BF16)
    upper2 = jnp.concatenate([upper, upper], axis=1)

    heads = range(SB_HEADS)
    hrows = [slice(h * SB_HEAD_DIM, (h + 1) * SB_HEAD_DIM) for h in heads]

    def tiles(blocks, carries):
        zs, sums, masks = [], [], []
        for ks, lo, masked in blocks:
            width = tq - lo
            zs.append([jnp.dot(k_ref[pl.ds(ks, bk), hrows[h]], qt_ref[hrows[h], lo:],
                               preferred_element_type=F32) for h in heads])
            if masked is True:
                masks.append(lax.broadcasted_iota(I32, (bk, width), 0)
                             < lax.broadcasted_iota(I32, (bk, width), 1))
            elif masked is False:
                masks.append(None)
            else:
                masks.append(jnp.broadcast_to(masked, (bk, width)))
        for zb, mask in zip(zs, masks):
            sps = [_softplus2(z) for z in zb]
            if mask is not None:
                sps = [jnp.where(mask, sp, 0.0) for sp in sps]
            sums.append([jnp.dot(upper2, jnp.concatenate(_split_bf16(sp), axis=0),
                                 preferred_element_type=F32) for sp in sps])
        csums = []
        for (ks, lo, masked), sb in zip(blocks, sums):
            cb = [sb[h] + carries[h][:, lo:] for h in heads]
            carries = [cb[h][0:1, :] if lo == 0
                       else jnp.concatenate([carries[h][:, :lo], cb[h][0:1, :]], axis=1)
                       for h in heads]
            csums.append(cb)
        probs = []
        for zb, cb, mask in zip(zs, csums, masks):
            pb = [jnp.exp2(zb[h] - cb[h]) for h in heads]
            if mask is not None:
                pb = [jnp.where(mask, a, 0.0) for a in pb]
            probs.append([a.astype(BF16) for a in pb])
        for (ks, lo, masked), pb in zip(blocks, probs):
            for h in heads:
                acc_refs[h][:, lo:] += jnp.dot(vt_ref[hrows[h], pl.ds(ks, bk)], pb[h],
                                               preferred_element_type=F32)
        return carries

    for acc_ref in acc_refs:
        acc_ref[...] = jnp.zeros_like(acc_ref)
    carries = [jnp.zeros((1, tq), F32) for _ in range(SB_HEADS)]
    blocks = [(pl.multiple_of(q0 + d * bk, bk), d * bk, True) for d in reversed(range(tq // bk))]
    for b in range(1, SB_STATIC_BELOW + 1):
        below = q0 // bk - b
        blocks.append((pl.multiple_of(jnp.maximum(below, 0) * bk, bk), 0, below >= 0))
    carries = tiles(blocks, carries)

    def cond(state):
        j = state[0]
        lowest = functools.reduce(jnp.minimum, state[1:])
        return jnp.logical_and(j >= 0, jnp.min(lowest) < SB_UNDERFLOW)

    def body(state):
        j = state[0]
        blocks = [(pl.multiple_of((j - b) * bk, bk), 0, False) for b in range(SB_LOOP_BLOCKS)]
        return (j - SB_LOOP_BLOCKS,) + tuple(tiles(blocks, list(state[1:])))

    assert (tq // bk) % SB_LOOP_BLOCKS == 0
    lax.while_loop(cond, body, (q0 // bk - 1 - SB_STATIC_BELOW,) + tuple(carries))
    for h in range(SB_HEADS):
        o_ref[hrows[h], :] = acc_refs[h][...].astype(o_ref.dtype)


def _stick_breaking(qvt, uvk, tq, bk):
    s = uvk.shape[0]
    return pl.pallas_call(
        functools.partial(_sb_kernel, tq=tq, bk=bk),
        grid=(s // tq,),
        in_specs=[pl.BlockSpec((SB_WIDTH, tq), lambda i: (0, i)),
                  _resident((s, SB_WIDTH), lambda i: (0, 2 * GM_WIDTH // SB_WIDTH)),
                  _resident((SB_WIDTH, s), lambda i: (1, 0))],
        out_specs=pl.BlockSpec((SB_WIDTH, tq), lambda i: (0, i)),
        out_shape=jax.ShapeDtypeStruct((SB_WIDTH, s), BF16),
        scratch_shapes=[pltpu.VMEM((SB_HEAD_DIM, tq), F32) for _ in range(SB_HEADS)],
        compiler_params=_cparams(("arbitrary",)),
        name="stick_breaking",
    )(qvt, uvk, qvt)


def _mixout_kernel(a_ref, bt_ref, wa_ref, wb_ref, x_ref, g_ref, beta_ref, o_ref):
    mixed = (jnp.dot(a_ref[...], wa_ref[...], preferred_element_type=F32)
             + lax.dot_general(bt_ref[...], wb_ref[...], (((0,), (0,)), ((), ())),
                               preferred_element_type=F32))
    y = DN_ALPHA * x_ref[...] + mixed
    o_ref[...] = _layer_norm(y, g_ref[...], beta_ref[...])


def _mixout(mix_a, mix_b, w, x, g, b, tm):
    s = x.shape[0]
    return pl.pallas_call(
        _mixout_kernel,
        grid=(s // tm,),
        in_specs=[pl.BlockSpec((tm, GM_WIDTH), lambda i: (i, 0)),
                  pl.BlockSpec((SB_WIDTH, tm), lambda i: (0, i)),
                  _resident((GM_WIDTH, D_MODEL), lambda i: (0, 0)),
                  _resident((SB_WIDTH, D_MODEL), lambda i: (1, 0)),
                  pl.BlockSpec((tm, D_MODEL), lambda i: (i, 0)),
                  pl.BlockSpec((1, D_MODEL), lambda i: (0, 0)),
                  pl.BlockSpec((1, D_MODEL), lambda i: (0, 0))],
        out_specs=pl.BlockSpec((tm, D_MODEL), lambda i: (i, 0)),
        out_shape=jax.ShapeDtypeStruct((s, D_MODEL), F32),
        compiler_params=_cparams(("arbitrary",)),
        name="mixout_ln1",
    )(mix_a, mix_b, w, w, x, g, b)


def _route(logits):
    lane = lax.broadcasted_iota(I32, logits.shape, 1)
    neg = jnp.float32(-jnp.inf)
    big = jnp.int32(1 << 20)

    def first_argmax(vals):
        m = jnp.max(vals, axis=-1, keepdims=True)
        idx = jnp.min(jnp.where(vals == m, lane, big), axis=-1, keepdims=True)
        return m, idx

    is_group = jnp.logical_and(lane >= N_EXPERTS, lane < N_EXPERTS + N_GROUPS)
    gl = jnp.where(is_group, logits, neg)
    gmax, glane = first_argmax(gl)
    g_val = 1.0 / jnp.sum(jnp.exp(gl - gmax), axis=-1, keepdims=True)
    gidx = glane - N_EXPERTS
    in_group = jnp.logical_and(lane >= gidx * EXPERTS_PER_GROUP,
                               lane < (gidx + 1) * EXPERTS_PER_GROUP)
    el = jnp.where(in_group, logits, neg)
    m1, i1 = first_argmax(el)
    m2, i2 = first_argmax(jnp.where(lane == i1, neg, el))
    e2 = jnp.exp(m2 - m1)
    gate1 = g_val / (1.0 + e2)
    gate2 = g_val * e2 / (1.0 + e2)
    return jnp.where(lane == 0, i1.astype(F32),
                     jnp.where(lane == 1, i2.astype(F32),
                               jnp.where(lane == 2, gate1,
                                         jnp.where(lane == 3, gate2, 0.0))))


def _memattn_kernel(h_ref, wq_ref, km_ref, vm_ref, wo_ref, g_ref, beta_ref, wr_ref, br_ref,
                    o_ref, r_ref):
    h = h_ref[...]
    q = jnp.dot(h.astype(BF16), wq_ref[...], preferred_element_type=F32).astype(BF16)
    outs = []
    for hd in range(MEM_HEADS):
        cols = slice(hd * MEM_HEAD_DIM, (hd + 1) * MEM_HEAD_DIM)
        kh = km_ref[:, cols]
        vh = vm_ref[:, cols]
        s = lax.dot_general(q[:, cols], kh, (((1,), (1,)), ((), ())),
                            preferred_element_type=F32) * (MEM_HEAD_DIM ** -0.5)
        s = s - jnp.max(s, axis=-1, keepdims=True)
        e = jnp.exp(s)
        p = e * (1.0 / jnp.sum(e, axis=-1, keepdims=True))
        outs.append(jnp.dot(p.astype(BF16), vh, preferred_element_type=F32).astype(BF16))
    o = jnp.concatenate(outs, axis=-1)
    attn = jnp.dot(o, wo_ref[...], preferred_element_type=F32)
    h2 = _layer_norm(DN_ALPHA * h + attn, g_ref[...], beta_ref[...])
    o_ref[...] = h2
    h_hi, h_lo = _split_bf16(h2)
    both = jnp.dot(h_hi, wr_ref[...], preferred_element_type=F32)
    logits = (both[:, :LANES] + both[:, LANES:]
              + jnp.dot(h_lo, wr_ref[:, :LANES], preferred_element_type=F32)) + br_ref[...]
    r_ref[...] = _route(logits)


def _memattn(h1, wq, k_mem, v_mem, wo, g, b, wr_hi_lo, b_r, tm):
    s = h1.shape[0]
    return pl.pallas_call(
        _memattn_kernel,
        grid=(s // tm,),
        in_specs=[pl.BlockSpec((tm, D_MODEL), lambda i: (i, 0)),
                  _resident((D_MODEL, D_MODEL), lambda i: (0, 0)),
                  _resident((MEM_LEN, D_MODEL), lambda i: (0, 0)),
                  _resident((MEM_LEN, D_MODEL), lambda i: (0, 0)),
                  _resident((D_MODEL, D_MODEL), lambda i: (0, 0)),
                  pl.BlockSpec((1, D_MODEL), lambda i: (0, 0)),
                  pl.BlockSpec((1, D_MODEL), lambda i: (0, 0)),
                  _resident((D_MODEL, 2 * LANES), lambda i: (0, 0)),
                  pl.BlockSpec((1, LANES), lambda i: (0, 0))],
        out_specs=[pl.BlockSpec((tm, D_MODEL), lambda i: (i, 0)),
                   pl.BlockSpec((tm, LANES), lambda i: (i, 0))],
        out_shape=[jax.ShapeDtypeStruct((s, D_MODEL), F32),
                   jax.ShapeDtypeStruct((s, LANES), F32)],
        compiler_params=_cparams(("arbitrary",)),
        name="memattn_ln2_router",
    )(h1, wq, k_mem, v_mem, wo, g, b, wr_hi_lo, b_r)


def _dispatch_kernel(e_ref, tok_ref, pos_ref, meta_ref):
    ca = DISPATCH_CHUNK
    n_chunks = N_ASSIGN // ca
    e_iota = lax.broadcasted_iota(I32, (N_EXPERTS, ca), 0)

    def onehot(c):
        e_c = e_ref[:, pl.ds(pl.multiple_of(c * ca, ca), ca)]
        return (e_iota == e_c).astype(F32)

    counts = lax.fori_loop(
        0, n_chunks, lambda c, acc: acc + jnp.sum(onehot(c), axis=1, keepdims=True),
        jnp.zeros((N_EXPERTS, 1), F32))
    nblk = jnp.right_shift(counts.astype(I32) + (MOE_ROWS - 1), MOE_ROWS_LOG2)
    strict_lower = (lax.broadcasted_iota(I32, (N_EXPERTS, N_EXPERTS), 1)
                    < lax.broadcasted_iota(I32, (N_EXPERTS, N_EXPERTS), 0)).astype(BF16)
    nblk_lanes = jnp.broadcast_to(nblk.astype(F32), (N_EXPERTS, LANES)).astype(BF16)
    blk_start = jnp.dot(strict_lower, nblk_lanes, preferred_element_type=F32)
    pad_start = blk_start[:, 0:1] * MOE_ROWS
    lane = lax.broadcasted_iota(I32, (N_EXPERTS, LANES), 1)
    meta_ref[...] = jnp.where(lane == 0, blk_start.astype(I32),
                              jnp.where(lane == 1, jnp.broadcast_to(nblk, (N_EXPERTS, LANES)), 0))

    earlier = (lax.broadcasted_iota(I32, (ca, ca), 0)
               < lax.broadcasted_iota(I32, (ca, ca), 1)).astype(BF16)
    blk_iota = lax.broadcasted_iota(I32, (N_ROW_BLOCKS, ca), 0)
    row_iota = lax.broadcasted_iota(I32, (MOE_ROWS, ca), 0)
    a_iota = lax.broadcasted_iota(I32, (1, ca), 1)

    def body(c, state):
        seen, acc_hi, acc_lo = state
        oh = onehot(c)
        before = jnp.dot(oh.astype(BF16), earlier, preferred_element_type=F32) + seen
        dest = jnp.sum(oh * (before + pad_start), axis=0, keepdims=True).astype(I32)
        pos_ref[:, pl.ds(pl.multiple_of(c * ca, ca), ca)] = dest
        in_blk = (blk_iota == jnp.right_shift(dest, MOE_ROWS_LOG2)).astype(F32)
        in_row = (row_iota == jnp.bitwise_and(dest, MOE_ROWS - 1)).astype(BF16)
        tok = jnp.right_shift(c * ca + a_iota, 1)
        tok_hi = (jnp.right_shift(tok, 7) + 1).astype(F32)
        tok_lo = jnp.bitwise_and(tok, 127).astype(F32)
        nt = (((1,), (1,)), ((), ()))
        acc_hi = acc_hi + lax.dot_general((in_blk * tok_hi).astype(BF16), in_row, nt,
                                          preferred_element_type=F32)
        acc_lo = acc_lo + lax.dot_general((in_blk * tok_lo).astype(BF16), in_row, nt,
                                          preferred_element_type=F32)
        return seen + jnp.sum(oh, axis=1, keepdims=True), acc_hi, acc_lo

    zeros = jnp.zeros((N_ROW_BLOCKS, MOE_ROWS), F32)
    _, acc_hi, acc_lo = lax.fori_loop(0, n_chunks, body,
                                      (jnp.zeros((N_EXPERTS, 1), F32), zeros, zeros))
    pad_tok = jnp.bitwise_and(
        lax.broadcasted_iota(I32, (N_ROW_BLOCKS, MOE_ROWS), 0) * MOE_ROWS
        + lax.broadcasted_iota(I32, (N_ROW_BLOCKS, MOE_ROWS), 1), SEQ - 1)
    tok_ref[...] = jnp.where(acc_hi > 0.5, ((acc_hi - 1.0) * 128.0 + acc_lo).astype(I32), pad_tok)


def _dispatch(flat_e):
    return pl.pallas_call(
        _dispatch_kernel,
        out_shape=[jax.ShapeDtypeStruct((N_ROW_BLOCKS, MOE_ROWS), I32),
                   jax.ShapeDtypeStruct((1, N_ASSIGN), I32),
                   jax.ShapeDtypeStruct((N_EXPERTS, LANES), I32)],
        compiler_params=pltpu.CompilerParams(vmem_limit_bytes=VMEM_LIMIT),
        name="moe_dispatch",
    )(flat_e)


def _expert_kernel(bs_ref, nb_ref, tot_ref, tok_hbm, h_hbm, w1_hbm, w3_hbm, w2_hbm, y_hbm,
                   idx, xbuf, ybuf, isem, gsem, ysem, wsem, w1f, w3f, w2f, w1b, w3b, w2b, zbuf, zsem):
    e = pl.program_id(0)
    n_exp = pl.num_programs(0)
    total = tot_ref[0]
    first = bs_ref[e]
    nblk = nb_ref[e]

    def idx_copy(blk):
        row = jnp.minimum(blk, N_ROW_BLOCKS - 1)
        slot = blk % GATHER_SLOTS
        return pltpu.make_async_copy(tok_hbm.at[pl.ds(row, 1)], idx.at[pl.ds(slot, 1)],
                                     isem.at[slot])

    def issue_rows(blk, rows):
        slot = blk % GATHER_SLOTS
        for r in rows:
            tok = idx[slot, r]
            pltpu.make_async_copy(h_hbm.at[pl.ds(tok, 1)], xbuf.at[slot, pl.ds(r, 1)],
                                  gsem.at[slot]).start()

    def wait_gather(blk):
        slot = blk % GATHER_SLOTS
        pltpu.make_async_copy(h_hbm.at[pl.ds(0, MOE_ROWS)], xbuf.at[slot], gsem.at[slot]).wait()

    def y_copy(blk, slot):
        return pltpu.make_async_copy(ybuf.at[slot], y_hbm.at[pl.ds(blk * MOE_ROWS, MOE_ROWS)],
                                     ysem.at[slot])

    def zero_copy(blk):
        return pltpu.make_async_copy(zbuf, y_hbm.at[pl.ds(blk * MOE_ROWS, MOE_ROWS)], zsem)

    def weight_copies(ex):
        slot = ex % WEIGHT_SLOTS
        return [pltpu.make_async_copy(src.at[ex], dst.at[slot], wsem.at[slot])
                for src, dst in ((w1_hbm, w1f), (w3_hbm, w3f), (w2_hbm, w2f))]

    @pl.when(e == 0)
    def _():
        for ex in range(WEIGHT_SLOTS - 1):
            for cp in weight_copies(ex):
                cp.start(priority=1)

        zbuf[...] = jnp.zeros_like(zbuf)

        def start_zero(blk, c):
            zero_copy(blk).start()
            return c
        lax.fori_loop(total, N_ROW_BLOCKS, start_zero, 0)

        for blk in range(GATHER_SLOTS):
            idx_copy(blk).start()
        for blk in range(GATHER_SLOTS - 1):
            idx_copy(blk).wait()

        def prime(r, c):
            for blk in range(GATHER_SLOTS - 1):
                pltpu.make_async_copy(h_hbm.at[pl.ds(idx[blk, r], 1)], xbuf.at[blk, pl.ds(r, 1)],
                                      gsem.at[blk]).start()
            return c
        lax.fori_loop(0, MOE_ROWS, prime, 0, unroll=8)

    @pl.when(e + WEIGHT_SLOTS - 1 < n_exp)
    def _():
        for cp in weight_copies(e + WEIGHT_SLOTS - 1):
            cp.start(priority=1)

    for cp in weight_copies(e):
        cp.wait()

    @pl.when(nblk > 0)
    def _():
        wslot = e % WEIGHT_SLOTS
        w1b[...] = w1f[wslot].astype(BF16)
        w3b[...] = w3f[wslot].astype(BF16)
        w2b[...] = w2f[wslot].astype(BF16)

    n_piece = 4
    rows_per_piece = MOE_ROWS // n_piece
    cols_per_piece = D_MODEL // n_piece

    def block(b, c):
        blk = first + b
        slot = blk % 2
        wait_gather(blk)
        idx_copy(blk + GATHER_SLOTS - 1).wait()
        idx_copy(blk + GATHER_SLOTS).start()

        @pl.when(blk >= 2)
        def _():
            y_copy(blk - 2, slot).wait()

        xb = xbuf[blk % GATHER_SLOTS].astype(BF16)
        a = jnp.dot(xb, w1b[...], preferred_element_type=F32)
        g = jnp.dot(xb, w3b[...], preferred_element_type=F32)
        hid = (a * jax.nn.sigmoid(a) * g).astype(BF16)
        for p in range(n_piece):
            issue_rows(blk + GATHER_SLOTS - 1,
                       range(p * rows_per_piece, (p + 1) * rows_per_piece))
            cols = slice(p * cols_per_piece, (p + 1) * cols_per_piece)
            ybuf[slot, :, cols] = jnp.dot(hid, w2b[:, cols], preferred_element_type=F32)
        y_copy(blk, slot).start()
        return c

    lax.fori_loop(0, nblk, block, 0)

    @pl.when(e == n_exp - 1)
    def _():
        for ahead in range(GATHER_SLOTS - 1):
            wait_gather(total + ahead)
        idx_copy(total + GATHER_SLOTS - 1).wait()
        for back in (1, 2):
            @pl.when(total >= back)
            def _():
                y_copy(total - back, (total - back) % 2).wait()

        def wait_zero(blk, c):
            zero_copy(blk).wait()
            return c
        lax.fori_loop(total, N_ROW_BLOCKS, wait_zero, 0)


def _experts(blk_start, nblk, total, row_tok, h2, w1, w3, w2):
    grid_spec = pltpu.PrefetchScalarGridSpec(
        num_scalar_prefetch=3,
        grid=(N_EXPERTS,),
        in_specs=[pl.BlockSpec(memory_space=pl.ANY)] * 5,
        out_specs=pl.BlockSpec(memory_space=pl.ANY),
        scratch_shapes=[pltpu.SMEM((GATHER_SLOTS, MOE_ROWS), I32),
                        pltpu.VMEM((GATHER_SLOTS, MOE_ROWS, D_MODEL), F32),
                        pltpu.VMEM((2, MOE_ROWS, D_MODEL), F32),
                        pltpu.SemaphoreType.DMA((GATHER_SLOTS,)),
                        pltpu.SemaphoreType.DMA((GATHER_SLOTS,)),
                        pltpu.SemaphoreType.DMA((2,)),
                        pltpu.SemaphoreType.DMA((WEIGHT_SLOTS,)),
                        pltpu.VMEM((WEIGHT_SLOTS, D_MODEL, D_EXPERT), F32),
                        pltpu.VMEM((WEIGHT_SLOTS, D_MODEL, D_EXPERT), F32),
                        pltpu.VMEM((WEIGHT_SLOTS, D_EXPERT, D_MODEL), F32),
                        pltpu.VMEM((D_MODEL, D_EXPERT), BF16),
                        pltpu.VMEM((D_MODEL, D_EXPERT), BF16),
                        pltpu.VMEM((D_EXPERT, D_MODEL), BF16),
                        pltpu.VMEM((MOE_ROWS, D_MODEL), F32),
                        pltpu.SemaphoreType.DMA(())],
    )
    return pl.pallas_call(
        _expert_kernel,
        grid_spec=grid_spec,
        out_shape=jax.ShapeDtypeStruct((N_ROWS, D_MODEL), F32),
        compiler_params=_cparams(("arbitrary",)),
        name="experts",
    )(blk_start, nblk, total, row_tok, h2, w1, w3, w2)


def _combine_kernel(y_hbm, pos_ref, pos_next_ref, h_ref, r_ref, g_ref, beta_ref, o_ref, ybuf, sem,
                    *, tc):
    i = pl.program_id(0)
    n = pl.num_programs(0)
    slot = i % 2

    def issue_rows(table, slot_, rows):
        for r in rows:
            for kk in range(TOP_K):
                p = table[0, 0, r * TOP_K + kk]
                pltpu.make_async_copy(y_hbm.at[pl.ds(p, 1)], ybuf.at[slot_, kk, pl.ds(r, 1)],
                                      sem.at[slot_]).start(priority=kk)

    def wait_rows(slot_):
        for kk in range(TOP_K):
            pltpu.make_async_copy(y_hbm.at[pl.ds(0, tc)], ybuf.at[slot_, kk], sem.at[slot_]).wait()

    @pl.when(i == 0)
    def _():
        issue_rows(pos_ref, 0, range(tc))

    def step(slot_):
        wait_rows(slot_)
        for c in range(tc // COMBINE_CHUNK):
            rows = slice(c * COMBINE_CHUNK, (c + 1) * COMBINE_CHUNK)
            issue_rows(pos_next_ref, 1 - slot_, range(c * COMBINE_CHUNK, (c + 1) * COMBINE_CHUNK))
            gates = r_ref[rows, :]
            moe = (ybuf[slot_, 0, rows, :] * gates[:, TOP_K:TOP_K + 1]
                   + ybuf[slot_, 1, rows, :] * gates[:, TOP_K + 1:TOP_K + 2])
            o_ref[rows, :] = _layer_norm(DN_ALPHA * h_ref[rows, :] + moe, g_ref[...], beta_ref[...])

    for parity in range(2):
        @pl.when(slot == parity)
        def _():
            step(parity)

    @pl.when(i == n - 1)
    def _():
        for parity in range(2):
            @pl.when(slot == parity)
            def _():
                wait_rows(1 - parity)


def _combine(pos, y_rows, h2, rout, g, b, tc):
    s = h2.shape[0]
    n = s // tc
    pos = pos.reshape(n, 1, tc * TOP_K)
    smem_block = functools.partial(pl.BlockSpec, (1, 1, tc * TOP_K), memory_space=pltpu.SMEM)
    return pl.pallas_call(
        functools.partial(_combine_kernel, tc=tc),
        grid=(n,),
        in_specs=[pl.BlockSpec(memory_space=pl.ANY),
                  smem_block(lambda i: (i, 0, 0)),
                  smem_block(lambda i: (jnp.minimum(i + 1, n - 1), 0, 0)),
                  pl.BlockSpec((tc, D_MODEL), lambda i: (i, 0)),
                  pl.BlockSpec((tc, LANES), lambda i: (i, 0)),
                  pl.BlockSpec((1, D_MODEL), lambda i: (0, 0)),
                  pl.BlockSpec((1, D_MODEL), lambda i: (0, 0))],
        out_specs=pl.BlockSpec((tc, D_MODEL), lambda i: (i, 0)),
        out_shape=jax.ShapeDtypeStruct((s, D_MODEL), F32),
        scratch_shapes=[pltpu.VMEM((2, TOP_K, tc, D_MODEL), F32),
                        pltpu.SemaphoreType.DMA((2,))],
        compiler_params=_cparams(("arbitrary",)),
        name="combine_ln3",
    )(y_rows, pos, pos, h2, rout, g, b)


def kernel(x, mem, w_in, gm_ln_g, gm_ln_b, gm_w_s, gm_b_s, w_mix_out, ln1_g, ln1_b,
           mem_w_q, mem_w_k, mem_w_v, mem_w_o, ln2_g, ln2_b,
           w_group, b_group, w_router, b_router, w1, w3, w2, ln3_g, ln3_b):
    assert x.shape == (1, SEQ, D_MODEL) and w_in.shape[0] == 1
    xs = x[0]
    l = 0
    w_mix = w_mix_out[l].astype(BF16)
    w_q = mem_w_q[l].astype(BF16)
    w_o = mem_w_o[l].astype(BF16)
    w_r = jnp.concatenate(
        [w_router[l].transpose(1, 0, 2).reshape(D_MODEL, N_EXPERTS), w_group[l],
         jnp.zeros((D_MODEL, LANES - N_EXPERTS - N_GROUPS), F32)], axis=1)
    wr_hi_lo = jnp.concatenate(_split_bf16(w_r), axis=1)
    b_r = jnp.concatenate([b_router[l].reshape(-1), b_group[l],
                           jnp.zeros((LANES - N_EXPERTS - N_GROUPS,), F32)]).reshape(1, LANES)
    row = lambda v: v.reshape(1, D_MODEL)

    uvk, qvt = _in_proj(xs, w_in[l], tm=1024, q_scale=SB_HEAD_DIM ** -0.5 * math.log2(math.e))
    mix_a = _gmlp(uvk, gm_w_s[l], gm_ln_g[l], gm_ln_b[l], gm_b_s[l].T, tg=512)
    mix_b = _stick_breaking(qvt, uvk, tq=256, bk=128)
    h1 = _mixout(mix_a, mix_b, w_mix, xs, row(ln1_g[l]), row(ln1_b[l]), tm=512)
    k_mem = _proj(mem[0], mem_w_k[l], tm=MEM_LEN, tn=1024)
    v_mem = _proj(mem[0], mem_w_v[l], tm=MEM_LEN, tn=1024)
    h2, rout = _memattn(h1, w_q, k_mem, v_mem, w_o, row(ln2_g[l]), row(ln2_b[l]), wr_hi_lo, b_r, tm=512)
    flat_e = rout[:, 0:TOP_K].astype(I32).reshape(1, N_ASSIGN)
    row_tok, pos, meta = _dispatch(flat_e)
    blk_start, nblk = meta[:, 0], meta[:, 1]
    total = (blk_start[N_EXPERTS - 1] + nblk[N_EXPERTS - 1]).reshape(1)
    y_rows = _experts(blk_start, nblk, total, row_tok, h2, w1[l], w3[l], w2[l])
    out = _combine(pos.reshape(N_ASSIGN), y_rows, h2, rout, row(ln3_g[l]), row(ln3_b[l]), tc=256)
    return out.reshape(1, SEQ, D_MODEL)
```

```python
import functools
import math

import jax
import jax.numpy as jnp
from jax import lax
from jax.experimental import pallas as pl
from jax.experimental.pallas import tpu as pltpu

D_MODEL = 2048
SEQ = 8192
CHUNK = 64
MEM_LEN = 256
GM_BLOCK = 128
GM_GROUPS = 8
GM_WIDTH = 1024
SB_HEADS = 8
SB_HEAD_DIM = 128
SB_WIDTH = 1024
MEM_HEADS = 4
MEM_HEAD_DIM = 512
N_GROUPS = 8
EXPERTS_PER_GROUP = 8
N_EXPERTS = 64
TOP_K = 2
D_EXPERT = 512
DN_ALPHA = 2.0 ** 0.25
LN_EPS = 1e-5

LANES = 128
MOE_ROWS = 128
MOE_ROWS_LOG2 = 7
N_ASSIGN = SEQ * TOP_K
N_ROW_BLOCKS = N_ASSIGN // MOE_ROWS + N_EXPERTS
N_ROWS = N_ROW_BLOCKS * MOE_ROWS
DISPATCH_CHUNK = 1024
GATHER_SLOTS = 3
WEIGHT_SLOTS = 3
COMBINE_CHUNK = 32
SB_UNDERFLOW = 160.0
SB_LOOP_BLOCKS = 1
SB_STATIC_BELOW = 2
VMEM_LIMIT = 56 * 1024 * 1024

BF16 = jnp.bfloat16
F32 = jnp.float32
I32 = jnp.int32


def _cparams(sem):
    return pltpu.CompilerParams(dimension_semantics=sem, vmem_limit_bytes=VMEM_LIMIT)


def _resident(shape, index_map):
    return pl.BlockSpec(shape, index_map, pipeline_mode=pl.Buffered(1))


def _layer_norm(y, g, b):
    mu = jnp.mean(y, axis=-1, keepdims=True)
    yc = y - mu
    var = jnp.mean(yc * yc, axis=-1, keepdims=True)
    return yc * lax.rsqrt(var + LN_EPS) * g + b


def _split_bf16(v):
    hi = v.astype(BF16)
    lo = (v - hi.astype(F32)).astype(BF16)
    return hi, lo


def _proj_kernel(x_ref, w_ref, o_ref, xb_ref):
    @pl.when(pl.program_id(1) == 0)
    def _():
        xb_ref[...] = x_ref[...].astype(BF16)

    o_ref[...] = jnp.dot(xb_ref[...], w_ref[...].astype(BF16),
                         preferred_element_type=F32).astype(o_ref.dtype)


def _proj(x, w, tm, tn):
    m, k = x.shape
    n = w.shape[1]
    return pl.pallas_call(
        _proj_kernel,
        grid=(m // tm, n // tn),
        in_specs=[pl.BlockSpec((tm, k), lambda i, j: (i, 0)),
                  pl.BlockSpec((k, tn), lambda i, j: (0, j))],
        out_specs=pl.BlockSpec((tm, tn), lambda i, j: (i, j)),
        out_shape=jax.ShapeDtypeStruct((m, n), BF16),
        scratch_shapes=[pltpu.VMEM((tm, k), BF16)],
        compiler_params=_cparams(("arbitrary", "arbitrary")),
        name="proj",
    )(x, w)


IN_BLOCKS = 5
IN_Q, IN_K, IN_V = 2, 3, 4


def _in_proj_kernel(x_ref, w_ref, o_ref, ot_ref, xb_ref, *, q_scale):
    j = pl.program_id(1)

    @pl.when(j == 0)
    def _():
        xb_ref[...] = x_ref[...].astype(BF16)

    wb = w_ref[...].astype(BF16)
    is_t = jnp.logical_or(j == IN_Q, j == IN_V)

    @pl.when(jnp.logical_not(is_t))
    def _():
        o_ref[...] = jnp.dot(xb_ref[...], wb, preferred_element_type=F32).astype(o_ref.dtype)

    @pl.when(is_t)
    def _():
        acc = lax.dot_general(wb, xb_ref[...], (((0,), (1,)), ((), ())), preferred_element_type=F32)
        ot_ref[...] = (acc * jnp.where(j == IN_Q, q_scale, 1.0).astype(F32)).astype(ot_ref.dtype)


def _in_proj(x, w, tm, q_scale):
    m, k = x.shape
    tn = SB_WIDTH
    omap = lambda i, j: (i, j - (j >= IN_Q).astype(I32) - (j >= IN_V).astype(I32))
    otmap = lambda i, j: ((j >= IN_V).astype(I32), i)
    return pl.pallas_call(
        functools.partial(_in_proj_kernel, q_scale=q_scale),
        grid=(m // tm, IN_BLOCKS),
        in_specs=[pl.BlockSpec((tm, k), lambda i, j: (i, 0)),
                  pl.BlockSpec((k, tn), lambda i, j: (0, j))],
        out_specs=[pl.BlockSpec((tm, tn), omap),
                   pl.BlockSpec((tn, tm), otmap)],
        out_shape=[jax.ShapeDtypeStruct((m, 3 * tn), BF16),
                   jax.ShapeDtypeStruct((2 * tn, m), BF16)],
        scratch_shapes=[pltpu.VMEM((tm, k), BF16)],
        compiler_params=_cparams(("arbitrary", "arbitrary")),
        name="in_proj",
    )(x, w)


def _gelu_tanh(x):
    c1 = math.sqrt(2.0 / math.pi)
    return x * (0.5 + 0.5 * jnp.tanh(x * (c1 + (0.044715 * c1) * (x * x))))


def _gmlp_kernel(u_ref, v_ref, ws_ref, g_ref, b_ref, bs_ref, o_ref, *, tg):
    t_chunk = lax.broadcasted_iota(I32, (GM_BLOCK, GM_BLOCK), 0) // CHUNK
    s_chunk = lax.broadcasted_iota(I32, (GM_BLOCK, GM_BLOCK), 1) // CHUNK
    causal = t_chunk >= s_chunk
    for g in range(GM_GROUPS):
        wm = jnp.where(causal, ws_ref[g], 0.0).astype(BF16)
        ln_g = g_ref[g:g + 1, :]
        ln_b = b_ref[g:g + 1, :]
        bias = bs_ref[:, g:g + 1]
        cols = slice(g * LANES, (g + 1) * LANES)
        for n in range(tg // GM_BLOCK):
            rows = slice(n * GM_BLOCK, (n + 1) * GM_BLOCK)
            v = _gelu_tanh(v_ref[rows, cols].astype(F32))
            vg = _layer_norm(v, ln_g, ln_b).astype(BF16)
            mixed = jnp.dot(wm, vg, preferred_element_type=F32) + bias
            u = _gelu_tanh(u_ref[rows, cols].astype(F32))
            o_ref[rows, cols] = (u * mixed).astype(o_ref.dtype)


def _gmlp(uvk, gm_w_s, gm_ln_g, gm_ln_b, gm_b_s_t, tg):
    s = uvk.shape[0]
    return pl.pallas_call(
        functools.partial(_gmlp_kernel, tg=tg),
        grid=(s // tg,),
        in_specs=[pl.BlockSpec((tg, GM_WIDTH), lambda i: (i, 0)),
                  pl.BlockSpec((tg, GM_WIDTH), lambda i: (i, 1)),
                  pl.BlockSpec((GM_GROUPS, GM_BLOCK, GM_BLOCK), lambda i: (0, 0, 0)),
                  pl.BlockSpec((GM_GROUPS, LANES), lambda i: (0, 0)),
                  pl.BlockSpec((GM_GROUPS, LANES), lambda i: (0, 0)),
                  pl.BlockSpec((GM_BLOCK, GM_GROUPS), lambda i: (0, 0))],
        out_specs=pl.BlockSpec((tg, GM_WIDTH), lambda i: (i, 0)),
        out_shape=jax.ShapeDtypeStruct((s, GM_WIDTH), BF16),
        compiler_params=_cparams(("arbitrary",)),
        name="gmlp",
    )(uvk, uvk, gm_w_s, gm_ln_g, gm_ln_b, gm_b_s_t)


def _softplus2(u):
    return jnp.maximum(u, 0.0) + jnp.log2(1.0 + jnp.exp2(-jnp.abs(u)))


def _sb_kernel(qt_ref, k_ref, vt_ref, o_ref, *acc_refs, tq, bk):
    q0 = pl.program_id(0) * tq
    upper = (lax.broadcasted_iota(I32, (bk, bk), 1)
             >= lax.broadcasted_iota(I32, (bk, bk), 0)).astype(BF16)
    upper2 = jnp.concatenate([upper, upper], axis=1)

    heads = range(SB_HEADS)
    hrows = [slice(h * SB_HEAD_DIM, (h + 1) * SB_HEAD_DIM) for h in heads]

    def tiles(blocks, carries):
        zs, sums, masks = [], [], []
        for ks, lo, masked in blocks:
            width = tq - lo
            zs.append([jnp.dot(k_ref[pl.ds(ks, bk), hrows[h]], qt_ref[hrows[h], lo:],
                               preferred_element_type=F32) for h in heads])
            if masked is True:
                masks.append(lax.broadcasted_iota(I32, (bk, width), 0)
                             < lax.broadcasted_iota(I32, (bk, width), 1))
            elif masked is False:
                masks.append(None)
            else:
                masks.append(jnp.broadcast_to(masked, (bk, width)))
        for zb, mask in zip(zs, masks):
            sps = [_softplus2(z) for z in zb]
            if mask is not None:
                sps = [jnp.where(mask, sp, 0.0) for sp in sps]
            sums.append([jnp.dot(upper2, jnp.concatenate(_split_bf16(sp), axis=0),
                                 preferred_element_type=F32) for sp in sps])
        csums = []
        for (ks, lo, masked), sb in zip(blocks, sums):
            cb = [sb[h] + carries[h][:, lo:] for h in heads]
            carries = [cb[h][0:1, :] if lo == 0
                       else jnp.concatenate([carries[h][:, :lo], cb[h][0:1, :]], axis=1)
                       for h in heads]
            csums.append(cb)
        probs = []
        for zb, cb, mask in zip(zs, csums, masks):
            pb = [jnp.exp2(zb[h] - cb[h]) for h in heads]
            if mask is not None:
                pb = [jnp.where(mask, a, 0.0) for a in pb]
            probs.append([a.astype(BF16) for a in pb])
        for (ks, lo, masked), pb in zip(blocks, probs):
            for h in heads:
                acc_refs[h][:, lo:] += jnp.dot(vt_ref[hrows[h], pl.ds(ks, bk)], pb[h],
                                               preferred_element_type=F32)
        return carries

    for acc_ref in acc_refs:
        acc_ref[...] = jnp.zeros_like(acc_ref)
    carries = [jnp.zeros((1, tq), F32) for _ in range(SB_HEADS)]
    blocks = [(pl.multiple_of(q0 + d * bk, bk), d * bk, True) for d in reversed(range(tq // bk))]
    for b in range(1, SB_STATIC_BELOW + 1):
        below = q0 // bk - b
        blocks.append((pl.multiple_of(jnp.maximum(below, 0) * bk, bk), 0, below >= 0))
    carries = tiles(blocks, carries)

    def cond(state):
        j = state[0]
        lowest = functools.reduce(jnp.minimum, state[1:])
        return jnp.logical_and(j >= 0, jnp.min(lowest) < SB_UNDERFLOW)

    def body(state):
        j = state[0]
        blocks = [(pl.multiple_of((j - b) * bk, bk), 0, False) for b in range(SB_LOOP_BLOCKS)]
        return (j - SB_LOOP_BLOCKS,) + tuple(tiles(blocks, list(state[1:])))

    assert (tq // bk) % SB_LOOP_BLOCKS == 0
    lax.while_loop(cond, body, (q0 // bk - 1 - SB_STATIC_BELOW,) + tuple(carries))
    for h in range(SB_HEADS):
        o_ref[hrows[h], :] = acc_refs[h][...].astype(o_ref.dtype)


def _stick_breaking(qvt, uvk, tq, bk):
    s = uvk.shape[0]
    return pl.pallas_call(
        functools.partial(_sb_kernel, tq=tq, bk=bk),
        grid=(s // tq,),
        in_specs=[pl.BlockSpec((SB_WIDTH, tq), lambda i: (0, i)),
                  _resident((s, SB_WIDTH), lambda i: (0, 2 * GM_WIDTH // SB_WIDTH)),
                  _resident((SB_WIDTH, s), lambda i: (1, 0))],
        out_specs=pl.BlockSpec((SB_WIDTH, tq), lambda i: (0, i)),
        out_shape=jax.ShapeDtypeStruct((SB_WIDTH, s), BF16),
        scratch_shapes=[pltpu.VMEM((SB_HEAD_DIM, tq), F32) for _ in range(SB_HEADS)],
        compiler_params=_cparams(("arbitrary",)),
        name="stick_breaking",
    )(qvt, uvk, qvt)


def _mixout_kernel(a_ref, bt_ref, wa_ref, wb_ref, x_ref, g_ref, beta_ref, o_ref):
    mixed = (jnp.dot(a_ref[...], wa_ref[...], preferred_element_type=F32)
             + lax.dot_general(bt_ref[...], wb_ref[...], (((0,), (0,)), ((), ())),
                               preferred_element_type=F32))
    y = DN_ALPHA * x_ref[...] + mixed
    o_ref[...] = _layer_norm(y, g_ref[...], beta_ref[...])


def _mixout(mix_a, mix_b, w, x, g, b, tm):
    s = x.shape[0]
    return pl.pallas_call(
        _mixout_kernel,
        grid=(s // tm,),
        in_specs=[pl.BlockSpec((tm, GM_WIDTH), lambda i: (i, 0)),
                  pl.BlockSpec((SB_WIDTH, tm), lambda i: (0, i)),
                  _resident((GM_WIDTH, D_MODEL), lambda i: (0, 0)),
                  _resident((SB_WIDTH, D_MODEL), lambda i: (1, 0)),
                  pl.BlockSpec((tm, D_MODEL), lambda i: (i, 0)),
                  pl.BlockSpec((1, D_MODEL), lambda i: (0, 0)),
                  pl.BlockSpec((1, D_MODEL), lambda i: (0, 0))],
        out_specs=pl.BlockSpec((tm, D_MODEL), lambda i: (i, 0)),
        out_shape=jax.ShapeDtypeStruct((s, D_MODEL), F32),
        compiler_params=_cparams(("arbitrary",)),
        name="mixout_ln1",
    )(mix_a, mix_b, w, w, x, g, b)


def _route(logits):
    lane = lax.broadcasted_iota(I32, logits.shape, 1)
    neg = jnp.float32(-jnp.inf)
    big = jnp.int32(1 << 20)

    def first_argmax(vals):
        m = jnp.max(vals, axis=-1, keepdims=True)
        idx = jnp.min(jnp.where(vals == m, lane, big), axis=-1, keepdims=True)
        return m, idx

    is_group = jnp.logical_and(lane >= N_EXPERTS, lane < N_EXPERTS + N_GROUPS)
    gl = jnp.where(is_group, logits, neg)
    gmax, glane = first_argmax(gl)
    g_val = 1.0 / jnp.sum(jnp.exp(gl - gmax), axis=-1, keepdims=True)
    gidx = glane - N_EXPERTS
    in_group = jnp.logical_and(lane >= gidx * EXPERTS_PER_GROUP,
                               lane < (gidx + 1) * EXPERTS_PER_GROUP)
    el = jnp.where(in_group, logits, neg)
    m1, i1 = first_argmax(el)
    m2, i2 = first_argmax(jnp.where(lane == i1, neg, el))
    e2 = jnp.exp(m2 - m1)
    gate1 = g_val / (1.0 + e2)
    gate2 = g_val * e2 / (1.0 + e2)
    return jnp.where(lane == 0, i1.astype(F32),
                     jnp.where(lane == 1, i2.astype(F32),
                               jnp.where(lane == 2, gate1,
                                         jnp.where(lane == 3, gate2, 0.0))))


def _memattn_kernel(h_ref, wq_ref, km_ref, vm_ref, wo_ref, g_ref, beta_ref, wr_ref, br_ref,
                    o_ref, r_ref):
    h = h_ref[...]
    q = jnp.dot(h.astype(BF16), wq_ref[...], preferred_element_type=F32).astype(BF16)
    outs = []
    for hd in range(MEM_HEADS):
        cols = slice(hd * MEM_HEAD_DIM, (hd + 1) * MEM_HEAD_DIM)
        kh = km_ref[:, cols]
        vh = vm_ref[:, cols]
        s = lax.dot_general(q[:, cols], kh, (((1,), (1,)), ((), ())),
                            preferred_element_type=F32) * (MEM_HEAD_DIM ** -0.5)
        s = s - jnp.max(s, axis=-1, keepdims=True)
        e = jnp.exp(s)
        p = e * (1.0 / jnp.sum(e, axis=-1, keepdims=True))
        outs.append(jnp.dot(p.astype(BF16), vh, preferred_element_type=F32).astype(BF16))
    o = jnp.concatenate(outs, axis=-1)
    attn = jnp.dot(o, wo_ref[...], preferred_element_type=F32)
    h2 = _layer_norm(DN_ALPHA * h + attn, g_ref[...], beta_ref[...])
    o_ref[...] = h2
    h_hi, h_lo = _split_bf16(h2)
    both = jnp.dot(h_hi, wr_ref[...], preferred_element_type=F32)
    logits = (both[:, :LANES] + both[:, LANES:]
              + jnp.dot(h_lo, wr_ref[:, :LANES], preferred_element_type=F32)) + br_ref[...]
    r_ref[...] = _route(logits)


def _memattn(h1, wq, k_mem, v_mem, wo, g, b, wr_hi_lo, b_r, tm):
    s = h1.shape[0]
    return pl.pallas_call(
        _memattn_kernel,
        grid=(s // tm,),
        in_specs=[pl.BlockSpec((tm, D_MODEL), lambda i: (i, 0)),
                  _resident((D_MODEL, D_MODEL), lambda i: (0, 0)),
                  _resident((MEM_LEN, D_MODEL), lambda i: (0, 0)),
                  _resident((MEM_LEN, D_MODEL), lambda i: (0, 0)),
                  _resident((D_MODEL, D_MODEL), lambda i: (0, 0)),
                  pl.BlockSpec((1, D_MODEL), lambda i: (0, 0)),
                  pl.BlockSpec((1, D_MODEL), lambda i: (0, 0)),
                  _resident((D_MODEL, 2 * LANES), lambda i: (0, 0)),
                  pl.BlockSpec((1, LANES), lambda i: (0, 0))],
        out_specs=[pl.BlockSpec((tm, D_MODEL), lambda i: (i, 0)),
                   pl.BlockSpec((tm, LANES), lambda i: (i, 0))],
        out_shape=[jax.ShapeDtypeStruct((s, D_MODEL), F32),
                   jax.ShapeDtypeStruct((s, LANES), F32)],
        compiler_params=_cparams(("arbitrary",)),
        name="memattn_ln2_router",
    )(h1, wq, k_mem, v_mem, wo, g, b, wr_hi_lo, b_r)


def _dispatch_kernel(e_ref, tok_ref, pos_ref, meta_ref):
    ca = DISPATCH_CHUNK
    n_chunks = N_ASSIGN // ca
    e_iota = lax.broadcasted_iota(I32, (N_EXPERTS, ca), 0)

    def onehot(c):
        e_c = e_ref[:, pl.ds(pl.multiple_of(c * ca, ca), ca)]
        return (e_iota == e_c).astype(F32)

    counts = lax.fori_loop(
        0, n_chunks, lambda c, acc: acc + jnp.sum(onehot(c), axis=1, keepdims=True),
        jnp.zeros((N_EXPERTS, 1), F32))
    nblk = jnp.right_shift(counts.astype(I32) + (MOE_ROWS - 1), MOE_ROWS_LOG2)
    strict_lower = (lax.broadcasted_iota(I32, (N_EXPERTS, N_EXPERTS), 1)
                    < lax.broadcasted_iota(I32, (N_EXPERTS, N_EXPERTS), 0)).astype(BF16)
    nblk_lanes = jnp.broadcast_to(nblk.astype(F32), (N_EXPERTS, LANES)).astype(BF16)
    blk_start = jnp.dot(strict_lower, nblk_lanes, preferred_element_type=F32)
    pad_start = blk_start[:, 0:1] * MOE_ROWS
    lane = lax.broadcasted_iota(I32, (N_EXPERTS, LANES), 1)
    meta_ref[...] = jnp.where(lane == 0, blk_start.astype(I32),
                              jnp.where(lane == 1, jnp.broadcast_to(nblk, (N_EXPERTS, LANES)), 0))

    earlier = (lax.broadcasted_iota(I32, (ca, ca), 0)
               < lax.broadcasted_iota(I32, (ca, ca), 1)).astype(BF16)
    blk_iota = lax.broadcasted_iota(I32, (N_ROW_BLOCKS, ca), 0)
    row_iota = lax.broadcasted_iota(I32, (MOE_ROWS, ca), 0)
    a_iota = lax.broadcasted_iota(I32, (1, ca), 1)

    def body(c, state):
        seen, acc_hi, acc_lo = state
        oh = onehot(c)
        before = jnp.dot(oh.astype(BF16), earlier, preferred_element_type=F32) + seen
        dest = jnp.sum(oh * (before + pad_start), axis=0, keepdims=True).astype(I32)
        pos_ref[:, pl.ds(pl.multiple_of(c * ca, ca), ca)] = dest
        in_blk = (blk_iota == jnp.right_shift(dest, MOE_ROWS_LOG2)).astype(F32)
        in_row = (row_iota == jnp.bitwise_and(dest, MOE_ROWS - 1)).astype(BF16)
        tok = jnp.right_shift(c * ca + a_iota, 1)
        tok_hi = (jnp.right_shift(tok, 7) + 1).astype(F32)
        tok_lo = jnp.bitwise_and(tok, 127).astype(F32)
        nt = (((1,), (1,)), ((), ()))
        acc_hi = acc_hi + lax.dot_general((in_blk * tok_hi).astype(BF16), in_row, nt,
                                          preferred_element_type=F32)
        acc_lo = acc_lo + lax.dot_general((in_blk * tok_lo).astype(BF16), in_row, nt,
                                          preferred_element_type=F32)
        return seen + jnp.sum(oh, axis=1, keepdims=True), acc_hi, acc_lo

    zeros = jnp.zeros((N_ROW_BLOCKS, MOE_ROWS), F32)
    _, acc_hi, acc_lo = lax.fori_loop(0, n_chunks, body,
                                      (jnp.zeros((N_EXPERTS, 1), F32), zeros, zeros))
    pad_tok = jnp.bitwise_and(
        lax.broadcasted_iota(I32, (N_ROW_BLOCKS, MOE_ROWS), 0) * MOE_ROWS
        + lax.broadcasted_iota(I32, (N_ROW_BLOCKS, MOE_ROWS), 1), SEQ - 1)
    tok_ref[...] = jnp.where(acc_hi > 0.5, ((acc_hi - 1.0) * 128.0 + acc_lo).astype(I32), pad_tok)


def _dispatch(flat_e):
    return pl.pallas_call(
        _dispatch_kernel,
        out_shape=[jax.ShapeDtypeStruct((N_ROW_BLOCKS, MOE_ROWS), I32),
                   jax.ShapeDtypeStruct((1, N_ASSIGN), I32),
                   jax.ShapeDtypeStruct((N_EXPERTS, LANES), I32)],
        compiler_params=pltpu.CompilerParams(vmem_limit_bytes=VMEM_LIMIT),
        name="moe_dispatch",
    )(flat_e)


def _expert_kernel(bs_ref, nb_ref, tot_ref, tok_hbm, h_hbm, w1_hbm, w3_hbm, w2_hbm, y_hbm,
                   idx, xbuf, ybuf, isem, gsem, ysem, wsem, w1f, w3f, w2f, w1b, w3b, w2b):
    e = pl.program_id(0)
    n_exp = pl.num_programs(0)
    total = tot_ref[0]
    first = bs_ref[e]
    nblk = nb_ref[e]

    def idx_copy(blk):
        row = jnp.minimum(blk, N_ROW_BLOCKS - 1)
        slot = blk % GATHER_SLOTS
        return pltpu.make_async_copy(tok_hbm.at[pl.ds(row, 1)], idx.at[pl.ds(slot, 1)],
                                     isem.at[slot])

    def issue_rows(blk, rows):
        slot = blk % GATHER_SLOTS
        for r in rows:
            tok = idx[slot, r]
            pltpu.make_async_copy(h_hbm.at[pl.ds(tok, 1)], xbuf.at[slot, pl.ds(r, 1)],
                                  gsem.at[slot]).start()

    def wait_gather(blk):
        slot = blk % GATHER_SLOTS
        pltpu.make_async_copy(h_hbm.at[pl.ds(0, MOE_ROWS)], xbuf.at[slot], gsem.at[slot]).wait()

    def y_copy(blk, slot):
        return pltpu.make_async_copy(ybuf.at[slot], y_hbm.at[pl.ds(blk * MOE_ROWS, MOE_ROWS)],
                                     ysem.at[slot])

    def weight_copies(ex):
        slot = ex % WEIGHT_SLOTS
        return [pltpu.make_async_copy(src.at[ex], dst.at[slot], wsem.at[slot])
                for src, dst in ((w1_hbm, w1f), (w3_hbm, w3f), (w2_hbm, w2f))]

    @pl.when(e == 0)
    def _():
        for ex in range(WEIGHT_SLOTS - 1):
            for cp in weight_copies(ex):
                cp.start(priority=1)

        for blk in range(GATHER_SLOTS):
            idx_copy(blk).start()
        for blk in range(GATHER_SLOTS - 1):
            idx_copy(blk).wait()

        def prime(r, c):
            for blk in range(GATHER_SLOTS - 1):
                pltpu.make_async_copy(h_hbm.at[pl.ds(idx[blk, r], 1)], xbuf.at[blk, pl.ds(r, 1)],
                                      gsem.at[blk]).start()
            return c
        lax.fori_loop(0, MOE_ROWS, prime, 0, unroll=8)

    @pl.when(e + WEIGHT_SLOTS - 1 < n_exp)
    def _():
        for cp in weight_copies(e + WEIGHT_SLOTS - 1):
            cp.start(priority=1)

    for cp in weight_copies(e):
        cp.wait()

    @pl.when(nblk > 0)
    def _():
        wslot = e % WEIGHT_SLOTS
        w1b[...] = w1f[wslot].astype(BF16)
        w3b[...] = w3f[wslot].astype(BF16)
        w2b[...] = w2f[wslot].astype(BF16)

    n_piece = 4
    rows_per_piece = MOE_ROWS // n_piece
    cols_per_piece = D_MODEL // n_piece

    def block(b, c):
        blk = first + b
        slot = blk % 2
        wait_gather(blk)
        idx_copy(blk + GATHER_SLOTS - 1).wait()
        idx_copy(blk + GATHER_SLOTS).start()

        @pl.when(blk >= 2)
        def _():
            y_copy(blk - 2, slot).wait()

        xb = xbuf[blk % GATHER_SLOTS].astype(BF16)
        a = jnp.dot(xb, w1b[...], preferred_element_type=F32)
        g = jnp.dot(xb, w3b[...], preferred_element_type=F32)
        hid = (a * jax.nn.sigmoid(a) * g).astype(BF16)
        for p in range(n_piece):
            issue_rows(blk + GATHER_SLOTS - 1,
                       range(p * rows_per_piece, (p + 1) * rows_per_piece))
            cols = slice(p * cols_per_piece, (p + 1) * cols_per_piece)
            ybuf[slot, :, cols] = jnp.dot(hid, w2b[:, cols], preferred_element_type=F32)
        y_copy(blk, slot).start()
        return c

    lax.fori_loop(0, nblk, block, 0)

    @pl.when(e == n_exp - 1)
    def _():
        for ahead in range(GATHER_SLOTS - 1):
            wait_gather(total + ahead)
        idx_copy(total + GATHER_SLOTS - 1).wait()
        for back in (1, 2):
            @pl.when(total >= back)
            def _():
                y_copy(total - back, (total - back) % 2).wait()

        ybuf[0] = jnp.zeros((MOE_ROWS, D_MODEL), F32)

        def start_zero(blk, c):
            y_copy(blk, 0).start()
            return c

        def wait_zero(blk, c):
            y_copy(blk, 0).wait()
            return c

        lax.fori_loop(total, N_ROW_BLOCKS, start_zero, 0)
        lax.fori_loop(total, N_ROW_BLOCKS, wait_zero, 0)


def _experts(blk_start, nblk, total, row_tok, h2, w1, w3, w2):
    grid_spec = pltpu.PrefetchScalarGridSpec(
        num_scalar_prefetch=3,
        grid=(N_EXPERTS,),
        in_specs=[pl.BlockSpec(memory_space=pl.ANY)] * 5,
        out_specs=pl.BlockSpec(memory_space=pl.ANY),
        scratch_shapes=[pltpu.SMEM((GATHER_SLOTS, MOE_ROWS), I32),
                        pltpu.VMEM((GATHER_SLOTS, MOE_ROWS, D_MODEL), F32),
                        pltpu.VMEM((2, MOE_ROWS, D_MODEL), F32),
                        pltpu.SemaphoreType.DMA((GATHER_SLOTS,)),
                        pltpu.SemaphoreType.DMA((GATHER_SLOTS,)),
                        pltpu.SemaphoreType.DMA((2,)),
                        pltpu.SemaphoreType.DMA((WEIGHT_SLOTS,)),
                        pltpu.VMEM((WEIGHT_SLOTS, D_MODEL, D_EXPERT), F32),
                        pltpu.VMEM((WEIGHT_SLOTS, D_MODEL, D_EXPERT), F32),
                        pltpu.VMEM((WEIGHT_SLOTS, D_EXPERT, D_MODEL), F32),
                        pltpu.VMEM((D_MODEL, D_EXPERT), BF16),
                        pltpu.VMEM((D_MODEL, D_EXPERT), BF16),
                        pltpu.VMEM((D_EXPERT, D_MODEL), BF16)],
    )
    return pl.pallas_call(
        _expert_kernel,
        grid_spec=grid_spec,
        out_shape=jax.ShapeDtypeStruct((N_ROWS, D_MODEL), F32),
        compiler_params=_cparams(("arbitrary",)),
        name="experts",
    )(blk_start, nblk, total, row_tok, h2, w1, w3, w2)


def _combine_kernel(y_hbm, pos_ref, pos_next_ref, h_ref, r_ref, g_ref, beta_ref, o_ref, ybuf, sem,
                    *, tc):
    i = pl.program_id(0)
    n = pl.num_programs(0)
    slot = i % 2

    def issue_rows(table, slot_, rows):
        for r in rows:
            for kk in range(TOP_K):
                p = table[0, 0, r * TOP_K + kk]
                pltpu.make_async_copy(y_hbm.at[pl.ds(p, 1)], ybuf.at[slot_, kk, pl.ds(r, 1)],
                                      sem.at[slot_]).start(priority=kk)

    def wait_rows(slot_):
        for kk in range(TOP_K):
            pltpu.make_async_copy(y_hbm.at[pl.ds(0, tc)], ybuf.at[slot_, kk], sem.at[slot_]).wait()

    @pl.when(i == 0)
    def _():
        issue_rows(pos_ref, 0, range(tc))

    def step(slot_):
        wait_rows(slot_)
        for c in range(tc // COMBINE_CHUNK):
            rows = slice(c * COMBINE_CHUNK, (c + 1) * COMBINE_CHUNK)
            issue_rows(pos_next_ref, 1 - slot_, range(c * COMBINE_CHUNK, (c + 1) * COMBINE_CHUNK))
            gates = r_ref[rows, :]
            moe = (ybuf[slot_, 0, rows, :] * gates[:, TOP_K:TOP_K + 1]
                   + ybuf[slot_, 1, rows, :] * gates[:, TOP_K + 1:TOP_K + 2])
            o_ref[rows, :] = _layer_norm(DN_ALPHA * h_ref[rows, :] + moe, g_ref[...], beta_ref[...])

    for parity in range(2):
        @pl.when(slot == parity)
        def _():
            step(parity)

    @pl.when(i == n - 1)
    def _():
        for parity in range(2):
            @pl.when(slot == parity)
            def _():
                wait_rows(1 - parity)


def _combine(pos, y_rows, h2, rout, g, b, tc):
    s = h2.shape[0]
    n = s // tc
    pos = pos.reshape(n, 1, tc * TOP_K)
    smem_block = functools.partial(pl.BlockSpec, (1, 1, tc * TOP_K), memory_space=pltpu.SMEM)
    return pl.pallas_call(
        functools.partial(_combine_kernel, tc=tc),
        grid=(n,),
        in_specs=[pl.BlockSpec(memory_space=pl.ANY),
                  smem_block(lambda i: (i, 0, 0)),
                  smem_block(lambda i: (jnp.minimum(i + 1, n - 1), 0, 0)),
                  pl.BlockSpec((tc, D_MODEL), lambda i: (i, 0)),
                  pl.BlockSpec((tc, LANES), lambda i: (i, 0)),
                  pl.BlockSpec((1, D_MODEL), lambda i: (0, 0)),
                  pl.BlockSpec((1, D_MODEL), lambda i: (0, 0))],
        out_specs=pl.BlockSpec((tc, D_MODEL), lambda i: (i, 0)),
        out_shape=jax.ShapeDtypeStruct((s, D_MODEL), F32),
        scratch_shapes=[pltpu.VMEM((2, TOP_K, tc, D_MODEL), F32),
                        pltpu.SemaphoreType.DMA((2,))],
        compiler_params=_cparams(("arbitrary",)),
        name="combine_ln3",
    )(y_rows, pos, pos, h2, rout, g, b)


def kernel(x, mem, w_in, gm_ln_g, gm_ln_b, gm_w_s, gm_b_s, w_mix_out, ln1_g, ln1_b,
           mem_w_q, mem_w_k, mem_w_v, mem_w_o, ln2_g, ln2_b,
           w_group, b_group, w_router, b_router, w1, w3, w2, ln3_g, ln3_b):
    assert x.shape == (1, SEQ, D_MODEL) and w_in.shape[0] == 1
    xs = x[0]
    l = 0
    w_mix = w_mix_out[l].astype(BF16)
    w_q = mem_w_q[l].astype(BF16)
    w_o = mem_w_o[l].astype(BF16)
    w_r = jnp.concatenate(
        [w_router[l].transpose(1, 0, 2).reshape(D_MODEL, N_EXPERTS), w_group[l],
         jnp.zeros((D_MODEL, LANES - N_EXPERTS - N_GROUPS), F32)], axis=1)
    wr_hi_lo = jnp.concatenate(_split_bf16(w_r), axis=1)
    b_r = jnp.concatenate([b_router[l].reshape(-1), b_group[l],
                           jnp.zeros((LANES - N_EXPERTS - N_GROUPS,), F32)]).reshape(1, LANES)
    row = lambda v: v.reshape(1, D_MODEL)

    uvk, qvt = _in_proj(xs, w_in[l], tm=1024, q_scale=SB_HEAD_DIM ** -0.5 * math.log2(math.e))
    mix_a = _gmlp(uvk, gm_w_s[l], gm_ln_g[l], gm_ln_b[l], gm_b_s[l].T, tg=1024)
    mix_b = _stick_breaking(qvt, uvk, tq=256, bk=128)
    h1 = _mixout(mix_a, mix_b, w_mix, xs, row(ln1_g[l]), row(ln1_b[l]), tm=512)
    k_mem = _proj(mem[0], mem_w_k[l], tm=MEM_LEN, tn=1024)
    v_mem = _proj(mem[0], mem_w_v[l], tm=MEM_LEN, tn=1024)
    h2, rout = _memattn(h1, w_q, k_mem, v_mem, w_o, row(ln2_g[l]), row(ln2_b[l]), wr_hi_lo, b_r, tm=512)
    flat_e = rout[:, 0:TOP_K].astype(I32).reshape(1, N_ASSIGN)
    row_tok, pos, meta = _dispatch(flat_e)
    blk_start, nblk = meta[:, 0], meta[:, 1]
    total = (blk_start[N_EXPERTS - 1] + nblk[N_EXPERTS - 1]).reshape(1)
    y_rows = _experts(blk_start, nblk, total, row_tok, h2, w1[l], w3[l], w2[l])
    out = _combine(pos.reshape(N_ASSIGN), y_rows, h2, rout, row(ln3_g[l]), row(ln3_b[l]), tc=256)
    return out.reshape(1, SEQ, D_MODEL)
```
